```python
import math
import jax, jax.numpy as jnp
from jax import lax
import numpy as np

D_MODEL = 1024
BATCH = 2
SEQ = 16384
DEPTH = 4

GRID_W = 64
CTX_LEN = 256
N_MIXERS = 2
MLA_HEADS = 16
QK_NOPE = 64
QK_ROPE = 32
V_DIM = 64
Q_RANK = 256
KV_RANK = 128
ROPE_BASE = 10000.0
Q_BLOCK = 128
HY_ORDER = 2
HY_EMB = 33
HY_BANDS = (HY_EMB - 1) // 2
HY_FILTER_HIDDEN = 64
HY_SHORT = 3
HY_TARGET = 1e-2
HY_FAST_PCT = 0.3
HY_SLOW_PCT = 1.5
D_FF = 4 * D_MODEL
EPS = 1e-6
N_MLA = (DEPTH + 1) // 2
N_HY = DEPTH // 2

kernel_name = "hybrid_mla_hyena_dit_prefix"


def rmsnorm(x, g):
    xf = x.astype(jnp.float32)
    y = xf * lax.rsqrt(jnp.mean(xf * xf, axis=-1, keepdims=True) + EPS)
    return (y * g.astype(jnp.float32)).astype(x.dtype)


def modulate(h, shift, scale):
    return h * (1.0 + scale) + shift


def axial_rope_tables(L):
    rows = L // GRID_W
    t = jnp.arange(rows * GRID_W)
    row = (t // GRID_W).astype(jnp.float32)
    col = (t % GRID_W).astype(jnp.float32)
    n_freq = QK_ROPE // 4
    inv = ROPE_BASE ** (-jnp.arange(n_freq, dtype=jnp.float32) / n_freq)
    ang = jnp.concatenate([row[:, None] * inv, col[:, None] * inv], axis=-1)
    return jnp.cos(ang), jnp.sin(ang)


def apply_rope(x, cos, sin):
    xf = x.astype(jnp.float32)
    x1, x2 = xf[..., : QK_ROPE // 2], xf[..., QK_ROPE // 2:]
    out = jnp.concatenate([x1 * cos - x2 * sin, x1 * sin + x2 * cos], axis=-1)
    return out.astype(x.dtype)


def mla_qkv(h, w_dq, g_q, w_uq, w_dkv, g_kv, w_ukv):
    B, L, _ = h.shape
    cq = rmsnorm(h @ w_dq, g_q)
    q = (cq @ w_uq).reshape(B, L, MLA_HEADS, QK_NOPE + QK_ROPE)
    kv_a = h @ w_dkv
    ckv = rmsnorm(kv_a[..., :KV_RANK], g_kv)
    k_rope = kv_a[..., KV_RANK:]
    kv = (ckv @ w_ukv).reshape(B, L, MLA_HEADS, QK_NOPE + V_DIM)
    return q[..., :QK_NOPE], q[..., QK_NOPE:], kv[..., :QK_NOPE], k_rope, kv[..., QK_NOPE:]


def attend(q_nope, q_rope, k_nope, k_rope, v):
    s = jnp.einsum('bqhd,bkhd->bhqk', q_nope, k_nope) + jnp.einsum('bqhr,bkr->bhqk', q_rope, k_rope)
    s = s.astype(jnp.float32) * (1.0 / math.sqrt(QK_NOPE + QK_ROPE))
    p = jax.nn.softmax(s, axis=-1).astype(v.dtype)
    return jnp.einsum('bhqk,bkhd->bqhd', p, v)


def mla_mixer(h_ctx, h_lat, w_dq, g_q, w_uq, w_dkv, g_kv, w_ukv, w_o, cos, sin, need_ctx):
    B, L, _ = h_lat.shape
    CL = h_ctx.shape[1]
    qn_c, qr_c, kn_c, kr_c, v_c = mla_qkv(h_ctx, w_dq, g_q, w_uq, w_dkv, g_kv, w_ukv)
    qn_l, qr_l, kn_l, kr_l, v_l = mla_qkv(h_lat, w_dq, g_q, w_uq, w_dkv, g_kv, w_ukv)
    qr_l = apply_rope(qr_l, cos[:, None, :], sin[:, None, :])
    kr_l = apply_rope(kr_l, cos, sin)
    kn = jnp.concatenate([kn_c, kn_l], axis=1)
    kr = jnp.concatenate([kr_c, kr_l], axis=1)
    v = jnp.concatenate([v_c, v_l], axis=1)
    nb = L // Q_BLOCK
    qn_b = qn_l.reshape(B, nb, Q_BLOCK, MLA_HEADS, QK_NOPE).swapaxes(0, 1)
    qr_b = qr_l.reshape(B, nb, Q_BLOCK, MLA_HEADS, QK_ROPE).swapaxes(0, 1)
    o = lax.map(lambda q: attend(q[0], q[1], kn, kr, v), (qn_b, qr_b))
    y_lat = o.swapaxes(0, 1).reshape(B, L, MLA_HEADS * V_DIM) @ w_o
    if need_ctx:
        o_c = attend(qn_c, qr_c, kn_c, kr_c, v_c)
        y_ctx = o_c.reshape(B, CL, MLA_HEADS * V_DIM) @ w_o
    else:
        y_ctx = None
    return y_ctx, y_lat


def short_conv(u, w, b):
    up = jnp.pad(u, ((0, 0), (1, 1), (0, 0)))
    return up[:, :-2] * w[0] + up[:, 1:-1] * w[1] + up[:, 2:] * w[2] + b


def hyena_filters(L, w1, b1, w2, b2, w3, freq):
    f32 = jnp.float32
    t = jnp.linspace(0.0, 1.0, L, dtype=f32)[:, None]
    w = (2.0 * math.pi) * jnp.arange(L, dtype=f32)[:, None] / L
    f = jnp.linspace(1e-4, HY_BANDS - 1, HY_BANDS, dtype=f32)[None, :]
    z = jnp.concatenate([t, jnp.cos(f * w), -jnp.sin(f * w)], axis=-1)
    fr = freq.astype(f32)
    hdn = jnp.sin(fr * (z @ w1.astype(f32) + b1.astype(f32)))
    hdn = jnp.sin(fr * (hdn @ w2.astype(f32) + b2.astype(f32)))
    hf = hdn @ w3.astype(f32)
    min_decay = math.log(HY_TARGET) / HY_SLOW_PCT
    max_decay = math.log(HY_TARGET) / HY_FAST_PCT
    deltas = jnp.linspace(min_decay, max_decay, D_MODEL, dtype=f32)
    decay = jnp.exp(-t * jnp.abs(deltas)[None, :])
    hf = hf.reshape(L, HY_ORDER, 2, D_MODEL) * decay[:, None, None, :]
    return hf.transpose(1, 2, 0, 3)


def long_conv(u, h_f, h_b, bias):
    L, D = h_f.shape
    k = jnp.concatenate([h_f, jnp.zeros((1, D), h_f.dtype), h_b[:0:-1]], axis=0)
    K = jnp.fft.rfft(k, n=2 * L, axis=0)
    uf = u.astype(jnp.float32)
    U = jnp.fft.rfft(uf, n=2 * L, axis=1)
    y = jnp.fft.irfft(U * K[None], n=2 * L, axis=1)[:, :L]
    return (y + uf * bias.astype(jnp.float32)).astype(u.dtype)


def hyena_mixer(h, w_in, b_in, w_short, b_short, f_w1, f_b1, f_w2, f_b2, f_w3, f_freq, bias, w_out, b_out):
    L = h.shape[1]
    u = short_conv(h @ w_in + b_in, w_short, b_short)
    x1, x2, v = jnp.split(u, 3, axis=-1)
    filt = hyena_filters(L, f_w1, f_b1, f_w2, f_b2, f_w3, f_freq)
    z = x1 * long_conv(v, filt[0, 0], filt[0, 1], bias[0])
    z = x2 * long_conv(z, filt[1, 0], filt[1, 1], bias[1])
    return z @ w_out + b_out


def sq_relu_mlp(h, w1, w2):
    return jnp.square(jax.nn.relu(h @ w1)) @ w2


def setup_inputs(seed: int = 0) -> dict:
    key = jax.random.key(seed)
    ks = iter(jax.random.split(key, 40))
    f32 = jnp.float32
    D = D_MODEL

    def nrm(shape, scale):
        return jax.random.normal(next(ks), shape, f32) * scale

    return {
        "x": nrm((BATCH, SEQ, D), 1.0),
        "c": nrm((BATCH, D), 1.0),
        "ctx": nrm((BATCH, CTX_LEN, D), 1.0),
        "c_ctx": nrm((D,), 1.0),
        "ada_w": nrm((DEPTH, D, 6 * D), 0.5 * D ** -0.5),
        "ada_b": nrm((DEPTH, 6 * D), 0.02),
        "norm_g": 1.0 + nrm((DEPTH, 4, D), 0.05),
        "mla_w_dq": nrm((N_MLA, D, Q_RANK), D ** -0.5),
        "mla_g_q": 1.0 + nrm((N_MLA, Q_RANK), 0.05),
        "mla_w_uq": nrm((N_MLA, Q_RANK, MLA_HEADS * (QK_NOPE + QK_ROPE)), Q_RANK ** -0.5),
        "mla_w_dkv": nrm((N_MLA, D, KV_RANK + QK_ROPE), D ** -0.5),
        "mla_g_kv": 1.0 + nrm((N_MLA, KV_RANK), 0.05),
        "mla_w_ukv": nrm((N_MLA, KV_RANK, MLA_HEADS * (QK_NOPE + V_DIM)), KV_RANK ** -0.5),
        "mla_w_o": nrm((N_MLA, MLA_HEADS * V_DIM, D), (MLA_HEADS * V_DIM) ** -0.5),
        "hy_w_in": nrm((N_HY, D, 3 * D), D ** -0.5),
        "hy_b_in": nrm((N_HY, 3 * D), 0.02),
        "hy_w_short": nrm((N_HY, HY_SHORT, 3 * D), HY_SHORT ** -0.5),
        "hy_b_short": nrm((N_HY, 3 * D), 0.02),
        "hy_f_w1": nrm((N_HY, HY_EMB, HY_FILTER_HIDDEN), HY_EMB ** -0.5),
        "hy_f_b1": nrm((N_HY, HY_FILTER_HIDDEN), 0.02),
        "hy_f_w2": nrm((N_HY, HY_FILTER_HIDDEN, HY_FILTER_HIDDEN), HY_FILTER_HIDDEN ** -0.5),
        "hy_f_b2": nrm((N_HY, HY_FILTER_HIDDEN), 0.02),
        "hy_f_w3": nrm((N_HY, HY_FILTER_HIDDEN, HY_ORDER * 2 * D), HY_FILTER_HIDDEN ** -0.5),
        "hy_f_freq": 1.0 + nrm((N_HY, HY_FILTER_HIDDEN), 0.05),
        "hy_bias": nrm((N_HY, HY_ORDER, D), 0.5),
        "hy_w_out": nrm((N_HY, D, D), D ** -0.5),
        "hy_b_out": nrm((N_HY, D), 0.02),
        "mlp_w1": nrm((DEPTH, D, D_FF), D ** -0.5),
        "mlp_w2": nrm((DEPTH, D_FF, D), D_FF ** -0.5),
    }


def reference(x, c, ctx, c_ctx, ada_w, ada_b, norm_g, mla_w_dq, mla_g_q, mla_w_uq, mla_w_dkv,
              mla_g_kv, mla_w_ukv, mla_w_o, hy_w_in, hy_b_in, hy_w_short, hy_b_short, hy_f_w1,
              hy_f_b1, hy_f_w2, hy_f_b2, hy_f_w3, hy_f_freq, hy_bias, hy_w_out, hy_b_out,
              mlp_w1, mlp_w2):
    L = x.shape[1]
    cos, sin = axial_rope_tables(L)
    s_lat = jax.nn.silu(c)
    s_ctx = jax.nn.silu(c_ctx)
    xc = ctx
    for i in range(DEPTH):
        last = i == DEPTH - 1
        j = i // N_MIXERS
        g_pre1, g_post1, g_pre2, g_post2 = norm_g[i, 0], norm_g[i, 1], norm_g[i, 2], norm_g[i, 3]
        sh1, sc1, gt1, sh2, sc2, gt2 = jnp.split((s_lat @ ada_w[i] + ada_b[i])[:, None, :], 6, axis=-1)
        csh1, csc1, cgt1, csh2, csc2, cgt2 = jnp.split(s_ctx @ ada_w[i] + ada_b[i], 6, axis=-1)

        h_lat = modulate(rmsnorm(x, g_pre1), sh1, sc1)
        h_ctx = modulate(rmsnorm(xc, g_pre1), csh1, csc1)
        if i % N_MIXERS == 0:
            y_ctx, y_lat = mla_mixer(h_ctx, h_lat, mla_w_dq[j], mla_g_q[j], mla_w_uq[j], mla_w_dkv[j],
                                     mla_g_kv[j], mla_w_ukv[j], mla_w_o[j], cos, sin, not last)
        else:
            hy_args = (hy_w_in[j], hy_b_in[j], hy_w_short[j], hy_b_short[j], hy_f_w1[j], hy_f_b1[j],
                       hy_f_w2[j], hy_f_b2[j], hy_f_w3[j], hy_f_freq[j], hy_bias[j], hy_w_out[j], hy_b_out[j])
            y_lat = hyena_mixer(h_lat, *hy_args)
            y_ctx = None if last else hyena_mixer(h_ctx, *hy_args)
        x = x + gt1 * rmsnorm(y_lat, g_post1)

        m_lat = sq_relu_mlp(modulate(rmsnorm(x, g_pre2), sh2, sc2), mlp_w1[i], mlp_w2[i])
        x = x + gt2 * rmsnorm(m_lat, g_post2)

        if not last:
            xc = xc + cgt1 * rmsnorm(y_ctx, g_post1)
            m_ctx = sq_relu_mlp(modulate(rmsnorm(xc, g_pre2), csh2, csc2), mlp_w1[i], mlp_w2[i])
            xc = xc + cgt2 * rmsnorm(m_ctx, g_post2)
    return x
```

```python
import functools
import math

import numpy as np
import jax
import jax.numpy as jnp
from jax import lax
from jax.experimental import pallas as pl
from jax.experimental.pallas import tpu as pltpu

F32 = jnp.float32
BF16 = jnp.bfloat16

GRID_W = 64
MLA_HEADS = 16
QK_NOPE = 64
QK_ROPE = 32
V_DIM = 64
Q_RANK = 256
KV_RANK = 128
ROPE_BASE = 10000.0
HY_ORDER = 2
HY_EMB = 33
HY_BANDS = (HY_EMB - 1) // 2
HY_FILTER_HIDDEN = 64
HY_TARGET = 1e-2
HY_FAST_PCT = 0.3
HY_SLOW_PCT = 1.5
EPS = 1e-6

LANES = 128
HEAD_PAD = LANES
VT_ROWS = 80
ONES_ROW = V_DIM
VMEM_LIMIT = 56 * 1024 * 1024
DFT_N2 = 128

LOG2E = 1.4426950408889634


def _cparams(*sem):
    return pltpu.CompilerParams(dimension_semantics=sem, vmem_limit_bytes=VMEM_LIMIT)


def _dot(a, b):
    return jnp.dot(a, b, preferred_element_type=F32)


def _dot3(a, b):
    ah = a.astype(BF16)
    al = (a - ah.astype(F32)).astype(BF16)
    bh = b.astype(BF16)
    bl = (b - bh.astype(F32)).astype(BF16)
    return _dot(ah, bh) + _dot(al, bh) + _dot(ah, bl)


def _rms(x, g):
    return x * lax.rsqrt(jnp.mean(x * x, axis=-1, keepdims=True) + EPS) * g


def _ada_kernel(c_ref, w_ref, b_ref, o_ref):
    c = c_ref[...]
    s = c / (1.0 + jnp.exp(-c))
    o_ref[0] = _dot3(s, w_ref[0]) + b_ref[0]


def _ada(cvec, ada_w, ada_b):
    depth, d, n6 = ada_w.shape
    tn = 1536
    return pl.pallas_call(
        _ada_kernel,
        grid=(depth, n6 // tn),
        in_specs=[
            pl.BlockSpec((8, d), lambda i, j: (0, 0)),
            pl.BlockSpec((1, d, tn), lambda i, j: (i, 0, j)),
            pl.BlockSpec((1, 1, tn), lambda i, j: (i, 0, j)),
        ],
        out_specs=pl.BlockSpec((1, 8, tn), lambda i, j: (i, 0, j)),
        out_shape=jax.ShapeDtypeStruct((depth, 8, n6), F32),
        compiler_params=_cparams("parallel", "parallel"),
        name="ada",
    )(cvec, ada_w, ada_b.reshape(depth, 1, n6))


def _qkv_kernel(x_ref, sh_ref, sc_ref, g_ref, wd_ref, gq_ref, gkv_ref, wqa_ref, wqb_ref, wk_ref,
                wvt_ref, cos_ref, sin_ref, q_ref, k_ref, vt_ref, *, qscale):
    h = _rms(x_ref[0], g_ref[...]) * (1.0 + sc_ref[0]) + sh_ref[0]
    t = _dot(h.astype(BF16), wd_ref[...])
    cq = _rms(t[:, :Q_RANK], gq_ref[...]).astype(BF16)
    ckv = _rms(t[:, Q_RANK:Q_RANK + KV_RANK], gkv_ref[...]).astype(BF16)
    cos = cos_ref[...]
    sin = sin_ref[...]
    o = Q_RANK + KV_RANK
    kr = t[:, o:o + LANES] * cos + t[:, o + LANES:o + 2 * LANES] * sin
    qa = _dot(cq, wqa_ref[...])
    qb = _dot(cq, wqb_ref[...])
    kn = _dot(ckv, wk_ref[...])
    for hd in range(MLA_HEADS):
        sl = slice(hd * HEAD_PAD, (hd + 1) * HEAD_PAD)
        q_ref[0, :, sl] = ((qa[:, sl] * cos + qb[:, sl] * sin) * qscale).astype(BF16)
        k_ref[0, :, sl] = (kn[:, sl] + kr).astype(BF16)
    vt = lax.dot_general(wvt_ref[...], ckv, (((1,), (1,)), ((), ())), preferred_element_type=F32)
    tm = vt.shape[1]
    vt = vt.reshape(MLA_HEADS, VT_ROWS, tm)
    ones = lax.broadcasted_iota(jnp.int32, vt.shape, 1) == ONES_ROW
    vt_ref[0, 0] = jnp.where(ones, 1.0, vt).astype(BF16)


def _qkv(x, sh, sc, g, w, cos_t, sin_t, tm):
    b, l, d = x.shape
    nt = l // tm
    hw = MLA_HEADS * HEAD_PAD
    qscale = LOG2E / math.sqrt(QK_NOPE + QK_ROPE)
    full = lambda a: pl.BlockSpec(a.shape, lambda bi, i: (0,) * a.ndim)
    return pl.pallas_call(
        functools.partial(_qkv_kernel, qscale=qscale),
        grid=(b, nt),
        in_specs=[
            pl.BlockSpec((1, tm, d), lambda bi, i: (bi, i, 0)),
            pl.BlockSpec((1, 1, d), lambda bi, i: (bi, 0, 0)),
            pl.BlockSpec((1, 1, d), lambda bi, i: (bi, 0, 0)),
            full(g), full(w["wd"]), full(w["gq"]), full(w["gkv"]), full(w["wqa"]), full(w["wqb"]),
            full(w["wk"]), full(w["wvt"]),
            pl.BlockSpec((tm, LANES), lambda bi, i: (i, 0)),
            pl.BlockSpec((tm, LANES), lambda bi, i: (i, 0)),
        ],
        out_specs=[
            pl.BlockSpec((1, tm, hw), lambda bi, i: (bi, i, 0)),
            pl.BlockSpec((1, tm, hw), lambda bi, i: (bi, i, 0)),
            pl.BlockSpec((1, 1, MLA_HEADS, VT_ROWS, tm), lambda bi, i: (bi, i, 0, 0, 0)),
        ],
        out_shape=[
            jax.ShapeDtypeStruct((b, l, hw), BF16),
            jax.ShapeDtypeStruct((b, l, hw), BF16),
            jax.ShapeDtypeStruct((b, nt, MLA_HEADS, VT_ROWS, tm), BF16),
        ],
        compiler_params=_cparams("parallel", "parallel"),
        name="mla_qkv",
    )(x, sh, sc, g, w["wd"], w["gq"], w["gkv"], w["wqa"], w["wqb"], w["wk"], w["wvt"], cos_t, sin_t)


def _mla_weights(w_dq, g_q, w_uq, w_dkv, g_kv, w_ukv):
    d = w_dq.shape[0]
    hq = QK_NOPE + QK_ROPE
    half = QK_ROPE // 2
    w_rope = w_dkv[:, KV_RANK:]
    w_rope_sw = jnp.concatenate([w_rope[:, half:], w_rope[:, :half]], axis=1)
    zl = jnp.zeros((d, QK_NOPE), F32)
    zr = jnp.zeros((d, HEAD_PAD - hq), F32)
    wd = jnp.concatenate([w_dq, w_dkv[:, :KV_RANK], zl, w_rope, zr, zl, w_rope_sw, zr], axis=1)
    wq = w_uq.reshape(Q_RANK, MLA_HEADS, hq)
    zq = jnp.zeros((Q_RANK, MLA_HEADS, HEAD_PAD - hq), F32)
    wqa = jnp.concatenate([wq, zq], axis=2).reshape(Q_RANK, MLA_HEADS * HEAD_PAD)
    zn = jnp.zeros((Q_RANK, MLA_HEADS, QK_NOPE), F32)
    wqb = jnp.concatenate([zn, wq[:, :, QK_NOPE + half:], wq[:, :, QK_NOPE:QK_NOPE + half], zq],
                          axis=2).reshape(Q_RANK, MLA_HEADS * HEAD_PAD)
    wkv = w_ukv.reshape(KV_RANK, MLA_HEADS, QK_NOPE + V_DIM)
    zk = jnp.zeros((KV_RANK, MLA_HEADS, HEAD_PAD - QK_NOPE), F32)
    wk = jnp.concatenate([wkv[:, :, :QK_NOPE], zk], axis=2).reshape(KV_RANK, MLA_HEADS * HEAD_PAD)
    wv = jnp.transpose(wkv[:, :, QK_NOPE:], (1, 2, 0))
    wvt = jnp.concatenate([wv, jnp.zeros((MLA_HEADS, VT_ROWS - V_DIM, KV_RANK), F32)], axis=1)
    return {
        "wd": wd.astype(BF16), "gq": g_q.reshape(1, -1), "gkv": g_kv.reshape(1, -1),
        "wqa": wqa.astype(BF16), "wqb": wqb.astype(BF16), "wk": wk.astype(BF16),
        "wvt": wvt.reshape(MLA_HEADS * VT_ROWS, KV_RANK).astype(BF16),
    }


def _rope_tables(l):
    t = jnp.arange(l)
    row = (t // GRID_W).astype(F32)
    col = (t % GRID_W).astype(F32)
    n_freq = QK_ROPE // 4
    inv = ROPE_BASE ** (-jnp.arange(n_freq, dtype=F32) / n_freq)
    ang = jnp.concatenate([row[:, None] * inv, col[:, None] * inv], axis=-1)
    cos, sin = jnp.cos(ang), jnp.sin(ang)
    ones = jnp.ones((l, QK_NOPE), F32)
    zl = jnp.zeros((l, QK_NOPE), F32)
    zr = jnp.zeros((l, HEAD_PAD - QK_NOPE - QK_ROPE), F32)
    cos_t = jnp.concatenate([ones, cos, cos, zr], axis=1)
    sin_t = jnp.concatenate([zl, -sin, sin, zr], axis=1)
    return cos_t, sin_t


def _no_rope_tables(l):
    keep = jnp.concatenate([jnp.ones((l, QK_NOPE + QK_ROPE), F32),
                            jnp.zeros((l, HEAD_PAD - QK_NOPE - QK_ROPE), F32)], axis=1)
    return keep, jnp.zeros((l, HEAD_PAD), F32)


def _attn_kernel(*refs, n_lat_chunks, tk):
    if n_lat_chunks:
        q_ref, kc_ref, vtc_ref, k_ref, vt_ref, o_ref = refs
    else:
        q_ref, kc_ref, vtc_ref, o_ref = refs
    outs = []
    for hd in range(2):
        sl = slice(hd * HEAD_PAD, (hd + 1) * HEAD_PAD)
        q = q_ref[0, :, sl]

        def scores(k):
            return lax.dot_general(k, q, (((1,), (1,)), ((), ())), preferred_element_type=F32)

        s = scores(kc_ref[0, :, sl])
        m = jnp.max(s, axis=0, keepdims=True)
        p = jnp.exp2(s - m).astype(BF16)
        acc = _dot(vtc_ref[0, 0, hd], p)

        if n_lat_chunks:
            def body(c, carry):
                m_old, acc_old = carry
                start = pl.multiple_of(c * tk, tk)
                s_c = scores(k_ref[0, pl.ds(start, tk), sl])
                m_new = jnp.maximum(m_old, jnp.max(s_c, axis=0, keepdims=True))
                alpha = jnp.exp2(m_old - m_new)
                p_c = jnp.exp2(s_c - m_new).astype(BF16)
                return m_new, acc_old * alpha + _dot(vt_ref[0, c, hd], p_c)

            m, acc = lax.fori_loop(0, n_lat_chunks, body, (m, acc))
        outs.append(acc[:V_DIM] / acc[ONES_ROW:ONES_ROW + 1])
    o_ref[0] = jnp.concatenate(outs, axis=0).T.astype(BF16)


def _attention(q, kc, vtc, k=None, vt=None, *, tq):
    b, lq, hw = q.shape
    cl = kc.shape[1]
    nq = lq // tq
    hp = 2 * HEAD_PAD
    in_specs = [
        pl.BlockSpec((1, tq, hp), lambda bi, h, i: (bi, i, h)),
        pl.BlockSpec((1, cl, hp), lambda bi, h, i: (bi, 0, h)),
        pl.BlockSpec((1, 1, 2, VT_ROWS, cl), lambda bi, h, i: (bi, 0, h, 0, 0)),
    ]
    args = [q, kc, vtc]
    n_chunks, tk = 0, 0
    if k is not None:
        lk = k.shape[1]
        n_chunks, tk = vt.shape[1], vt.shape[4]
        in_specs += [
            pl.BlockSpec((1, lk, hp), lambda bi, h, i: (bi, 0, h)),
            pl.BlockSpec((1, n_chunks, 2, VT_ROWS, tk), lambda bi, h, i: (bi, 0, h, 0, 0)),
        ]
        args += [k, vt]
    return pl.pallas_call(
        functools.partial(_attn_kernel, n_lat_chunks=n_chunks, tk=tk),
        grid=(b, MLA_HEADS // 2, nq),
        in_specs=in_specs,
        out_specs=pl.BlockSpec((1, tq, 2 * V_DIM), lambda bi, h, i: (bi, i, h)),
        out_shape=jax.ShapeDtypeStruct((b, lq, MLA_HEADS * V_DIM), BF16),
        compiler_params=_cparams("parallel", "parallel", "arbitrary"),
        name="attention" if n_chunks else "attention_ctx",
    )(*args)


def _proj_post_kernel(a_ref, w_ref, b_ref, x_ref, gt_ref, g_ref, o_ref):
    y = _dot(a_ref[0], w_ref[...]) + b_ref[...]
    o_ref[0] = x_ref[0] + gt_ref[0] * _rms(y, g_ref[...])


def _proj_post(a, w, bias, x, gt, g, tm):
    b, l, d = x.shape
    din = a.shape[2]
    return pl.pallas_call(
        _proj_post_kernel,
        grid=(b, l // tm),
        in_specs=[
            pl.BlockSpec((1, tm, din), lambda bi, i: (bi, i, 0)),
            pl.BlockSpec((din, d), lambda bi, i: (0, 0)),
            pl.BlockSpec((1, d), lambda bi, i: (0, 0)),
            pl.BlockSpec((1, tm, d), lambda bi, i: (bi, i, 0)),
            pl.BlockSpec((1, 1, d), lambda bi, i: (bi, 0, 0)),
            pl.BlockSpec((1, d), lambda bi, i: (0, 0)),
        ],
        out_specs=pl.BlockSpec((1, tm, d), lambda bi, i: (bi, i, 0)),
        out_shape=jax.ShapeDtypeStruct((b, l, d), F32),
        compiler_params=_cparams("parallel", "parallel"),
        name="proj_post",
    )(a, w, bias, x, gt, g)


def _mlp_kernel(x_ref, sh_ref, sc_ref, gt_ref, g1_ref, g2_ref, w1_ref, w2_ref, o_ref, *, ff_chunk):
    x = x_ref[0]
    h = (_rms(x, g1_ref[...]) * (1.0 + sc_ref[0]) + sh_ref[0]).astype(BF16)
    dff = w1_ref.shape[1]
    m = None
    for c in range(dff // ff_chunk):
        u = _dot(h, w1_ref[:, c * ff_chunk:(c + 1) * ff_chunk])
        u = jnp.maximum(u, 0.0)
        part = _dot((u * u).astype(BF16), w2_ref[c * ff_chunk:(c + 1) * ff_chunk, :])
        m = part if m is None else m + part
    o_ref[0] = x + gt_ref[0] * _rms(m, g2_ref[...])


def _mlp(x, sh, sc, gt, g1, g2, w1, w2, tm):
    b, l, d = x.shape
    dff = w1.shape[1]
    vec = pl.BlockSpec((1, 1, d), lambda bi, i: (bi, 0, 0))
    row = pl.BlockSpec((1, d), lambda bi, i: (0, 0))
    return pl.pallas_call(
        functools.partial(_mlp_kernel, ff_chunk=1024),
        grid=(b, l // tm),
        in_specs=[
            pl.BlockSpec((1, tm, d), lambda bi, i: (bi, i, 0)),
            vec, vec, vec, row, row,
            pl.BlockSpec((d, dff), lambda bi, i: (0, 0)),
            pl.BlockSpec((dff, d), lambda bi, i: (0, 0)),
        ],
        out_specs=pl.BlockSpec((1, tm, d), lambda bi, i: (bi, i, 0)),
        out_shape=jax.ShapeDtypeStruct((b, l, d), F32),
        compiler_params=_cparams("parallel", "parallel"),
        name="mlp",
    )(x, sh, sc, gt, g1, g2, w1, w2)


HALO = 8


def _hy_in_kernel(x_ref, xp_ref, xn_ref, sh_ref, sc_ref, g_ref, w_ref, b_ref, ws_ref, bs_ref,
                  x1_ref, x2_ref, v_ref, u_scr):
    i = pl.program_id(1)
    n = pl.num_programs(1)
    tm = x_ref.shape[1]
    d = x_ref.shape[2]
    xa = jnp.concatenate([xp_ref[0], x_ref[0], xn_ref[0]], axis=0)
    h = (_rms(xa, g_ref[...]) * (1.0 + sc_ref[0]) + sh_ref[0]).astype(BF16)
    u_scr[...] = _dot(h, w_ref[...]) + b_ref[...]

    @pl.when(i == 0)
    def _():
        u_scr[0:HALO, :] = jnp.zeros((HALO, 3 * d), F32)

    @pl.when(i == n - 1)
    def _():
        u_scr[tm + HALO:tm + 2 * HALO, :] = jnp.zeros((HALO, 3 * d), F32)

    for j, o_ref in enumerate((x1_ref, x2_ref, v_ref)):
        sl = slice(j * d, (j + 1) * d)
        y = (u_scr[HALO - 1:HALO - 1 + tm, sl] * ws_ref[0:1, sl]
             + u_scr[HALO:HALO + tm, sl] * ws_ref[1:2, sl]
             + u_scr[HALO + 1:HALO + 1 + tm, sl] * ws_ref[2:3, sl]
             + bs_ref[:, sl])
        o_ref[0] = y.astype(BF16)


def _hy_in(x, sh, sc, g, w_in, b_in, w_short, b_short, tm):
    b, l, d = x.shape
    nt = l // tm
    tb = tm // HALO
    nb = l // HALO
    vec = pl.BlockSpec((1, 1, d), lambda bi, i: (bi, 0, 0))
    out = pl.BlockSpec((1, tm, d), lambda bi, i: (bi, i, 0))
    return pl.pallas_call(
        _hy_in_kernel,
        grid=(b, nt),
        in_specs=[
            pl.BlockSpec((1, tm, d), lambda bi, i: (bi, i, 0)),
            pl.BlockSpec((1, HALO, d), lambda bi, i: (bi, jnp.maximum(i * tb - 1, 0), 0)),
            pl.BlockSpec((1, HALO, d), lambda bi, i: (bi, jnp.minimum((i + 1) * tb, nb - 1), 0)),
            vec, vec,
            pl.BlockSpec((1, d), lambda bi, i: (0, 0)),
            pl.BlockSpec((d, 3 * d), lambda bi, i: (0, 0)),
            pl.BlockSpec((1, 3 * d), lambda bi, i: (0, 0)),
            pl.BlockSpec((3, 3 * d), lambda bi, i: (0, 0)),
            pl.BlockSpec((1, 3 * d), lambda bi, i: (0, 0)),
        ],
        out_specs=[out, out, out],
        out_shape=[jax.ShapeDtypeStruct((b, l, d), BF16)] * 3,
        scratch_shapes=[pltpu.VMEM((tm + 2 * HALO, 3 * d), F32)],
        compiler_params=_cparams("parallel", "parallel"),
        name="hyena_in",
    )(x, x, x, sh, sc, g, w_in, b_in, w_short, b_short)


def _filt_kernel(z_ref, w1_ref, b1_ref, w2_ref, b2_ref, w3_ref, fr_ref, dl_ref, o_ref, *, l):
    tl = z_ref.shape[0]
    d = dl_ref.shape[1]
    fr = fr_ref[...]
    hdn = jnp.sin(fr * (_dot3(z_ref[...], w1_ref[...]) + b1_ref[...]))
    hdn = jnp.sin(fr * (_dot3(hdn, w2_ref[...]) + b2_ref[...]))
    hf = _dot3(hdn, w3_ref[...])
    rows = pl.program_id(0) * tl + lax.broadcasted_iota(jnp.int32, (tl, d), 0)
    t = rows.astype(F32) * (1.0 / (l - 1))
    decay = jnp.exp(-t * jnp.abs(dl_ref[...]))
    for j in range(2 * HY_ORDER):
        f = hf[:, j * d:(j + 1) * d] * decay
        if j % 2 == 1:
            f = jnp.where(rows == 0, 0.0, f)
        o_ref[j] = f.astype(BF16)


def _filters(l, d, f_w1, f_b1, f_w2, f_b2, f_w3, f_freq):
    t = jnp.linspace(0.0, 1.0, l, dtype=F32)[:, None]
    w = (2.0 * math.pi) * jnp.arange(l, dtype=F32)[:, None] / l
    f = jnp.linspace(1e-4, HY_BANDS - 1, HY_BANDS, dtype=F32)[None, :]
    kz = HY_FILTER_HIDDEN
    z = jnp.concatenate([t, jnp.cos(f * w), -jnp.sin(f * w), jnp.zeros((l, kz - HY_EMB), F32)], axis=-1)
    w1 = jnp.concatenate([f_w1, jnp.zeros((kz - HY_EMB, HY_FILTER_HIDDEN), F32)], axis=0)
    min_decay = math.log(HY_TARGET) / HY_SLOW_PCT
    max_decay = math.log(HY_TARGET) / HY_FAST_PCT
    deltas = jnp.linspace(min_decay, max_decay, d, dtype=F32)[None, :]
    tl = 256
    full = lambda a: pl.BlockSpec(a.shape, lambda i: (0,) * a.ndim)
    ops = [w1, f_b1.reshape(1, -1), f_w2, f_b2.reshape(1, -1), f_w3, f_freq.reshape(1, -1), deltas]
    return pl.pallas_call(
        functools.partial(_filt_kernel, l=l),
        grid=(l // tl,),
        in_specs=[pl.BlockSpec((tl, kz), lambda i: (i, 0))] + [full(a) for a in ops],
        out_specs=pl.BlockSpec((2 * HY_ORDER, tl, d), lambda i: (0, i, 0)),
        out_shape=jax.ShapeDtypeStruct((2 * HY_ORDER, l, d), BF16),
        compiler_params=_cparams("parallel"),
        name="hyena_filters",
    )(z, *ops)


def _dft_tables(l):
    n = 2 * l
    n2 = DFT_N2
    n1 = n // n2
    n1h = n1 // 2
    slots = n1h + 8
    k1 = jnp.arange(slots)
    valid = (k1 <= n1h)[:, None]
    a = jnp.arange(n1h)
    ang1 = (2.0 * math.pi / n1) * ((k1[:, None] * a[None, :]) % n1).astype(F32)
    f1 = jnp.concatenate([jnp.where(valid, jnp.cos(ang1), 0.0), jnp.where(valid, -jnp.sin(ang1), 0.0)], axis=0)
    kk = jnp.arange(n1h)
    wgt = jnp.where(kk == 0, 1.0, 2.0)[None, :] / n
    ang2 = (2.0 * math.pi / n1) * ((a[:, None] * kk[None, :]) % n1).astype(F32)
    f2 = jnp.concatenate([wgt * jnp.cos(ang2), -wgt * jnp.sin(ang2)], axis=1)
    kg = jnp.arange(n1h + 1)
    k2 = jnp.arange(n2)
    j = jnp.arange(n2)
    prod = (j[None, None, :] * (kg[:, None, None] + n1 * k2[None, :, None])) % n
    th = (2.0 * math.pi / n) * prod.astype(F32)
    c, s = jnp.cos(th), jnp.sin(th)
    g = jnp.concatenate([jnp.concatenate([c, s], axis=2), jnp.concatenate([-s, c], axis=2)], axis=1)
    ct, st = jnp.swapaxes(c, 1, 2), jnp.swapaxes(s, 1, 2)
    gi = jnp.concatenate([jnp.concatenate([ct, -st], axis=2), jnp.concatenate([st, ct], axis=2)], axis=1)
    return {"n": n, "n1h": n1h, "slots": slots,
            "f1": f1.astype(BF16), "f2": f2.astype(BF16), "g": g.astype(BF16), "gi": gi.astype(BF16)}


def _stage1_kernel(f_ref, u_ref, a_ref):
    a_ref[0] = _dot(f_ref[...], u_ref[0]).astype(BF16)


def _stage1(u, tab):
    b, l, d = u.shape
    n1h, slots = tab["n1h"], tab["slots"]
    cols = DFT_N2 * d
    tc = 4096
    a = pl.pallas_call(
        _stage1_kernel,
        grid=(b, cols // tc),
        in_specs=[
            pl.BlockSpec((2 * slots, n1h), lambda bi, j: (0, 0)),
            pl.BlockSpec((1, n1h, tc), lambda bi, j: (bi, 0, j)),
        ],
        out_specs=pl.BlockSpec((1, 2 * slots, tc), lambda bi, j: (bi, 0, j)),
        out_shape=jax.ShapeDtypeStruct((b, 2 * slots, cols), BF16),
        compiler_params=_cparams("parallel", "parallel"),
        name="dft_stage1",
    )(tab["f1"], u.reshape(b, n1h, cols))
    return a.reshape(b, 2, slots, DFT_N2, d)


def _cplx_rows(ref):
    n2, d = ref.shape[-2], ref.shape[-1]
    return ref[...].reshape(2 * n2, d)


def _filt_spec_kernel(g_ref, af_ref, ab_ref, bias_ref, k_ref):
    n2 = af_ref.shape[-2]
    xf = _dot(g_ref[0], _cplx_rows(af_ref))
    xb = _dot(g_ref[0], _cplx_rows(ab_ref))
    k_ref[0, 0, 0] = (xf[:n2] + xb[:n2] + bias_ref[0]).astype(BF16)
    k_ref[0, 1, 0] = (xf[n2:] - xb[n2:]).astype(BF16)


def _filt_spectrum(a_filt, bias, tab):
    _, _, slots, n2, d = a_filt.shape
    n1h = tab["n1h"]
    blk = lambda seq: pl.BlockSpec((1, 2, 1, n2, d), lambda o, k: (2 * o + seq, 0, k, 0, 0))
    return pl.pallas_call(
        _filt_spec_kernel,
        grid=(HY_ORDER, slots),
        in_specs=[
            pl.BlockSpec((1, 2 * n2, 2 * n2), lambda o, k: (jnp.minimum(k, n1h), 0, 0)),
            blk(0), blk(1),
            pl.BlockSpec((1, 1, d), lambda o, k: (o, 0, 0)),
        ],
        out_specs=pl.BlockSpec((1, 2, 1, n2, d), lambda o, k: (o, 0, k, 0, 0)),
        out_shape=jax.ShapeDtypeStruct((HY_ORDER, 2, slots, n2, d), BF16),
        compiler_params=_cparams("parallel", "parallel"),
        name="hyena_filter_spectrum",
    )(tab["g"], a_filt, a_filt, bias.reshape(HY_ORDER, 1, d))


def _mid_kernel(g_ref, gi_ref, a_ref, k_ref, z_ref):
    n2 = a_ref.shape[-2]
    x = _dot(g_ref[0], _cplx_rows(a_ref))
    xr, xi = x[:n2], x[n2:]
    kr = k_ref[0, 0, 0].astype(F32)
    ki = k_ref[0, 1, 0].astype(F32)
    y = jnp.concatenate([xr * kr - xi * ki, xr * ki + xi * kr], axis=0).astype(BF16)
    z = _dot(gi_ref[0], y)
    z_ref[0, 0, 0] = z[:n2].astype(BF16)
    z_ref[0, 1, 0] = z[n2:].astype(BF16)


def _mid(a, kf, order, tab):
    b, _, slots, n2, d = a.shape
    n1h = tab["n1h"]
    gspec = pl.BlockSpec((1, 2 * n2, 2 * n2), lambda bi, k: (jnp.minimum(k, n1h), 0, 0))
    return pl.pallas_call(
        _mid_kernel,
        grid=(b, slots),
        in_specs=[
            gspec, gspec,
            pl.BlockSpec((1, 2, 1, n2, d), lambda bi, k: (bi, 0, k, 0, 0)),
            pl.BlockSpec((1, 2, 1, n2, d), lambda bi, k: (order, 0, k, 0, 0)),
        ],
        out_specs=pl.BlockSpec((1, 2, 1, n2, d), lambda bi, k: (bi, 0, k, 0, 0)),
        out_shape=jax.ShapeDtypeStruct((b, 2, slots, n2, d), BF16),
        compiler_params=_cparams("parallel", "parallel"),
        name="dft_mid",
    )(tab["g"], tab["gi"], a, kf)


def _stage2_kernel(f_ref, zr_ref, zi_ref, zt_ref, x_ref, o_ref, *, inv_n):
    z = jnp.concatenate([zr_ref[0, 0], zi_ref[0, 0]], axis=0)
    y = _dot(f_ref[...], z)
    rows = lax.broadcasted_iota(jnp.int32, y.shape, 0)
    nyq = zt_ref[0, 0, 0:1, :].astype(F32) * inv_n
    y = y + jnp.where(rows % 2 == 0, nyq, -nyq)
    o_ref[0] = (y * x_ref[0].astype(F32)).astype(BF16)


def _stage2(z, gate, tab):
    b, _, slots, n2, d = z.shape
    n1h = tab["n1h"]
    cols = n2 * d
    tc = 4096
    zf = z.reshape(b, 2, slots, cols)
    l = n1h * n2
    out = pl.pallas_call(
        functools.partial(_stage2_kernel, inv_n=1.0 / tab["n"]),
        grid=(b, cols // tc),
        in_specs=[
            pl.BlockSpec((n1h, 2 * n1h), lambda bi, j: (0, 0)),
            pl.BlockSpec((1, 1, n1h, tc), lambda bi, j: (bi, 0, 0, j)),
            pl.BlockSpec((1, 1, n1h, tc), lambda bi, j: (bi, 1, 0, j)),
            pl.BlockSpec((1, 1, 8, tc), lambda bi, j: (bi, 0, n1h // 8, j)),
            pl.BlockSpec((1, n1h, tc), lambda bi, j: (bi, 0, j)),
        ],
        out_specs=pl.BlockSpec((1, n1h, tc), lambda bi, j: (bi, 0, j)),
        out_shape=jax.ShapeDtypeStruct((b, n1h, cols), BF16),
        compiler_params=_cparams("parallel", "parallel"),
        name="dft_stage2",
    )(tab["f2"], zf, zf, zf, gate.reshape(b, n1h, cols))
    return out.reshape(b, l, d)


def _small_dft_tables(l):
    n = 2 * l
    k = jnp.arange(l)
    t = jnp.arange(l)
    ang = (2.0 * math.pi / n) * ((k[:, None] * t[None, :]) % n).astype(F32)
    sign = jnp.where(t % 2 == 0, 1.0, -1.0)[None, :]
    fwd = jnp.concatenate([jnp.cos(ang), -jnp.sin(ang), sign, jnp.zeros((7, l), F32)], axis=0)
    wgt = jnp.where(k == 0, 1.0, 2.0)[None, :] / n
    ang_t = ang.T
    inv = jnp.concatenate([wgt * jnp.cos(ang_t), -wgt * jnp.sin(ang_t)], axis=1)
    return {"n": n, "fwd": fwd.astype(BF16), "inv": inv.astype(BF16)}


def _small_spec_kernel(f_ref, hf_ref, hb_ref, bias_ref, k_ref):
    l = hf_ref.shape[1]
    xf = _dot(f_ref[...], hf_ref[0])
    xb = _dot(f_ref[...], hb_ref[0])
    bias = bias_ref[0]
    k_ref[0, 0:l, :] = (xf[:l] + xb[:l] + bias).astype(BF16)
    k_ref[0, l:2 * l, :] = (xf[l:2 * l] - xb[l:2 * l]).astype(BF16)
    k_ref[0, 2 * l:, :] = (xf[2 * l:] + xb[2 * l:] + bias).astype(BF16)


def _small_filt_spectrum(filt, bias, tab):
    _, l, d = filt.shape
    rows = 2 * l + 8
    tc = 512
    return pl.pallas_call(
        _small_spec_kernel,
        grid=(HY_ORDER, d // tc),
        in_specs=[
            pl.BlockSpec((rows, l), lambda o, j: (0, 0)),
            pl.BlockSpec((1, l, tc), lambda o, j: (2 * o, 0, j)),
            pl.BlockSpec((1, l, tc), lambda o, j: (2 * o + 1, 0, j)),
            pl.BlockSpec((1, 1, tc), lambda o, j: (o, 0, j)),
        ],
        out_specs=pl.BlockSpec((1, rows, tc), lambda o, j: (o, 0, j)),
        out_shape=jax.ShapeDtypeStruct((HY_ORDER, rows, d), BF16),
        compiler_params=_cparams("parallel", "parallel"),
        name="hyena_filter_spectrum_ctx",
    )(tab["fwd"], filt, filt, bias.reshape(HY_ORDER, 1, d))


def _small_conv_kernel(f_ref, fi_ref, u_ref, k_ref, x_ref, o_ref, *, inv_n):
    l = u_ref.shape[1]
    x = _dot(f_ref[...], u_ref[0])
    xr, xi = x[:l], x[l:2 * l]
    kr = k_ref[0, 0:l, :].astype(F32)
    ki = k_ref[0, l:2 * l, :].astype(F32)
    y = jnp.concatenate([xr * kr - xi * ki, xr * ki + xi * kr], axis=0).astype(BF16)
    out = _dot(fi_ref[...], y)
    nyq = x[2 * l:2 * l + 1] * k_ref[0, 2 * l:2 * l + 1, :].astype(F32) * inv_n
    rows = lax.broadcasted_iota(jnp.int32, out.shape, 0)
    out = out + jnp.where(rows % 2 == 0, nyq, -nyq)
    o_ref[0] = (out * x_ref[0].astype(F32)).astype(BF16)


def _small_conv(u, kf, order, gate, tab):
    b, l, d = u.shape
    rows = 2 * l + 8
    tc = 512
    return pl.pallas_call(
        functools.partial(_small_conv_kernel, inv_n=1.0 / tab["n"]),
        grid=(b, d // tc),
        in_specs=[
            pl.BlockSpec((rows, l), lambda bi, j: (0, 0)),
            pl.BlockSpec((l, 2 * l), lambda bi, j: (0, 0)),
            pl.BlockSpec((1, l, tc), lambda bi, j: (bi, 0, j)),
            pl.BlockSpec((1, rows, tc), lambda bi, j: (order, 0, j)),
            pl.BlockSpec((1, l, tc), lambda bi, j: (bi, 0, j)),
        ],
        out_specs=pl.BlockSpec((1, l, tc), lambda bi, j: (bi, 0, j)),
        out_shape=jax.ShapeDtypeStruct((b, l, d), BF16),
        compiler_params=_cparams("parallel", "parallel"),
        name="long_conv_ctx",
    )(tab["fwd"], tab["inv"], u, kf, gate)


def _row_tile(l):
    return 512 if l % 512 == 0 else 256


def _hyena_mixer_lat(x, sh, sc, g, p, tabs):
    b, l, d = x.shape
    x1, x2, v = _hy_in(x, sh, sc, g, p["w_in"], p["b_in"], p["w_short"], p["b_short"], _row_tile(l))
    filt = _filters(l, d, *p["filter"])
    kf = _filt_spectrum(_stage1(filt, tabs), p["bias"], tabs)
    z = _stage2(_mid(_stage1(v, tabs), kf, 0, tabs), x1, tabs)
    z = _stage2(_mid(_stage1(z, tabs), kf, 1, tabs), x2, tabs)
    return z


def _hyena_mixer_ctx(x, sh, sc, g, p, tabs):
    b, l, d = x.shape
    x1, x2, v = _hy_in(x, sh, sc, g, p["w_in"], p["b_in"], p["w_short"], p["b_short"], l)
    filt = _filters(l, d, *p["filter"])
    kf = _small_filt_spectrum(filt, p["bias"], tabs)
    z = _small_conv(v, kf, 0, x1, tabs)
    z = _small_conv(z, kf, 1, x2, tabs)
    return z


def kernel(x, c, ctx, c_ctx, ada_w, ada_b, norm_g, mla_w_dq, mla_g_q, mla_w_uq, mla_w_dkv, mla_g_kv,
           mla_w_ukv, mla_w_o, hy_w_in, hy_b_in, hy_w_short, hy_b_short, hy_f_w1, hy_f_b1, hy_f_w2,
           hy_f_b2, hy_f_w3, hy_f_freq, hy_bias, hy_w_out, hy_b_out, mlp_w1, mlp_w2):
    b, l, d = x.shape
    cl = ctx.shape[1]
    depth = ada_w.shape[0]
    assert b + 1 <= 8 and l % 256 == 0 and cl % 256 == 0

    cvec = jnp.concatenate([c, c_ctx[None, :], jnp.zeros((8 - b - 1, d), F32)], axis=0)
    mods = _ada(cvec, ada_w, ada_b)

    rope_lat = _rope_tables(l)
    rope_ctx = _no_rope_tables(cl)
    tabs_lat = _dft_tables(l)
    tabs_ctx = _small_dft_tables(cl)
    zero_bias = jnp.zeros((1, d), F32)
    tm = _row_tile(l)

    xc = ctx
    for i in range(depth):
        last = i == depth - 1
        j = i // 2
        g = norm_g[i].reshape(4, 1, d)
        m_lat = mods[i, :b].reshape(b, 6, 1, d)
        m_ctx = jnp.broadcast_to(mods[i, b].reshape(1, 6, 1, d), (b, 6, 1, d))
        lat = [m_lat[:, k] for k in range(6)]
        cx = [m_ctx[:, k] for k in range(6)]

        if i % 2 == 0:
            w = _mla_weights(mla_w_dq[j], mla_g_q[j], mla_w_uq[j], mla_w_dkv[j], mla_g_kv[j], mla_w_ukv[j])
            wo = mla_w_o[j].astype(BF16)
            qc, kc, vtc = _qkv(xc, cx[0], cx[1], g[0], w, *rope_ctx, cl)
            ql, kl, vtl = _qkv(x, lat[0], lat[1], g[0], w, *rope_lat, tm)
            o_lat = _attention(ql, kc, vtc, kl, vtl, tq=256)
            x = _proj_post(o_lat, wo, zero_bias, x, lat[2], g[1], tm)
            if not last:
                o_ctx = _attention(qc, kc, vtc, tq=cl)
                xc = _proj_post(o_ctx, wo, zero_bias, xc, cx[2], g[1], cl)
        else:
            p = {
                "w_in": hy_w_in[j].astype(BF16), "b_in": hy_b_in[j].reshape(1, -1),
                "w_short": hy_w_short[j], "b_short": hy_b_short[j].reshape(1, -1),
                "filter": (hy_f_w1[j], hy_f_b1[j], hy_f_w2[j], hy_f_b2[j], hy_f_w3[j], hy_f_freq[j]),
                "bias": hy_bias[j],
            }
            wo = hy_w_out[j].astype(BF16)
            bo = hy_b_out[j].reshape(1, d)
            z_lat = _hyena_mixer_lat(x, lat[0], lat[1], g[0], p, tabs_lat)
            x = _proj_post(z_lat, wo, bo, x, lat[2], g[1], tm)
            if not last:
                z_ctx = _hyena_mixer_ctx(xc, cx[0], cx[1], g[0], p, tabs_ctx)
                xc = _proj_post(z_ctx, wo, bo, xc, cx[2], g[1], cl)

        w1 = mlp_w1[i].astype(BF16)
        w2 = mlp_w2[i].astype(BF16)
        x = _mlp(x, lat[3], lat[4], lat[5], g[2], g[3], w1, w2, tm)
        if not last:
            xc = _mlp(xc, cx[3], cx[4], cx[5], g[2], g[3], w1, w2, cl)
    return x
```

```python
import functools
import math

import numpy as np
import jax
import jax.numpy as jnp
from jax import lax
from jax.experimental import pallas as pl
from jax.experimental.pallas import tpu as pltpu

F32 = jnp.float32
BF16 = jnp.bfloat16

GRID_W = 64
MLA_HEADS = 16
QK_NOPE = 64
QK_ROPE = 32
V_DIM = 64
Q_RANK = 256
KV_RANK = 128
ROPE_BASE = 10000.0
HY_ORDER = 2
HY_EMB = 33
HY_BANDS = (HY_EMB - 1) // 2
HY_FILTER_HIDDEN = 64
HY_TARGET = 1e-2
HY_FAST_PCT = 0.3
HY_SLOW_PCT = 1.5
EPS = 1e-6

LANES = 128
HEAD_PAD = LANES
VT_ROWS = 80
ONES_ROW = V_DIM
VMEM_LIMIT = 56 * 1024 * 1024
DFT_N2 = 128

LOG2E = 1.4426950408889634


def _cparams(*sem):
    return pltpu.CompilerParams(dimension_semantics=sem, vmem_limit_bytes=VMEM_LIMIT)


def _dot(a, b):
    return jnp.dot(a, b, preferred_element_type=F32)


def _dot3(a, b):
    ah = a.astype(BF16)
    al = (a - ah.astype(F32)).astype(BF16)
    bh = b.astype(BF16)
    bl = (b - bh.astype(F32)).astype(BF16)
    return _dot(ah, bh) + _dot(al, bh) + _dot(ah, bl)


def _rms(x, g):
    return x * lax.rsqrt(jnp.mean(x * x, axis=-1, keepdims=True) + EPS) * g


def _ada_kernel(c_ref, w_ref, b_ref, o_ref):
    c = c_ref[...]
    s = c / (1.0 + jnp.exp(-c))
    o_ref[0] = _dot3(s, w_ref[0]) + b_ref[0]


def _ada(cvec, ada_w, ada_b):
    depth, d, n6 = ada_w.shape
    tn = 1536
    return pl.pallas_call(
        _ada_kernel,
        grid=(depth, n6 // tn),
        in_specs=[
            pl.BlockSpec((8, d), lambda i, j: (0, 0)),
            pl.BlockSpec((1, d, tn), lambda i, j: (i, 0, j)),
            pl.BlockSpec((1, 1, tn), lambda i, j: (i, 0, j)),
        ],
        out_specs=pl.BlockSpec((1, 8, tn), lambda i, j: (i, 0, j)),
        out_shape=jax.ShapeDtypeStruct((depth, 8, n6), F32),
        compiler_params=_cparams("parallel", "parallel"),
        name="ada",
    )(cvec, ada_w, ada_b.reshape(depth, 1, n6))


def _qkv_kernel(x_ref, sh_ref, sc_ref, g_ref, wd_ref, gq_ref, gkv_ref, wqa_ref, wqb_ref, wk_ref,
                wvt_ref, cos_ref, sin_ref, q_ref, k_ref, vt_ref, *, qscale):
    h = _rms(x_ref[0], g_ref[...]) * (1.0 + sc_ref[0]) + sh_ref[0]
    t = _dot(h.astype(BF16), wd_ref[...])
    cq = _rms(t[:, :Q_RANK], gq_ref[...]).astype(BF16)
    ckv = _rms(t[:, Q_RANK:Q_RANK + KV_RANK], gkv_ref[...]).astype(BF16)
    cos = cos_ref[...]
    sin = sin_ref[...]
    o = Q_RANK + KV_RANK
    kr = t[:, o:o + LANES] * cos + t[:, o + LANES:o + 2 * LANES] * sin
    qa = _dot(cq, wqa_ref[...])
    qb = _dot(cq, wqb_ref[...])
    kn = _dot(ckv, wk_ref[...])
    for hd in range(MLA_HEADS):
        sl = slice(hd * HEAD_PAD, (hd + 1) * HEAD_PAD)
        q_ref[0, :, sl] = ((qa[:, sl] * cos + qb[:, sl] * sin) * qscale).astype(BF16)
        k_ref[0, :, sl] = (kn[:, sl] + kr).astype(BF16)
    vt = lax.dot_general(wvt_ref[...], ckv, (((1,), (1,)), ((), ())), preferred_element_type=F32)
    tm = vt.shape[1]
    vt = vt.reshape(MLA_HEADS, VT_ROWS, tm)
    ones = lax.broadcasted_iota(jnp.int32, vt.shape, 1) == ONES_ROW
    vt_ref[0, 0] = jnp.where(ones, 1.0, vt).astype(BF16)


def _qkv(x, sh, sc, g, w, cos_t, sin_t, tm):
    b, l, d = x.shape
    nt = l // tm
    hw = MLA_HEADS * HEAD_PAD
    qscale = LOG2E / math.sqrt(QK_NOPE + QK_ROPE)
    full = lambda a: pl.BlockSpec(a.shape, lambda bi, i: (0,) * a.ndim)
    return pl.pallas_call(
        functools.partial(_qkv_kernel, qscale=qscale),
        grid=(b, nt),
        in_specs=[
            pl.BlockSpec((1, tm, d), lambda bi, i: (bi, i, 0)),
            pl.BlockSpec((1, 1, d), lambda bi, i: (bi, 0, 0)),
            pl.BlockSpec((1, 1, d), lambda bi, i: (bi, 0, 0)),
            full(g), full(w["wd"]), full(w["gq"]), full(w["gkv"]), full(w["wqa"]), full(w["wqb"]),
            full(w["wk"]), full(w["wvt"]),
            pl.BlockSpec((tm, LANES), lambda bi, i: (i, 0)),
            pl.BlockSpec((tm, LANES), lambda bi, i: (i, 0)),
        ],
        out_specs=[
            pl.BlockSpec((1, tm, hw), lambda bi, i: (bi, i, 0)),
            pl.BlockSpec((1, tm, hw), lambda bi, i: (bi, i, 0)),
            pl.BlockSpec((1, 1, MLA_HEADS, VT_ROWS, tm), lambda bi, i: (bi, i, 0, 0, 0)),
        ],
        out_shape=[
            jax.ShapeDtypeStruct((b, l, hw), BF16),
            jax.ShapeDtypeStruct((b, l, hw), BF16),
            jax.ShapeDtypeStruct((b, nt, MLA_HEADS, VT_ROWS, tm), BF16),
        ],
        compiler_params=_cparams("parallel", "parallel"),
        name="mla_qkv",
    )(x, sh, sc, g, w["wd"], w["gq"], w["gkv"], w["wqa"], w["wqb"], w["wk"], w["wvt"], cos_t, sin_t)


def _mla_weights(w_dq, g_q, w_uq, w_dkv, g_kv, w_ukv):
    d = w_dq.shape[0]
    hq = QK_NOPE + QK_ROPE
    half = QK_ROPE // 2
    w_rope = w_dkv[:, KV_RANK:]
    w_rope_sw = jnp.concatenate([w_rope[:, half:], w_rope[:, :half]], axis=1)
    zl = jnp.zeros((d, QK_NOPE), F32)
    zr = jnp.zeros((d, HEAD_PAD - hq), F32)
    wd = jnp.concatenate([w_dq, w_dkv[:, :KV_RANK], zl, w_rope, zr, zl, w_rope_sw, zr], axis=1)
    wq = w_uq.reshape(Q_RANK, MLA_HEADS, hq)
    zq = jnp.zeros((Q_RANK, MLA_HEADS, HEAD_PAD - hq), F32)
    wqa = jnp.concatenate([wq, zq], axis=2).reshape(Q_RANK, MLA_HEADS * HEAD_PAD)
    zn = jnp.zeros((Q_RANK, MLA_HEADS, QK_NOPE), F32)
    wqb = jnp.concatenate([zn, wq[:, :, QK_NOPE + half:], wq[:, :, QK_NOPE:QK_NOPE + half], zq],
                          axis=2).reshape(Q_RANK, MLA_HEADS * HEAD_PAD)
    wkv = w_ukv.reshape(KV_RANK, MLA_HEADS, QK_NOPE + V_DIM)
    zk = jnp.zeros((KV_RANK, MLA_HEADS, HEAD_PAD - QK_NOPE), F32)
    wk = jnp.concatenate([wkv[:, :, :QK_NOPE], zk], axis=2).reshape(KV_RANK, MLA_HEADS * HEAD_PAD)
    wv = jnp.transpose(wkv[:, :, QK_NOPE:], (1, 2, 0))
    wvt = jnp.concatenate([wv, jnp.zeros((MLA_HEADS, VT_ROWS - V_DIM, KV_RANK), F32)], axis=1)
    return {
        "wd": wd.astype(BF16), "gq": g_q.reshape(1, -1), "gkv": g_kv.reshape(1, -1),
        "wqa": wqa.astype(BF16), "wqb": wqb.astype(BF16), "wk": wk.astype(BF16),
        "wvt": wvt.reshape(MLA_HEADS * VT_ROWS, KV_RANK).astype(BF16),
    }


def _rope_tables(l):
    t = jnp.arange(l)
    row = (t // GRID_W).astype(F32)
    col = (t % GRID_W).astype(F32)
    n_freq = QK_ROPE // 4
    inv = ROPE_BASE ** (-jnp.arange(n_freq, dtype=F32) / n_freq)
    ang = jnp.concatenate([row[:, None] * inv, col[:, None] * inv], axis=-1)
    cos, sin = jnp.cos(ang), jnp.sin(ang)
    ones = jnp.ones((l, QK_NOPE), F32)
    zl = jnp.zeros((l, QK_NOPE), F32)
    zr = jnp.zeros((l, HEAD_PAD - QK_NOPE - QK_ROPE), F32)
    cos_t = jnp.concatenate([ones, cos, cos, zr], axis=1)
    sin_t = jnp.concatenate([zl, -sin, sin, zr], axis=1)
    return cos_t, sin_t


def _no_rope_tables(l):
    keep = jnp.concatenate([jnp.ones((l, QK_NOPE + QK_ROPE), F32),
                            jnp.zeros((l, HEAD_PAD - QK_NOPE - QK_ROPE), F32)], axis=1)
    return keep, jnp.zeros((l, HEAD_PAD), F32)


def _attn_kernel(*refs, n_lat_chunks, tk):
    heads = (0, 1)
    sls = [slice(hd * HEAD_PAD, (hd + 1) * HEAD_PAD) for hd in heads]
    if n_lat_chunks:
        q_ref, kc_ref, vtc_ref, k_ref, vt_ref, o_ref = refs[:6]
        scr = refs[6:]
        s_scr = [scr[0:2], scr[2:4]]
        p_scr = [scr[4:6], scr[6:8]]
        acc_scr = scr[8:10]
    else:
        q_ref, kc_ref, vtc_ref, o_ref = refs

    def scores(k, hd):
        return lax.dot_general(k, q_ref[0, :, sls[hd]], (((1,), (1,)), ((), ())), preferred_element_type=F32)

    init = []
    for hd in heads:
        s = scores(kc_ref[0, :, sls[hd]], hd)
        m = jnp.max(s, axis=0, keepdims=True)
        p = jnp.exp2(s - m).astype(BF16)
        init.append((m, _dot(vtc_ref[0, 0, hd], p)))

    if not n_lat_chunks:
        outs = [acc[:V_DIM] / acc[ONES_ROW:ONES_ROW + 1] for _, acc in init]
        o_ref[0] = jnp.concatenate(outs, axis=0).T.astype(BF16)
        return

    def stage_a(hd, slot, c):
        start = pl.multiple_of(c * tk, tk)
        s = scores(k_ref[0, pl.ds(start, tk), sls[hd]], hd)
        s_scr[hd][slot][...] = s
        return jnp.max(s.reshape(tk // 8, 8, s.shape[1]), axis=0)

    def stage_b(hd, slot, cm, m_old):
        m_new = jnp.maximum(m_old, jnp.max(cm, axis=0, keepdims=True))
        p_scr[hd][slot][...] = jnp.exp2(s_scr[hd][slot][...] - m_new).astype(BF16)
        return m_new, jnp.exp2(m_old - m_new)

    def stage_c(hd, slot, c, alpha):
        acc_scr[hd][...] = acc_scr[hd][...] * alpha + _dot(vt_ref[0, c, hd], p_scr[hd][slot][...])

    carry = []
    for hd in heads:
        m, acc = init[hd]
        acc_scr[hd][...] = acc
        cm0 = stage_a(hd, 0, 0)
        cm1 = stage_a(hd, 1, 1)
        m, al = stage_b(hd, 0, cm0, m)
        carry += [m, al, cm1]

    def body(j, carry):
        c = 2 * j
        out = []
        for hd in heads:
            m, al, cm = carry[3 * hd:3 * hd + 3]
            stage_c(hd, 0, c, al)
            m, al = stage_b(hd, 1, cm, m)
            cm = stage_a(hd, 0, c + 2)
            stage_c(hd, 1, c + 1, al)
            m, al = stage_b(hd, 0, cm, m)
            cm = stage_a(hd, 1, c + 3)
            out += [m, al, cm]
        return tuple(out)

    carry = lax.fori_loop(0, n_lat_chunks // 2 - 1, body, tuple(carry))
    outs = []
    for hd in heads:
        m, al, cm = carry[3 * hd:3 * hd + 3]
        stage_c(hd, 0, n_lat_chunks - 2, al)
        m, al = stage_b(hd, 1, cm, m)
        stage_c(hd, 1, n_lat_chunks - 1, al)
        acc = acc_scr[hd][...]
        outs.append(acc[:V_DIM] / acc[ONES_ROW:ONES_ROW + 1])
    o_ref[0] = jnp.concatenate(outs, axis=0).T.astype(BF16)


def _attention(q, kc, vtc, k=None, vt=None, *, tq):
    b, lq, hw = q.shape
    cl = kc.shape[1]
    nq = lq // tq
    hp = 2 * HEAD_PAD
    in_specs = [
        pl.BlockSpec((1, tq, hp), lambda bi, h, i: (bi, i, h)),
        pl.BlockSpec((1, cl, hp), lambda bi, h, i: (bi, 0, h)),
        pl.BlockSpec((1, 1, 2, VT_ROWS, cl), lambda bi, h, i: (bi, 0, h, 0, 0)),
    ]
    args = [q, kc, vtc]
    n_chunks, tk = 0, 0
    scratch = []
    if k is not None:
        lk = k.shape[1]
        n_chunks, tk = vt.shape[1], vt.shape[4]
        assert n_chunks >= 2 and n_chunks % 2 == 0
        in_specs += [
            pl.BlockSpec((1, lk, hp), lambda bi, h, i: (bi, 0, h)),
            pl.BlockSpec((1, n_chunks, 2, VT_ROWS, tk), lambda bi, h, i: (bi, 0, h, 0, 0)),
        ]
        args += [k, vt]
        scratch = ([pltpu.VMEM((tk, tq), F32)] * 4 + [pltpu.VMEM((tk, tq), BF16)] * 4
                   + [pltpu.VMEM((VT_ROWS, tq), F32)] * 2)
    return pl.pallas_call(
        functools.partial(_attn_kernel, n_lat_chunks=n_chunks, tk=tk),
        grid=(b, MLA_HEADS // 2, nq),
        in_specs=in_specs,
        out_specs=pl.BlockSpec((1, tq, 2 * V_DIM), lambda bi, h, i: (bi, i, h)),
        out_shape=jax.ShapeDtypeStruct((b, lq, MLA_HEADS * V_DIM), BF16),
        scratch_shapes=scratch,
        compiler_params=_cparams("parallel", "parallel", "arbitrary"),
        name="attention" if n_chunks else "attention_ctx",
    )(*args)


def _proj_post_kernel(a_ref, w_ref, b_ref, x_ref, gt_ref, g_ref, o_ref):
    y = _dot(a_ref[0], w_ref[...]) + b_ref[...]
    o_ref[0] = x_ref[0] + gt_ref[0] * _rms(y, g_ref[...])


def _proj_post(a, w, bias, x, gt, g, tm):
    b, l, d = x.shape
    din = a.shape[2]
    return pl.pallas_call(
        _proj_post_kernel,
        grid=(b, l // tm),
        in_specs=[
            pl.BlockSpec((1, tm, din), lambda bi, i: (bi, i, 0)),
            pl.BlockSpec((din, d), lambda bi, i: (0, 0)),
            pl.BlockSpec((1, d), lambda bi, i: (0, 0)),
            pl.BlockSpec((1, tm, d), lambda bi, i: (bi, i, 0)),
            pl.BlockSpec((1, 1, d), lambda bi, i: (bi, 0, 0)),
            pl.BlockSpec((1, d), lambda bi, i: (0, 0)),
        ],
        out_specs=pl.BlockSpec((1, tm, d), lambda bi, i: (bi, i, 0)),
        out_shape=jax.ShapeDtypeStruct((b, l, d), F32),
        compiler_params=_cparams("parallel", "parallel"),
        name="proj_post",
    )(a, w, bias, x, gt, g)


def _mlp_kernel(x_ref, sh_ref, sc_ref, gt_ref, g1_ref, g2_ref, w1_ref, w2_ref, o_ref, *, ff_chunk):
    x = x_ref[0]
    h = (_rms(x, g1_ref[...]) * (1.0 + sc_ref[0]) + sh_ref[0]).astype(BF16)
    dff = w1_ref.shape[1]
    m = None
    for c in range(dff // ff_chunk):
        u = _dot(h, w1_ref[:, c * ff_chunk:(c + 1) * ff_chunk])
        u = jnp.maximum(u, 0.0)
        part = _dot((u * u).astype(BF16), w2_ref[c * ff_chunk:(c + 1) * ff_chunk, :])
        m = part if m is None else m + part
    o_ref[0] = x + gt_ref[0] * _rms(m, g2_ref[...])


def _mlp(x, sh, sc, gt, g1, g2, w1, w2, tm):
    b, l, d = x.shape
    dff = w1.shape[1]
    vec = pl.BlockSpec((1, 1, d), lambda bi, i: (bi, 0, 0))
    row = pl.BlockSpec((1, d), lambda bi, i: (0, 0))
    return pl.pallas_call(
        functools.partial(_mlp_kernel, ff_chunk=1024),
        grid=(b, l // tm),
        in_specs=[
            pl.BlockSpec((1, tm, d), lambda bi, i: (bi, i, 0)),
            vec, vec, vec, row, row,
            pl.BlockSpec((d, dff), lambda bi, i: (0, 0)),
            pl.BlockSpec((dff, d), lambda bi, i: (0, 0)),
        ],
        out_specs=pl.BlockSpec((1, tm, d), lambda bi, i: (bi, i, 0)),
        out_shape=jax.ShapeDtypeStruct((b, l, d), F32),
        compiler_params=_cparams("parallel", "parallel"),
        name="mlp",
    )(x, sh, sc, gt, g1, g2, w1, w2)


HALO = 8


def _hy_in_kernel(x_ref, xp_ref, xn_ref, sh_ref, sc_ref, g_ref, w_ref, b_ref, ws_ref, bs_ref,
                  x1_ref, x2_ref, v_ref, u_scr):
    i = pl.program_id(1)
    n = pl.num_programs(1)
    tm = x_ref.shape[1]
    d = x_ref.shape[2]
    xa = jnp.concatenate([xp_ref[0], x_ref[0], xn_ref[0]], axis=0)
    h = (_rms(xa, g_ref[...]) * (1.0 + sc_ref[0]) + sh_ref[0]).astype(BF16)
    u_scr[...] = _dot(h, w_ref[...]) + b_ref[...]

    @pl.when(i == 0)
    def _():
        u_scr[0:HALO, :] = jnp.zeros((HALO, 3 * d), F32)

    @pl.when(i == n - 1)
    def _():
        u_scr[tm + HALO:tm + 2 * HALO, :] = jnp.zeros((HALO, 3 * d), F32)

    for j, o_ref in enumerate((x1_ref, x2_ref, v_ref)):
        sl = slice(j * d, (j + 1) * d)
        y = (u_scr[HALO - 1:HALO - 1 + tm, sl] * ws_ref[0:1, sl]
             + u_scr[HALO:HALO + tm, sl] * ws_ref[1:2, sl]
             + u_scr[HALO + 1:HALO + 1 + tm, sl] * ws_ref[2:3, sl]
             + bs_ref[:, sl])
        o_ref[0] = y.astype(BF16)


def _hy_in(x, sh, sc, g, w_in, b_in, w_short, b_short, tm):
    b, l, d = x.shape
    nt = l // tm
    tb = tm // HALO
    nb = l // HALO
    vec = pl.BlockSpec((1, 1, d), lambda bi, i: (bi, 0, 0))
    out = pl.BlockSpec((1, tm, d), lambda bi, i: (bi, i, 0))
    return pl.pallas_call(
        _hy_in_kernel,
        grid=(b, nt),
        in_specs=[
            pl.BlockSpec((1, tm, d), lambda bi, i: (bi, i, 0)),
            pl.BlockSpec((1, HALO, d), lambda bi, i: (bi, jnp.maximum(i * tb - 1, 0), 0)),
            pl.BlockSpec((1, HALO, d), lambda bi, i: (bi, jnp.minimum((i + 1) * tb, nb - 1), 0)),
            vec, vec,
            pl.BlockSpec((1, d), lambda bi, i: (0, 0)),
            pl.BlockSpec((d, 3 * d), lambda bi, i: (0, 0)),
            pl.BlockSpec((1, 3 * d), lambda bi, i: (0, 0)),
            pl.BlockSpec((3, 3 * d), lambda bi, i: (0, 0)),
            pl.BlockSpec((1, 3 * d), lambda bi, i: (0, 0)),
        ],
        out_specs=[out, out, out],
        out_shape=[jax.ShapeDtypeStruct((b, l, d), BF16)] * 3,
        scratch_shapes=[pltpu.VMEM((tm + 2 * HALO, 3 * d), F32)],
        compiler_params=_cparams("parallel", "parallel"),
        name="hyena_in",
    )(x, x, x, sh, sc, g, w_in, b_in, w_short, b_short)


def _filt_kernel(z_ref, w1_ref, b1_ref, w2_ref, b2_ref, w3_ref, fr_ref, dl_ref, o_ref, *, l):
    tl = z_ref.shape[0]
    d = dl_ref.shape[1]
    fr = fr_ref[...]
    hdn = jnp.sin(fr * (_dot3(z_ref[...], w1_ref[...]) + b1_ref[...]))
    hdn = jnp.sin(fr * (_dot3(hdn, w2_ref[...]) + b2_ref[...]))
    hf = _dot3(hdn, w3_ref[...])
    rows = pl.program_id(0) * tl + lax.broadcasted_iota(jnp.int32, (tl, d), 0)
    t = rows.astype(F32) * (1.0 / (l - 1))
    decay = jnp.exp(-t * jnp.abs(dl_ref[...]))
    for j in range(2 * HY_ORDER):
        f = hf[:, j * d:(j + 1) * d] * decay
        if j % 2 == 1:
            f = jnp.where(rows == 0, 0.0, f)
        o_ref[j] = f.astype(BF16)


def _filters(l, d, f_w1, f_b1, f_w2, f_b2, f_w3, f_freq):
    t = jnp.linspace(0.0, 1.0, l, dtype=F32)[:, None]
    w = (2.0 * math.pi) * jnp.arange(l, dtype=F32)[:, None] / l
    f = jnp.linspace(1e-4, HY_BANDS - 1, HY_BANDS, dtype=F32)[None, :]
    kz = HY_FILTER_HIDDEN
    z = jnp.concatenate([t, jnp.cos(f * w), -jnp.sin(f * w), jnp.zeros((l, kz - HY_EMB), F32)], axis=-1)
    w1 = jnp.concatenate([f_w1, jnp.zeros((kz - HY_EMB, HY_FILTER_HIDDEN), F32)], axis=0)
    min_decay = math.log(HY_TARGET) / HY_SLOW_PCT
    max_decay = math.log(HY_TARGET) / HY_FAST_PCT
    deltas = jnp.linspace(min_decay, max_decay, d, dtype=F32)[None, :]
    tl = 256
    full = lambda a: pl.BlockSpec(a.shape, lambda i: (0,) * a.ndim)
    ops = [w1, f_b1.reshape(1, -1), f_w2, f_b2.reshape(1, -1), f_w3, f_freq.reshape(1, -1), deltas]
    return pl.pallas_call(
        functools.partial(_filt_kernel, l=l),
        grid=(l // tl,),
        in_specs=[pl.BlockSpec((tl, kz), lambda i: (i, 0))] + [full(a) for a in ops],
        out_specs=pl.BlockSpec((2 * HY_ORDER, tl, d), lambda i: (0, i, 0)),
        out_shape=jax.ShapeDtypeStruct((2 * HY_ORDER, l, d), BF16),
        compiler_params=_cparams("parallel"),
        name="hyena_filters",
    )(z, *ops)


def _dft_tables(l):
    n = 2 * l
    n2 = DFT_N2
    n1 = n // n2
    n1h = n1 // 2
    slots = n1h + 8
    k1 = jnp.arange(slots)
    valid = (k1 <= n1h)[:, None]
    a = jnp.arange(n1h)
    ang1 = (2.0 * math.pi / n1) * ((k1[:, None] * a[None, :]) % n1).astype(F32)
    f1 = jnp.concatenate([jnp.where(valid, jnp.cos(ang1), 0.0), jnp.where(valid, -jnp.sin(ang1), 0.0)], axis=0)
    kk = jnp.arange(n1h)
    wgt = jnp.where(kk == 0, 1.0, 2.0)[None, :] / n
    ang2 = (2.0 * math.pi / n1) * ((a[:, None] * kk[None, :]) % n1).astype(F32)
    f2 = jnp.concatenate([wgt * jnp.cos(ang2), -wgt * jnp.sin(ang2)], axis=1)
    kg = jnp.arange(n1h + 1)
    k2 = jnp.arange(n2)
    j = jnp.arange(n2)
    prod = (j[None, None, :] * (kg[:, None, None] + n1 * k2[None, :, None])) % n
    th = (2.0 * math.pi / n) * prod.astype(F32)
    c, s = jnp.cos(th), jnp.sin(th)
    g = jnp.concatenate([jnp.concatenate([c, s], axis=2), jnp.concatenate([-s, c], axis=2)], axis=1)
    ct, st = jnp.swapaxes(c, 1, 2), jnp.swapaxes(s, 1, 2)
    gi = jnp.concatenate([jnp.concatenate([ct, -st], axis=2), jnp.concatenate([st, ct], axis=2)], axis=1)
    return {"n": n, "n1h": n1h, "slots": slots,
            "f1": f1.astype(BF16), "f2": f2.astype(BF16), "g": g.astype(BF16), "gi": gi.astype(BF16)}


def _stage1_kernel(f_ref, u_ref, a_ref):
    a_ref[0] = _dot(f_ref[...], u_ref[0]).astype(BF16)


def _stage1(u, tab):
    b, l, d = u.shape
    n1h, slots = tab["n1h"], tab["slots"]
    cols = DFT_N2 * d
    tc = 4096
    a = pl.pallas_call(
        _stage1_kernel,
        grid=(b, cols // tc),
        in_specs=[
            pl.BlockSpec((2 * slots, n1h), lambda bi, j: (0, 0)),
            pl.BlockSpec((1, n1h, tc), lambda bi, j: (bi, 0, j)),
        ],
        out_specs=pl.BlockSpec((1, 2 * slots, tc), lambda bi, j: (bi, 0, j)),
        out_shape=jax.ShapeDtypeStruct((b, 2 * slots, cols), BF16),
        compiler_params=_cparams("parallel", "parallel"),
        name="dft_stage1",
    )(tab["f1"], u.reshape(b, n1h, cols))
    return a.reshape(b, 2, slots, DFT_N2, d)


def _cplx_rows(ref):
    n2, d = ref.shape[-2], ref.shape[-1]
    return ref[...].reshape(2 * n2, d)


def _filt_spec_kernel(g_ref, af_ref, ab_ref, bias_ref, k_ref):
    n2 = af_ref.shape[-2]
    xf = _dot(g_ref[0], _cplx_rows(af_ref))
    xb = _dot(g_ref[0], _cplx_rows(ab_ref))
    k_ref[0, 0, 0] = (xf[:n2] + xb[:n2] + bias_ref[0]).astype(BF16)
    k_ref[0, 1, 0] = (xf[n2:] - xb[n2:]).astype(BF16)


def _filt_spectrum(a_filt, bias, tab):
    _, _, slots, n2, d = a_filt.shape
    n1h = tab["n1h"]
    blk = lambda seq: pl.BlockSpec((1, 2, 1, n2, d), lambda o, k: (2 * o + seq, 0, k, 0, 0))
    return pl.pallas_call(
        _filt_spec_kernel,
        grid=(HY_ORDER, slots),
        in_specs=[
            pl.BlockSpec((1, 2 * n2, 2 * n2), lambda o, k: (jnp.minimum(k, n1h), 0, 0)),
            blk(0), blk(1),
            pl.BlockSpec((1, 1, d), lambda o, k: (o, 0, 0)),
        ],
        out_specs=pl.BlockSpec((1, 2, 1, n2, d), lambda o, k: (o, 0, k, 0, 0)),
        out_shape=jax.ShapeDtypeStruct((HY_ORDER, 2, slots, n2, d), BF16),
        compiler_params=_cparams("parallel", "parallel"),
        name="hyena_filter_spectrum",
    )(tab["g"], a_filt, a_filt, bias.reshape(HY_ORDER, 1, d))


def _mid_kernel(g_ref, gi_ref, a_ref, k_ref, z_ref):
    n2 = a_ref.shape[-2]
    x = _dot(g_ref[0], _cplx_rows(a_ref))
    xr, xi = x[:n2], x[n2:]
    kr = k_ref[0, 0, 0].astype(F32)
    ki = k_ref[0, 1, 0].astype(F32)
    y = jnp.concatenate([xr * kr - xi * ki, xr * ki + xi * kr], axis=0).astype(BF16)
    z = _dot(gi_ref[0], y)
    z_ref[0, 0, 0] = z[:n2].astype(BF16)
    z_ref[0, 1, 0] = z[n2:].astype(BF16)


def _mid(a, kf, order, tab):
    b, _, slots, n2, d = a.shape
    n1h = tab["n1h"]
    gspec = pl.BlockSpec((1, 2 * n2, 2 * n2), lambda bi, k: (jnp.minimum(k, n1h), 0, 0))
    return pl.pallas_call(
        _mid_kernel,
        grid=(b, slots),
        in_specs=[
            gspec, gspec,
            pl.BlockSpec((1, 2, 1, n2, d), lambda bi, k: (bi, 0, k, 0, 0)),
            pl.BlockSpec((1, 2, 1, n2, d), lambda bi, k: (order, 0, k, 0, 0)),
        ],
        out_specs=pl.BlockSpec((1, 2, 1, n2, d), lambda bi, k: (bi, 0, k, 0, 0)),
        out_shape=jax.ShapeDtypeStruct((b, 2, slots, n2, d), BF16),
        compiler_params=_cparams("parallel", "parallel"),
        name="dft_mid",
    )(tab["g"], tab["gi"], a, kf)


def _stage2_kernel(f_ref, zr_ref, zi_ref, zt_ref, x_ref, o_ref, *, inv_n):
    z = jnp.concatenate([zr_ref[0, 0], zi_ref[0, 0]], axis=0)
    y = _dot(f_ref[...], z)
    rows = lax.broadcasted_iota(jnp.int32, y.shape, 0)
    nyq = zt_ref[0, 0, 0:1, :].astype(F32) * inv_n
    y = y + jnp.where(rows % 2 == 0, nyq, -nyq)
    o_ref[0] = (y * x_ref[0].astype(F32)).astype(BF16)


def _stage2(z, gate, tab):
    b, _, slots, n2, d = z.shape
    n1h = tab["n1h"]
    cols = n2 * d
    tc = 4096
    zf = z.reshape(b, 2, slots, cols)
    l = n1h * n2
    out = pl.pallas_call(
        functools.partial(_stage2_kernel, inv_n=1.0 / tab["n"]),
        grid=(b, cols // tc),
        in_specs=[
            pl.BlockSpec((n1h, 2 * n1h), lambda bi, j: (0, 0)),
            pl.BlockSpec((1, 1, n1h, tc), lambda bi, j: (bi, 0, 0, j)),
            pl.BlockSpec((1, 1, n1h, tc), lambda bi, j: (bi, 1, 0, j)),
            pl.BlockSpec((1, 1, 8, tc), lambda bi, j: (bi, 0, n1h // 8, j)),
            pl.BlockSpec((1, n1h, tc), lambda bi, j: (bi, 0, j)),
        ],
        out_specs=pl.BlockSpec((1, n1h, tc), lambda bi, j: (bi, 0, j)),
        out_shape=jax.ShapeDtypeStruct((b, n1h, cols), BF16),
        compiler_params=_cparams("parallel", "parallel"),
        name="dft_stage2",
    )(tab["f2"], zf, zf, zf, gate.reshape(b, n1h, cols))
    return out.reshape(b, l, d)


def _small_dft_tables(l):
    n = 2 * l
    k = jnp.arange(l)
    t = jnp.arange(l)
    ang = (2.0 * math.pi / n) * ((k[:, None] * t[None, :]) % n).astype(F32)
    sign = jnp.where(t % 2 == 0, 1.0, -1.0)[None, :]
    fwd = jnp.concatenate([jnp.cos(ang), -jnp.sin(ang), sign, jnp.zeros((7, l), F32)], axis=0)
    wgt = jnp.where(k == 0, 1.0, 2.0)[None, :] / n
    ang_t = ang.T
    inv = jnp.concatenate([wgt * jnp.cos(ang_t), -wgt * jnp.sin(ang_t)], axis=1)
    return {"n": n, "fwd": fwd.astype(BF16), "inv": inv.astype(BF16)}


def _small_spec_kernel(f_ref, hf_ref, hb_ref, bias_ref, k_ref):
    l = hf_ref.shape[1]
    xf = _dot(f_ref[...], hf_ref[0])
    xb = _dot(f_ref[...], hb_ref[0])
    bias = bias_ref[0]
    k_ref[0, 0:l, :] = (xf[:l] + xb[:l] + bias).astype(BF16)
    k_ref[0, l:2 * l, :] = (xf[l:2 * l] - xb[l:2 * l]).astype(BF16)
    k_ref[0, 2 * l:, :] = (xf[2 * l:] + xb[2 * l:] + bias).astype(BF16)


def _small_filt_spectrum(filt, bias, tab):
    _, l, d = filt.shape
    rows = 2 * l + 8
    tc = 512
    return pl.pallas_call(
        _small_spec_kernel,
        grid=(HY_ORDER, d // tc),
        in_specs=[
            pl.BlockSpec((rows, l), lambda o, j: (0, 0)),
            pl.BlockSpec((1, l, tc), lambda o, j: (2 * o, 0, j)),
            pl.BlockSpec((1, l, tc), lambda o, j: (2 * o + 1, 0, j)),
            pl.BlockSpec((1, 1, tc), lambda o, j: (o, 0, j)),
        ],
        out_specs=pl.BlockSpec((1, rows, tc), lambda o, j: (o, 0, j)),
        out_shape=jax.ShapeDtypeStruct((HY_ORDER, rows, d), BF16),
        compiler_params=_cparams("parallel", "parallel"),
        name="hyena_filter_spectrum_ctx",
    )(tab["fwd"], filt, filt, bias.reshape(HY_ORDER, 1, d))


def _small_conv_kernel(f_ref, fi_ref, u_ref, k_ref, x_ref, o_ref, *, inv_n):
    l = u_ref.shape[1]
    x = _dot(f_ref[...], u_ref[0])
    xr, xi = x[:l], x[l:2 * l]
    kr = k_ref[0, 0:l, :].astype(F32)
    ki = k_ref[0, l:2 * l, :].astype(F32)
    y = jnp.concatenate([xr * kr - xi * ki, xr * ki + xi * kr], axis=0).astype(BF16)
    out = _dot(fi_ref[...], y)
    nyq = x[2 * l:2 * l + 1] * k_ref[0, 2 * l:2 * l + 1, :].astype(F32) * inv_n
    rows = lax.broadcasted_iota(jnp.int32, out.shape, 0)
    out = out + jnp.where(rows % 2 == 0, nyq, -nyq)
    o_ref[0] = (out * x_ref[0].astype(F32)).astype(BF16)


def _small_conv(u, kf, order, gate, tab):
    b, l, d = u.shape
    rows = 2 * l + 8
    tc = 512
    return pl.pallas_call(
        functools.partial(_small_conv_kernel, inv_n=1.0 / tab["n"]),
        grid=(b, d // tc),
        in_specs=[
            pl.BlockSpec((rows, l), lambda bi, j: (0, 0)),
            pl.BlockSpec((l, 2 * l), lambda bi, j: (0, 0)),
            pl.BlockSpec((1, l, tc), lambda bi, j: (bi, 0, j)),
            pl.BlockSpec((1, rows, tc), lambda bi, j: (order, 0, j)),
            pl.BlockSpec((1, l, tc), lambda bi, j: (bi, 0, j)),
        ],
        out_specs=pl.BlockSpec((1, l, tc), lambda bi, j: (bi, 0, j)),
        out_shape=jax.ShapeDtypeStruct((b, l, d), BF16),
        compiler_params=_cparams("parallel", "parallel"),
        name="long_conv_ctx",
    )(tab["fwd"], tab["inv"], u, kf, gate)


def _row_tile(l):
    return 512 if l % 512 == 0 else 256


def _hyena_mixer_lat(x, sh, sc, g, p, tabs):
    b, l, d = x.shape
    x1, x2, v = _hy_in(x, sh, sc, g, p["w_in"], p["b_in"], p["w_short"], p["b_short"], _row_tile(l))
    filt = _filters(l, d, *p["filter"])
    kf = _filt_spectrum(_stage1(filt, tabs), p["bias"], tabs)
    z = _stage2(_mid(_stage1(v, tabs), kf, 0, tabs), x1, tabs)
    z = _stage2(_mid(_stage1(z, tabs), kf, 1, tabs), x2, tabs)
    return z


def _hyena_mixer_ctx(x, sh, sc, g, p, tabs):
    b, l, d = x.shape
    x1, x2, v = _hy_in(x, sh, sc, g, p["w_in"], p["b_in"], p["w_short"], p["b_short"], l)
    filt = _filters(l, d, *p["filter"])
    kf = _small_filt_spectrum(filt, p["bias"], tabs)
    z = _small_conv(v, kf, 0, x1, tabs)
    z = _small_conv(z, kf, 1, x2, tabs)
    return z


def kernel(x, c, ctx, c_ctx, ada_w, ada_b, norm_g, mla_w_dq, mla_g_q, mla_w_uq, mla_w_dkv, mla_g_kv,
           mla_w_ukv, mla_w_o, hy_w_in, hy_b_in, hy_w_short, hy_b_short, hy_f_w1, hy_f_b1, hy_f_w2,
           hy_f_b2, hy_f_w3, hy_f_freq, hy_bias, hy_w_out, hy_b_out, mlp_w1, mlp_w2):
    b, l, d = x.shape
    cl = ctx.shape[1]
    depth = ada_w.shape[0]
    assert b + 1 <= 8 and l % 256 == 0 and cl % 256 == 0

    cvec = jnp.concatenate([c, c_ctx[None, :], jnp.zeros((8 - b - 1, d), F32)], axis=0)
    mods = _ada(cvec, ada_w, ada_b)

    rope_lat = _rope_tables(l)
    rope_ctx = _no_rope_tables(cl)
    tabs_lat = _dft_tables(l)
    tabs_ctx = _small_dft_tables(cl)
    zero_bias = jnp.zeros((1, d), F32)
    tm = _row_tile(l)

    xc = ctx
    for i in range(depth):
        last = i == depth - 1
        j = i // 2
        g = norm_g[i].reshape(4, 1, d)
        m_lat = mods[i, :b].reshape(b, 6, 1, d)
        m_ctx = jnp.broadcast_to(mods[i, b].reshape(1, 6, 1, d), (b, 6, 1, d))
        lat = [m_lat[:, k] for k in range(6)]
        cx = [m_ctx[:, k] for k in range(6)]

        if i % 2 == 0:
            w = _mla_weights(mla_w_dq[j], mla_g_q[j], mla_w_uq[j], mla_w_dkv[j], mla_g_kv[j], mla_w_ukv[j])
            wo = mla_w_o[j].astype(BF16)
            qc, kc, vtc = _qkv(xc, cx[0], cx[1], g[0], w, *rope_ctx, cl)
            ql, kl, vtl = _qkv(x, lat[0], lat[1], g[0], w, *rope_lat, tm)
            o_lat = _attention(ql, kc, vtc, kl, vtl, tq=256)
            x = _proj_post(o_lat, wo, zero_bias, x, lat[2], g[1], tm)
            if not last:
                o_ctx = _attention(qc, kc, vtc, tq=cl)
                xc = _proj_post(o_ctx, wo, zero_bias, xc, cx[2], g[1], cl)
        else:
            p = {
                "w_in": hy_w_in[j].astype(BF16), "b_in": hy_b_in[j].reshape(1, -1),
                "w_short": hy_w_short[j], "b_short": hy_b_short[j].reshape(1, -1),
                "filter": (hy_f_w1[j], hy_f_b1[j], hy_f_w2[j], hy_f_b2[j], hy_f_w3[j], hy_f_freq[j]),
                "bias": hy_bias[j],
            }
            wo = hy_w_out[j].astype(BF16)
            bo = hy_b_out[j].reshape(1, d)
            z_lat = _hyena_mixer_lat(x, lat[0], lat[1], g[0], p, tabs_lat)
            x = _proj_post(z_lat, wo, bo, x, lat[2], g[1], tm)
            if not last:
                z_ctx = _hyena_mixer_ctx(xc, cx[0], cx[1], g[0], p, tabs_ctx)
                xc = _proj_post(z_ctx, wo, bo, xc, cx[2], g[1], cl)

        w1 = mlp_w1[i].astype(BF16)
        w2 = mlp_w2[i].astype(BF16)
        x = _mlp(x, lat[3], lat[4], lat[5], g[2], g[3], w1, w2, tm)
        if not last:
            xc = _mlp(xc, cx[3], cx[4], cx[5], g[2], g[3], w1, w2, cl)
    return x
```

```python
import functools
import math

import numpy as np
import jax
import jax.numpy as jnp
from jax import lax
from jax.experimental import pallas as pl
from jax.experimental.pallas import tpu as pltpu

F32 = jnp.float32
BF16 = jnp.bfloat16

GRID_W = 64
MLA_HEADS = 16
QK_NOPE = 64
QK_ROPE = 32
V_DIM = 64
Q_RANK = 256
KV_RANK = 128
ROPE_BASE = 10000.0
HY_ORDER = 2
HY_EMB = 33
HY_BANDS = (HY_EMB - 1) // 2
HY_FILTER_HIDDEN = 64
HY_TARGET = 1e-2
HY_FAST_PCT = 0.3
HY_SLOW_PCT = 1.5
EPS = 1e-6

LANES = 128
HEAD_PAD = LANES
VT_ROWS = 80
ONES_ROW = V_DIM
VMEM_LIMIT = 56 * 1024 * 1024
DFT_N2 = 128
ATTN_TQ = 512
ATTN_TK = 256
ATTN_UNROLL = 6

LOG2E = 1.4426950408889634


def _cparams(*sem):
    return pltpu.CompilerParams(dimension_semantics=sem, vmem_limit_bytes=VMEM_LIMIT)


def _dot(a, b):
    return jnp.dot(a, b, preferred_element_type=F32)


def _dot3(a, b):
    ah = a.astype(BF16)
    al = (a - ah.astype(F32)).astype(BF16)
    bh = b.astype(BF16)
    bl = (b - bh.astype(F32)).astype(BF16)
    return _dot(ah, bh) + _dot(al, bh) + _dot(ah, bl)


def _rms(x, g):
    return x * lax.rsqrt(jnp.mean(x * x, axis=-1, keepdims=True) + EPS) * g


def _ada_kernel(c_ref, w_ref, b_ref, o_ref):
    c = c_ref[...]
    s = c / (1.0 + jnp.exp(-c))
    o_ref[0] = _dot3(s, w_ref[0]) + b_ref[0]


def _ada(cvec, ada_w, ada_b):
    depth, d, n6 = ada_w.shape
    tn = 1536
    return pl.pallas_call(
        _ada_kernel,
        grid=(depth, n6 // tn),
        in_specs=[
            pl.BlockSpec((8, d), lambda i, j: (0, 0)),
            pl.BlockSpec((1, d, tn), lambda i, j: (i, 0, j)),
            pl.BlockSpec((1, 1, tn), lambda i, j: (i, 0, j)),
        ],
        out_specs=pl.BlockSpec((1, 8, tn), lambda i, j: (i, 0, j)),
        out_shape=jax.ShapeDtypeStruct((depth, 8, n6), F32),
        compiler_params=_cparams("parallel", "parallel"),
        name="ada",
    )(cvec, ada_w, ada_b.reshape(depth, 1, n6))


def _qkv_kernel(x_ref, sh_ref, sc_ref, g_ref, wd_ref, gq_ref, gkv_ref, wqa_ref, wqb_ref, wk_ref,
                wvt_ref, cos_ref, sin_ref, q_ref, k_ref, vt_ref, *, qscale):
    h = _rms(x_ref[0], g_ref[...]) * (1.0 + sc_ref[0]) + sh_ref[0]
    t = _dot(h.astype(BF16), wd_ref[...])
    cq = _rms(t[:, :Q_RANK], gq_ref[...]).astype(BF16)
    ckv = _rms(t[:, Q_RANK:Q_RANK + KV_RANK], gkv_ref[...]).astype(BF16)
    cos = cos_ref[...]
    sin = sin_ref[...]
    o = Q_RANK + KV_RANK
    kr = t[:, o:o + LANES] * cos + t[:, o + LANES:o + 2 * LANES] * sin
    qa = _dot(cq, wqa_ref[...])
    qb = _dot(cq, wqb_ref[...])
    kn = _dot(ckv, wk_ref[...])
    for hd in range(MLA_HEADS):
        sl = slice(hd * HEAD_PAD, (hd + 1) * HEAD_PAD)
        q_ref[0, :, sl] = ((qa[:, sl] * cos + qb[:, sl] * sin) * qscale).astype(BF16)
        k_ref[0, :, sl] = (kn[:, sl] + kr).astype(BF16)
    vt = lax.dot_general(wvt_ref[...], ckv, (((1,), (1,)), ((), ())), preferred_element_type=F32)
    tm = vt.shape[1]
    vt = vt.reshape(MLA_HEADS, VT_ROWS, tm)
    ones = lax.broadcasted_iota(jnp.int32, vt.shape, 1) == ONES_ROW
    vt_ref[0, 0] = jnp.where(ones, 1.0, vt).astype(BF16)


def _qkv(x, sh, sc, g, w, cos_t, sin_t, tm):
    b, l, d = x.shape
    nt = l // tm
    hw = MLA_HEADS * HEAD_PAD
    qscale = LOG2E / math.sqrt(QK_NOPE + QK_ROPE)
    full = lambda a: pl.BlockSpec(a.shape, lambda bi, i: (0,) * a.ndim)
    return pl.pallas_call(
        functools.partial(_qkv_kernel, qscale=qscale),
        grid=(b, nt),
        in_specs=[
            pl.BlockSpec((1, tm, d), lambda bi, i: (bi, i, 0)),
            pl.BlockSpec((1, 1, d), lambda bi, i: (bi, 0, 0)),
            pl.BlockSpec((1, 1, d), lambda bi, i: (bi, 0, 0)),
            full(g), full(w["wd"]), full(w["gq"]), full(w["gkv"]), full(w["wqa"]), full(w["wqb"]),
            full(w["wk"]), full(w["wvt"]),
            pl.BlockSpec((tm, LANES), lambda bi, i: (i, 0)),
            pl.BlockSpec((tm, LANES), lambda bi, i: (i, 0)),
        ],
        out_specs=[
            pl.BlockSpec((1, tm, hw), lambda bi, i: (bi, i, 0)),
            pl.BlockSpec((1, tm, hw), lambda bi, i: (bi, i, 0)),
            pl.BlockSpec((1, 1, MLA_HEADS, VT_ROWS, tm), lambda bi, i: (bi, i, 0, 0, 0)),
        ],
        out_shape=[
            jax.ShapeDtypeStruct((b, l, hw), BF16),
            jax.ShapeDtypeStruct((b, l, hw), BF16),
            jax.ShapeDtypeStruct((b, nt, MLA_HEADS, VT_ROWS, tm), BF16),
        ],
        compiler_params=_cparams("parallel", "parallel"),
        name="mla_qkv",
    )(x, sh, sc, g, w["wd"], w["gq"], w["gkv"], w["wqa"], w["wqb"], w["wk"], w["wvt"], cos_t, sin_t)


def _mla_weights(w_dq, g_q, w_uq, w_dkv, g_kv, w_ukv):
    d = w_dq.shape[0]
    hq = QK_NOPE + QK_ROPE
    half = QK_ROPE // 2
    w_rope = w_dkv[:, KV_RANK:]
    w_rope_sw = jnp.concatenate([w_rope[:, half:], w_rope[:, :half]], axis=1)
    zl = jnp.zeros((d, QK_NOPE), F32)
    zr = jnp.zeros((d, HEAD_PAD - hq), F32)
    wd = jnp.concatenate([w_dq, w_dkv[:, :KV_RANK], zl, w_rope, zr, zl, w_rope_sw, zr], axis=1)
    wq = w_uq.reshape(Q_RANK, MLA_HEADS, hq)
    zq = jnp.zeros((Q_RANK, MLA_HEADS, HEAD_PAD - hq), F32)
    wqa = jnp.concatenate([wq, zq], axis=2).reshape(Q_RANK, MLA_HEADS * HEAD_PAD)
    zn = jnp.zeros((Q_RANK, MLA_HEADS, QK_NOPE), F32)
    wqb = jnp.concatenate([zn, wq[:, :, QK_NOPE + half:], wq[:, :, QK_NOPE:QK_NOPE + half], zq],
                          axis=2).reshape(Q_RANK, MLA_HEADS * HEAD_PAD)
    wkv = w_ukv.reshape(KV_RANK, MLA_HEADS, QK_NOPE + V_DIM)
    zk = jnp.zeros((KV_RANK, MLA_HEADS, HEAD_PAD - QK_NOPE), F32)
    wk = jnp.concatenate([wkv[:, :, :QK_NOPE], zk], axis=2).reshape(KV_RANK, MLA_HEADS * HEAD_PAD)
    wv = jnp.transpose(wkv[:, :, QK_NOPE:], (1, 2, 0))
    wvt = jnp.concatenate([wv, jnp.zeros((MLA_HEADS, VT_ROWS - V_DIM, KV_RANK), F32)], axis=1)
    return {
        "wd": wd.astype(BF16), "gq": g_q.reshape(1, -1), "gkv": g_kv.reshape(1, -1),
        "wqa": wqa.astype(BF16), "wqb": wqb.astype(BF16), "wk": wk.astype(BF16),
        "wvt": wvt.reshape(MLA_HEADS * VT_ROWS, KV_RANK).astype(BF16),
    }


def _rope_tables(l):
    t = jnp.arange(l)
    row = (t // GRID_W).astype(F32)
    col = (t % GRID_W).astype(F32)
    n_freq = QK_ROPE // 4
    inv = ROPE_BASE ** (-jnp.arange(n_freq, dtype=F32) / n_freq)
    ang = jnp.concatenate([row[:, None] * inv, col[:, None] * inv], axis=-1)
    cos, sin = jnp.cos(ang), jnp.sin(ang)
    ones = jnp.ones((l, QK_NOPE), F32)
    zl = jnp.zeros((l, QK_NOPE), F32)
    zr = jnp.zeros((l, HEAD_PAD - QK_NOPE - QK_ROPE), F32)
    cos_t = jnp.concatenate([ones, cos, cos, zr], axis=1)
    sin_t = jnp.concatenate([zl, -sin, sin, zr], axis=1)
    return cos_t, sin_t


def _no_rope_tables(l):
    keep = jnp.concatenate([jnp.ones((l, QK_NOPE + QK_ROPE), F32),
                            jnp.zeros((l, HEAD_PAD - QK_NOPE - QK_ROPE), F32)], axis=1)
    return keep, jnp.zeros((l, HEAD_PAD), F32)


def _attn_kernel(*refs, n_lat_chunks, tk, unroll):
    heads = (0, 1)
    sls = [slice(hd * HEAD_PAD, (hd + 1) * HEAD_PAD) for hd in heads]
    if n_lat_chunks:
        q_ref, kc_ref, vtc_ref, k_ref, vt_ref, o_ref = refs[:6]
        scr = refs[6:]
        s_scr = [scr[0:2], scr[2:4]]
        p_scr = [scr[4:6], scr[6:8]]
        acc_scr = scr[8:10]
    else:
        q_ref, kc_ref, vtc_ref, o_ref = refs

    def scores(k, hd):
        return lax.dot_general(k, q_ref[0, :, sls[hd]], (((1,), (1,)), ((), ())), preferred_element_type=F32)

    init = []
    for hd in heads:
        s = scores(kc_ref[0, :, sls[hd]], hd)
        m = jnp.max(s, axis=0, keepdims=True)
        p = jnp.exp2(s - m).astype(BF16)
        init.append((m, _dot(vtc_ref[0, 0, hd], p)))

    if not n_lat_chunks:
        outs = [acc[:V_DIM] / acc[ONES_ROW:ONES_ROW + 1] for _, acc in init]
        o_ref[0] = jnp.concatenate(outs, axis=0).T.astype(BF16)
        return

    def stage_a(hd, slot, c):
        start = pl.multiple_of(c * tk, tk)
        s = scores(k_ref[0, pl.ds(start, tk), sls[hd]], hd)
        s_scr[hd][slot][...] = s
        return jnp.max(s.reshape(tk // 8, 8, s.shape[1]), axis=0)

    def stage_b(hd, slot, cm, m_old):
        m_new = jnp.maximum(m_old, jnp.max(cm, axis=0, keepdims=True))
        p_scr[hd][slot][...] = jnp.exp2(s_scr[hd][slot][...] - m_new).astype(BF16)
        return m_new, jnp.exp2(m_old - m_new)

    def stage_c(hd, slot, c, alpha):
        acc_scr[hd][...] = acc_scr[hd][...] * alpha + _dot(vt_ref[0, c, hd], p_scr[hd][slot][...])

    carry = []
    for hd in heads:
        m, acc = init[hd]
        acc_scr[hd][...] = acc
        cm0 = stage_a(hd, 0, 0)
        cm1 = stage_a(hd, 1, 1)
        m, al = stage_b(hd, 0, cm0, m)
        carry += [m, al, cm1]

    def steps(carry, c0, count):
        carry = list(carry)
        for u in range(count):
            for hd in heads:
                m, al, cm = carry[3 * hd:3 * hd + 3]
                stage_c(hd, u % 2, c0 + u, al)
                m, al = stage_b(hd, (u + 1) % 2, cm, m)
                cm = stage_a(hd, u % 2, c0 + u + 2)
                carry[3 * hd:3 * hd + 3] = [m, al, cm]
        return tuple(carry)

    peel = (n_lat_chunks - 2) % unroll
    carry = steps(carry, 0, peel)
    carry = lax.fori_loop(0, (n_lat_chunks - 2) // unroll,
                          lambda j, cr: steps(cr, peel + unroll * j, unroll), carry)
    outs = []
    for hd in heads:
        m, al, cm = carry[3 * hd:3 * hd + 3]
        stage_c(hd, 0, n_lat_chunks - 2, al)
        m, al = stage_b(hd, 1, cm, m)
        stage_c(hd, 1, n_lat_chunks - 1, al)
        acc = acc_scr[hd][...]
        outs.append(acc[:V_DIM] / acc[ONES_ROW:ONES_ROW + 1])
    o_ref[0] = jnp.concatenate(outs, axis=0).T.astype(BF16)


def _attention(q, kc, vtc, k=None, vt=None, *, tq):
    b, lq, hw = q.shape
    cl = kc.shape[1]
    nq = lq // tq
    hp = 2 * HEAD_PAD
    in_specs = [
        pl.BlockSpec((1, tq, hp), lambda bi, h, i: (bi, i, h)),
        pl.BlockSpec((1, cl, hp), lambda bi, h, i: (bi, 0, h)),
        pl.BlockSpec((1, 1, 2, VT_ROWS, cl), lambda bi, h, i: (bi, 0, h, 0, 0)),
    ]
    args = [q, kc, vtc]
    n_chunks, tk, unroll = 0, 0, 2
    scratch = []
    if k is not None:
        lk = k.shape[1]
        n_chunks, tk = vt.shape[1], vt.shape[4]
        assert n_chunks >= 2 and n_chunks % 2 == 0
        unroll = min(ATTN_UNROLL, n_chunks - 2) if n_chunks > 2 else 2
        in_specs += [
            pl.BlockSpec((1, lk, hp), lambda bi, h, i: (bi, 0, h)),
            pl.BlockSpec((1, n_chunks, 2, VT_ROWS, tk), lambda bi, h, i: (bi, 0, h, 0, 0)),
        ]
        args += [k, vt]
        scratch = ([pltpu.VMEM((tk, tq), F32)] * 4 + [pltpu.VMEM((tk, tq), BF16)] * 4
                   + [pltpu.VMEM((VT_ROWS, tq), F32)] * 2)
    return pl.pallas_call(
        functools.partial(_attn_kernel, n_lat_chunks=n_chunks, tk=tk, unroll=unroll),
        grid=(b, MLA_HEADS // 2, nq),
        in_specs=in_specs,
        out_specs=pl.BlockSpec((1, tq, 2 * V_DIM), lambda bi, h, i: (bi, i, h)),
        out_shape=jax.ShapeDtypeStruct((b, lq, MLA_HEADS * V_DIM), BF16),
        scratch_shapes=scratch,
        compiler_params=_cparams("parallel", "parallel", "arbitrary"),
        name="attention" if n_chunks else "attention_ctx",
    )(*args)


def _proj_post_kernel(a_ref, w_ref, b_ref, x_ref, gt_ref, g_ref, o_ref):
    y = _dot(a_ref[0], w_ref[...]) + b_ref[...]
    o_ref[0] = x_ref[0] + gt_ref[0] * _rms(y, g_ref[...])


def _proj_post(a, w, bias, x, gt, g, tm):
    b, l, d = x.shape
    din = a.shape[2]
    return pl.pallas_call(
        _proj_post_kernel,
        grid=(b, l // tm),
        in_specs=[
            pl.BlockSpec((1, tm, din), lambda bi, i: (bi, i, 0)),
            pl.BlockSpec((din, d), lambda bi, i: (0, 0)),
            pl.BlockSpec((1, d), lambda bi, i: (0, 0)),
            pl.BlockSpec((1, tm, d), lambda bi, i: (bi, i, 0)),
            pl.BlockSpec((1, 1, d), lambda bi, i: (bi, 0, 0)),
            pl.BlockSpec((1, d), lambda bi, i: (0, 0)),
        ],
        out_specs=pl.BlockSpec((1, tm, d), lambda bi, i: (bi, i, 0)),
        out_shape=jax.ShapeDtypeStruct((b, l, d), F32),
        compiler_params=_cparams("parallel", "parallel"),
        name="proj_post",
    )(a, w, bias, x, gt, g)


def _mlp_kernel(x_ref, sh_ref, sc_ref, gt_ref, g1_ref, g2_ref, w1_ref, w2_ref, o_ref, *, ff_chunk):
    x = x_ref[0]
    h = (_rms(x, g1_ref[...]) * (1.0 + sc_ref[0]) + sh_ref[0]).astype(BF16)
    dff = w1_ref.shape[1]
    m = None
    for c in range(dff // ff_chunk):
        u = _dot(h, w1_ref[:, c * ff_chunk:(c + 1) * ff_chunk])
        u = jnp.maximum(u, 0.0)
        part = _dot((u * u).astype(BF16), w2_ref[c * ff_chunk:(c + 1) * ff_chunk, :])
        m = part if m is None else m + part
    o_ref[0] = x + gt_ref[0] * _rms(m, g2_ref[...])


def _mlp(x, sh, sc, gt, g1, g2, w1, w2, tm):
    b, l, d = x.shape
    dff = w1.shape[1]
    vec = pl.BlockSpec((1, 1, d), lambda bi, i: (bi, 0, 0))
    row = pl.BlockSpec((1, d), lambda bi, i: (0, 0))
    return pl.pallas_call(
        functools.partial(_mlp_kernel, ff_chunk=1024),
        grid=(b, l // tm),
        in_specs=[
            pl.BlockSpec((1, tm, d), lambda bi, i: (bi, i, 0)),
            vec, vec, vec, row, row,
            pl.BlockSpec((d, dff), lambda bi, i: (0, 0)),
            pl.BlockSpec((dff, d), lambda bi, i: (0, 0)),
        ],
        out_specs=pl.BlockSpec((1, tm, d), lambda bi, i: (bi, i, 0)),
        out_shape=jax.ShapeDtypeStruct((b, l, d), F32),
        compiler_params=_cparams("parallel", "parallel"),
        name="mlp",
    )(x, sh, sc, gt, g1, g2, w1, w2)


HALO = 8


def _hy_in_kernel(x_ref, xp_ref, xn_ref, sh_ref, sc_ref, g_ref, w_ref, b_ref, ws_ref, bs_ref,
                  x1_ref, x2_ref, v_ref, u_scr):
    i = pl.program_id(1)
    n = pl.num_programs(1)
    tm = x_ref.shape[1]
    d = x_ref.shape[2]
    xa = jnp.concatenate([xp_ref[0], x_ref[0], xn_ref[0]], axis=0)
    h = (_rms(xa, g_ref[...]) * (1.0 + sc_ref[0]) + sh_ref[0]).astype(BF16)
    u_scr[...] = _dot(h, w_ref[...]) + b_ref[...]

    @pl.when(i == 0)
    def _():
        u_scr[0:HALO, :] = jnp.zeros((HALO, 3 * d), F32)

    @pl.when(i == n - 1)
    def _():
        u_scr[tm + HALO:tm + 2 * HALO, :] = jnp.zeros((HALO, 3 * d), F32)

    for j, o_ref in enumerate((x1_ref, x2_ref, v_ref)):
        sl = slice(j * d, (j + 1) * d)
        y = (u_scr[HALO - 1:HALO - 1 + tm, sl] * ws_ref[0:1, sl]
             + u_scr[HALO:HALO + tm, sl] * ws_ref[1:2, sl]
             + u_scr[HALO + 1:HALO + 1 + tm, sl] * ws_ref[2:3, sl]
             + bs_ref[:, sl])
        o_ref[0] = y.astype(BF16)


def _hy_in(x, sh, sc, g, w_in, b_in, w_short, b_short, tm):
    b, l, d = x.shape
    nt = l // tm
    tb = tm // HALO
    nb = l // HALO
    vec = pl.BlockSpec((1, 1, d), lambda bi, i: (bi, 0, 0))
    out = pl.BlockSpec((1, tm, d), lambda bi, i: (bi, i, 0))
    return pl.pallas_call(
        _hy_in_kernel,
        grid=(b, nt),
        in_specs=[
            pl.BlockSpec((1, tm, d), lambda bi, i: (bi, i, 0)),
            pl.BlockSpec((1, HALO, d), lambda bi, i: (bi, jnp.maximum(i * tb - 1, 0), 0)),
            pl.BlockSpec((1, HALO, d), lambda bi, i: (bi, jnp.minimum((i + 1) * tb, nb - 1), 0)),
            vec, vec,
            pl.BlockSpec((1, d), lambda bi, i: (0, 0)),
            pl.BlockSpec((d, 3 * d), lambda bi, i: (0, 0)),
            pl.BlockSpec((1, 3 * d), lambda bi, i: (0, 0)),
            pl.BlockSpec((3, 3 * d), lambda bi, i: (0, 0)),
            pl.BlockSpec((1, 3 * d), lambda bi, i: (0, 0)),
        ],
        out_specs=[out, out, out],
        out_shape=[jax.ShapeDtypeStruct((b, l, d), BF16)] * 3,
        scratch_shapes=[pltpu.VMEM((tm + 2 * HALO, 3 * d), F32)],
        compiler_params=_cparams("parallel", "parallel"),
        name="hyena_in",
    )(x, x, x, sh, sc, g, w_in, b_in, w_short, b_short)


def _filt_kernel(z_ref, w1_ref, b1_ref, w2_ref, b2_ref, w3_ref, fr_ref, dl_ref, o_ref, *, l):
    tl = z_ref.shape[0]
    d = dl_ref.shape[1]
    fr = fr_ref[...]
    hdn = jnp.sin(fr * (_dot3(z_ref[...], w1_ref[...]) + b1_ref[...]))
    hdn = jnp.sin(fr * (_dot3(hdn, w2_ref[...]) + b2_ref[...]))
    hf = _dot3(hdn, w3_ref[...])
    rows = pl.program_id(0) * tl + lax.broadcasted_iota(jnp.int32, (tl, d), 0)
    t = rows.astype(F32) * (1.0 / (l - 1))
    decay = jnp.exp(-t * jnp.abs(dl_ref[...]))
    for j in range(2 * HY_ORDER):
        f = hf[:, j * d:(j + 1) * d] * decay
        if j % 2 == 1:
            f = jnp.where(rows == 0, 0.0, f)
        o_ref[j] = f.astype(BF16)


def _filters(l, d, f_w1, f_b1, f_w2, f_b2, f_w3, f_freq):
    t = jnp.linspace(0.0, 1.0, l, dtype=F32)[:, None]
    w = (2.0 * math.pi) * jnp.arange(l, dtype=F32)[:, None] / l
    f = jnp.linspace(1e-4, HY_BANDS - 1, HY_BANDS, dtype=F32)[None, :]
    kz = HY_FILTER_HIDDEN
    z = jnp.concatenate([t, jnp.cos(f * w), -jnp.sin(f * w), jnp.zeros((l, kz - HY_EMB), F32)], axis=-1)
    w1 = jnp.concatenate([f_w1, jnp.zeros((kz - HY_EMB, HY_FILTER_HIDDEN), F32)], axis=0)
    min_decay = math.log(HY_TARGET) / HY_SLOW_PCT
    max_decay = math.log(HY_TARGET) / HY_FAST_PCT
    deltas = jnp.linspace(min_decay, max_decay, d, dtype=F32)[None, :]
    tl = 256
    full = lambda a: pl.BlockSpec(a.shape, lambda i: (0,) * a.ndim)
    ops = [w1, f_b1.reshape(1, -1), f_w2, f_b2.reshape(1, -1), f_w3, f_freq.reshape(1, -1), deltas]
    return pl.pallas_call(
        functools.partial(_filt_kernel, l=l),
        grid=(l // tl,),
        in_specs=[pl.BlockSpec((tl, kz), lambda i: (i, 0))] + [full(a) for a in ops],
        out_specs=pl.BlockSpec((2 * HY_ORDER, tl, d), lambda i: (0, i, 0)),
        out_shape=jax.ShapeDtypeStruct((2 * HY_ORDER, l, d), BF16),
        compiler_params=_cparams("parallel"),
        name="hyena_filters",
    )(z, *ops)


def _dft_tables(l):
    n = 2 * l
    n2 = DFT_N2
    n1 = n // n2
    n1h = n1 // 2
    slots = n1h + 8
    k1 = jnp.arange(slots)
    valid = (k1 <= n1h)[:, None]
    a = jnp.arange(n1h)
    ang1 = (2.0 * math.pi / n1) * ((k1[:, None] * a[None, :]) % n1).astype(F32)
    f1 = jnp.concatenate([jnp.where(valid, jnp.cos(ang1), 0.0), jnp.where(valid, -jnp.sin(ang1), 0.0)], axis=0)
    kk = jnp.arange(n1h)
    wgt = jnp.where(kk == 0, 1.0, 2.0)[None, :] / n
    ang2 = (2.0 * math.pi / n1) * ((a[:, None] * kk[None, :]) % n1).astype(F32)
    f2 = jnp.concatenate([wgt * jnp.cos(ang2), -wgt * jnp.sin(ang2)], axis=1)
    kg = jnp.arange(n1h + 1)
    k2 = jnp.arange(n2)
    j = jnp.arange(n2)
    prod = (j[None, None, :] * (kg[:, None, None] + n1 * k2[None, :, None])) % n
    th = (2.0 * math.pi / n) * prod.astype(F32)
    c, s = jnp.cos(th), jnp.sin(th)
    g = jnp.concatenate([jnp.concatenate([c, s], axis=2), jnp.concatenate([-s, c], axis=2)], axis=1)
    ct, st = jnp.swapaxes(c, 1, 2), jnp.swapaxes(s, 1, 2)
    gi = jnp.concatenate([jnp.concatenate([ct, -st], axis=2), jnp.concatenate([st, ct], axis=2)], axis=1)
    return {"n": n, "n1h": n1h, "slots": slots,
            "f1": f1.astype(BF16), "f2": f2.astype(BF16), "g": g.astype(BF16), "gi": gi.astype(BF16)}


def _stage1_kernel(f_ref, u_ref, a_ref):
    a_ref[0] = _dot(f_ref[...], u_ref[0]).astype(BF16)


def _stage1(u, tab):
    b, l, d = u.shape
    n1h, slots = tab["n1h"], tab["slots"]
    cols = DFT_N2 * d
    tc = 4096
    a = pl.pallas_call(
        _stage1_kernel,
        grid=(b, cols // tc),
        in_specs=[
            pl.BlockSpec((2 * slots, n1h), lambda bi, j: (0, 0)),
            pl.BlockSpec((1, n1h, tc), lambda bi, j: (bi, 0, j)),
        ],
        out_specs=pl.BlockSpec((1, 2 * slots, tc), lambda bi, j: (bi, 0, j)),
        out_shape=jax.ShapeDtypeStruct((b, 2 * slots, cols), BF16),
        compiler_params=_cparams("parallel", "parallel"),
        name="dft_stage1",
    )(tab["f1"], u.reshape(b, n1h, cols))
    return a.reshape(b, 2, slots, DFT_N2, d)


def _cplx_rows(ref):
    n2, d = ref.shape[-2], ref.shape[-1]
    return ref[...].reshape(2 * n2, d)


def _filt_spec_kernel(g_ref, af_ref, ab_ref, bias_ref, k_ref):
    n2 = af_ref.shape[-2]
    xf = _dot(g_ref[0], _cplx_rows(af_ref))
    xb = _dot(g_ref[0], _cplx_rows(ab_ref))
    k_ref[0, 0, 0] = (xf[:n2] + xb[:n2] + bias_ref[0]).astype(BF16)
    k_ref[0, 1, 0] = (xf[n2:] - xb[n2:]).astype(BF16)


def _filt_spectrum(a_filt, bias, tab):
    _, _, slots, n2, d = a_filt.shape
    n1h = tab["n1h"]
    blk = lambda seq: pl.BlockSpec((1, 2, 1, n2, d), lambda o, k: (2 * o + seq, 0, k, 0, 0))
    return pl.pallas_call(
        _filt_spec_kernel,
        grid=(HY_ORDER, slots),
        in_specs=[
            pl.BlockSpec((1, 2 * n2, 2 * n2), lambda o, k: (jnp.minimum(k, n1h), 0, 0)),
            blk(0), blk(1),
            pl.BlockSpec((1, 1, d), lambda o, k: (o, 0, 0)),
        ],
        out_specs=pl.BlockSpec((1, 2, 1, n2, d), lambda o, k: (o, 0, k, 0, 0)),
        out_shape=jax.ShapeDtypeStruct((HY_ORDER, 2, slots, n2, d), BF16),
        compiler_params=_cparams("parallel", "parallel"),
        name="hyena_filter_spectrum",
    )(tab["g"], a_filt, a_filt, bias.reshape(HY_ORDER, 1, d))


def _mid_kernel(g_ref, gi_ref, a_ref, k_ref, z_ref):
    n2 = a_ref.shape[-2]
    x = _dot(g_ref[0], _cplx_rows(a_ref))
    xr, xi = x[:n2], x[n2:]
    kr = k_ref[0, 0, 0].astype(F32)
    ki = k_ref[0, 1, 0].astype(F32)
    y = jnp.concatenate([xr * kr - xi * ki, xr * ki + xi * kr], axis=0).astype(BF16)
    z = _dot(gi_ref[0], y)
    z_ref[0, 0, 0] = z[:n2].astype(BF16)
    z_ref[0, 1, 0] = z[n2:].astype(BF16)


def _mid(a, kf, order, tab):
    b, _, slots, n2, d = a.shape
    n1h = tab["n1h"]
    gspec = pl.BlockSpec((1, 2 * n2, 2 * n2), lambda bi, k: (jnp.minimum(k, n1h), 0, 0))
    return pl.pallas_call(
        _mid_kernel,
        grid=(b, slots),
        in_specs=[
            gspec, gspec,
            pl.BlockSpec((1, 2, 1, n2, d), lambda bi, k: (bi, 0, k, 0, 0)),
            pl.BlockSpec((1, 2, 1, n2, d), lambda bi, k: (order, 0, k, 0, 0)),
        ],
        out_specs=pl.BlockSpec((1, 2, 1, n2, d), lambda bi, k: (bi, 0, k, 0, 0)),
        out_shape=jax.ShapeDtypeStruct((b, 2, slots, n2, d), BF16),
        compiler_params=_cparams("parallel", "parallel"),
        name="dft_mid",
    )(tab["g"], tab["gi"], a, kf)


def _stage2_kernel(f_ref, zr_ref, zi_ref, zt_ref, x_ref, o_ref, *, inv_n):
    z = jnp.concatenate([zr_ref[0, 0], zi_ref[0, 0]], axis=0)
    y = _dot(f_ref[...], z)
    rows = lax.broadcasted_iota(jnp.int32, y.shape, 0)
    nyq = zt_ref[0, 0, 0:1, :].astype(F32) * inv_n
    y = y + jnp.where(rows % 2 == 0, nyq, -nyq)
    o_ref[0] = (y * x_ref[0].astype(F32)).astype(BF16)


def _stage2(z, gate, tab):
    b, _, slots, n2, d = z.shape
    n1h = tab["n1h"]
    cols = n2 * d
    tc = 4096
    zf = z.reshape(b, 2, slots, cols)
    l = n1h * n2
    out = pl.pallas_call(
        functools.partial(_stage2_kernel, inv_n=1.0 / tab["n"]),
        grid=(b, cols // tc),
        in_specs=[
            pl.BlockSpec((n1h, 2 * n1h), lambda bi, j: (0, 0)),
            pl.BlockSpec((1, 1, n1h, tc), lambda bi, j: (bi, 0, 0, j)),
            pl.BlockSpec((1, 1, n1h, tc), lambda bi, j: (bi, 1, 0, j)),
            pl.BlockSpec((1, 1, 8, tc), lambda bi, j: (bi, 0, n1h // 8, j)),
            pl.BlockSpec((1, n1h, tc), lambda bi, j: (bi, 0, j)),
        ],
        out_specs=pl.BlockSpec((1, n1h, tc), lambda bi, j: (bi, 0, j)),
        out_shape=jax.ShapeDtypeStruct((b, n1h, cols), BF16),
        compiler_params=_cparams("parallel", "parallel"),
        name="dft_stage2",
    )(tab["f2"], zf, zf, zf, gate.reshape(b, n1h, cols))
    return out.reshape(b, l, d)


def _small_dft_tables(l):
    n = 2 * l
    k = jnp.arange(l)
    t = jnp.arange(l)
    ang = (2.0 * math.pi / n) * ((k[:, None] * t[None, :]) % n).astype(F32)
    sign = jnp.where(t % 2 == 0, 1.0, -1.0)[None, :]
    fwd = jnp.concatenate([jnp.cos(ang), -jnp.sin(ang), sign, jnp.zeros((7, l), F32)], axis=0)
    wgt = jnp.where(k == 0, 1.0, 2.0)[None, :] / n
    ang_t = ang.T
    inv = jnp.concatenate([wgt * jnp.cos(ang_t), -wgt * jnp.sin(ang_t)], axis=1)
    return {"n": n, "fwd": fwd.astype(BF16), "inv": inv.astype(BF16)}


def _small_spec_kernel(f_ref, hf_ref, hb_ref, bias_ref, k_ref):
    l = hf_ref.shape[1]
    xf = _dot(f_ref[...], hf_ref[0])
    xb = _dot(f_ref[...], hb_ref[0])
    bias = bias_ref[0]
    k_ref[0, 0:l, :] = (xf[:l] + xb[:l] + bias).astype(BF16)
    k_ref[0, l:2 * l, :] = (xf[l:2 * l] - xb[l:2 * l]).astype(BF16)
    k_ref[0, 2 * l:, :] = (xf[2 * l:] + xb[2 * l:] + bias).astype(BF16)


def _small_filt_spectrum(filt, bias, tab):
    _, l, d = filt.shape
    rows = 2 * l + 8
    tc = 512
    return pl.pallas_call(
        _small_spec_kernel,
        grid=(HY_ORDER, d // tc),
        in_specs=[
            pl.BlockSpec((rows, l), lambda o, j: (0, 0)),
            pl.BlockSpec((1, l, tc), lambda o, j: (2 * o, 0, j)),
            pl.BlockSpec((1, l, tc), lambda o, j: (2 * o + 1, 0, j)),
            pl.BlockSpec((1, 1, tc), lambda o, j: (o, 0, j)),
        ],
        out_specs=pl.BlockSpec((1, rows, tc), lambda o, j: (o, 0, j)),
        out_shape=jax.ShapeDtypeStruct((HY_ORDER, rows, d), BF16),
        compiler_params=_cparams("parallel", "parallel"),
        name="hyena_filter_spectrum_ctx",
    )(tab["fwd"], filt, filt, bias.reshape(HY_ORDER, 1, d))


def _small_conv_kernel(f_ref, fi_ref, u_ref, k_ref, x_ref, o_ref, *, inv_n):
    l = u_ref.shape[1]
    x = _dot(f_ref[...], u_ref[0])
    xr, xi = x[:l], x[l:2 * l]
    kr = k_ref[0, 0:l, :].astype(F32)
    ki = k_ref[0, l:2 * l, :].astype(F32)
    y = jnp.concatenate([xr * kr - xi * ki, xr * ki + xi * kr], axis=0).astype(BF16)
    out = _dot(fi_ref[...], y)
    nyq = x[2 * l:2 * l + 1] * k_ref[0, 2 * l:2 * l + 1, :].astype(F32) * inv_n
    rows = lax.broadcasted_iota(jnp.int32, out.shape, 0)
    out = out + jnp.where(rows % 2 == 0, nyq, -nyq)
    o_ref[0] = (out * x_ref[0].astype(F32)).astype(BF16)


def _small_conv(u, kf, order, gate, tab):
    b, l, d = u.shape
    rows = 2 * l + 8
    tc = 512
    return pl.pallas_call(
        functools.partial(_small_conv_kernel, inv_n=1.0 / tab["n"]),
        grid=(b, d // tc),
        in_specs=[
            pl.BlockSpec((rows, l), lambda bi, j: (0, 0)),
            pl.BlockSpec((l, 2 * l), lambda bi, j: (0, 0)),
            pl.BlockSpec((1, l, tc), lambda bi, j: (bi, 0, j)),
            pl.BlockSpec((1, rows, tc), lambda bi, j: (order, 0, j)),
            pl.BlockSpec((1, l, tc), lambda bi, j: (bi, 0, j)),
        ],
        out_specs=pl.BlockSpec((1, l, tc), lambda bi, j: (bi, 0, j)),
        out_shape=jax.ShapeDtypeStruct((b, l, d), BF16),
        compiler_params=_cparams("parallel", "parallel"),
        name="long_conv_ctx",
    )(tab["fwd"], tab["inv"], u, kf, gate)


def _row_tile(l):
    return 512 if l % 512 == 0 else 256


def _hyena_mixer_lat(x, sh, sc, g, p, tabs):
    b, l, d = x.shape
    x1, x2, v = _hy_in(x, sh, sc, g, p["w_in"], p["b_in"], p["w_short"], p["b_short"], _row_tile(l))
    filt = _filters(l, d, *p["filter"])
    kf = _filt_spectrum(_stage1(filt, tabs), p["bias"], tabs)
    z = _stage2(_mid(_stage1(v, tabs), kf, 0, tabs), x1, tabs)
    z = _stage2(_mid(_stage1(z, tabs), kf, 1, tabs), x2, tabs)
    return z


def _hyena_mixer_ctx(x, sh, sc, g, p, tabs):
    b, l, d = x.shape
    x1, x2, v = _hy_in(x, sh, sc, g, p["w_in"], p["b_in"], p["w_short"], p["b_short"], l)
    filt = _filters(l, d, *p["filter"])
    kf = _small_filt_spectrum(filt, p["bias"], tabs)
    z = _small_conv(v, kf, 0, x1, tabs)
    z = _small_conv(z, kf, 1, x2, tabs)
    return z


def kernel(x, c, ctx, c_ctx, ada_w, ada_b, norm_g, mla_w_dq, mla_g_q, mla_w_uq, mla_w_dkv, mla_g_kv,
           mla_w_ukv, mla_w_o, hy_w_in, hy_b_in, hy_w_short, hy_b_short, hy_f_w1, hy_f_b1, hy_f_w2,
           hy_f_b2, hy_f_w3, hy_f_freq, hy_bias, hy_w_out, hy_b_out, mlp_w1, mlp_w2):
    b, l, d = x.shape
    cl = ctx.shape[1]
    depth = ada_w.shape[0]
    assert b + 1 <= 8 and l % 256 == 0 and cl % 256 == 0

    cvec = jnp.concatenate([c, c_ctx[None, :], jnp.zeros((8 - b - 1, d), F32)], axis=0)
    mods = _ada(cvec, ada_w, ada_b)

    rope_lat = _rope_tables(l)
    rope_ctx = _no_rope_tables(cl)
    tabs_lat = _dft_tables(l)
    tabs_ctx = _small_dft_tables(cl)
    zero_bias = jnp.zeros((1, d), F32)
    tm = _row_tile(l)

    xc = ctx
    for i in range(depth):
        last = i == depth - 1
        j = i // 2
        g = norm_g[i].reshape(4, 1, d)
        m_lat = mods[i, :b].reshape(b, 6, 1, d)
        m_ctx = jnp.broadcast_to(mods[i, b].reshape(1, 6, 1, d), (b, 6, 1, d))
        lat = [m_lat[:, k] for k in range(6)]
        cx = [m_ctx[:, k] for k in range(6)]

        if i % 2 == 0:
            w = _mla_weights(mla_w_dq[j], mla_g_q[j], mla_w_uq[j], mla_w_dkv[j], mla_g_kv[j], mla_w_ukv[j])
            wo = mla_w_o[j].astype(BF16)
            qc, kc, vtc = _qkv(xc, cx[0], cx[1], g[0], w, *rope_ctx, cl)
            ql, kl, vtl = _qkv(x, lat[0], lat[1], g[0], w, *rope_lat, ATTN_TK)
            o_lat = _attention(ql, kc, vtc, kl, vtl, tq=min(ATTN_TQ, l))
            x = _proj_post(o_lat, wo, zero_bias, x, lat[2], g[1], tm)
            if not last:
                o_ctx = _attention(qc, kc, vtc, tq=cl)
                xc = _proj_post(o_ctx, wo, zero_bias, xc, cx[2], g[1], cl)
        else:
            p = {
                "w_in": hy_w_in[j].astype(BF16), "b_in": hy_b_in[j].reshape(1, -1),
                "w_short": hy_w_short[j], "b_short": hy_b_short[j].reshape(1, -1),
                "filter": (hy_f_w1[j], hy_f_b1[j], hy_f_w2[j], hy_f_b2[j], hy_f_w3[j], hy_f_freq[j]),
                "bias": hy_bias[j],
            }
            wo = hy_w_out[j].astype(BF16)
            bo = hy_b_out[j].reshape(1, d)
            z_lat = _hyena_mixer_lat(x, lat[0], lat[1], g[0], p, tabs_lat)
            x = _proj_post(z_lat, wo, bo, x, lat[2], g[1], tm)
            if not last:
                z_ctx = _hyena_mixer_ctx(xc, cx[0], cx[1], g[0], p, tabs_ctx)
                xc = _proj_post(z_ctx, wo, bo, xc, cx[2], g[1], cl)

        w1 = mlp_w1[i].astype(BF16)
        w2 = mlp_w2[i].astype(BF16)
        x = _mlp(x, lat[3], lat[4], lat[5], g[2], g[3], w1, w2, tm)
        if not last:
            xc = _mlp(xc, cx[3], cx[4], cx[5], g[2], g[3], w1, w2, cl)
    return x
```

```python
import functools
import math

import numpy as np
import jax
import jax.numpy as jnp
from jax import lax
from jax.experimental import pallas as pl
from jax.experimental.pallas import tpu as pltpu

F32 = jnp.float32
BF16 = jnp.bfloat16

GRID_W = 64
MLA_HEADS = 16
QK_NOPE = 64
QK_ROPE = 32
V_DIM = 64
Q_RANK = 256
KV_RANK = 128
ROPE_BASE = 10000.0
HY_ORDER = 2
HY_EMB = 33
HY_BANDS = (HY_EMB - 1) // 2
HY_FILTER_HIDDEN = 64
HY_TARGET = 1e-2
HY_FAST_PCT = 0.3
HY_SLOW_PCT = 1.5
EPS = 1e-6

LANES = 128
HEAD_PAD = LANES
VT_ROWS = 80
ONES_ROW = V_DIM
VMEM_LIMIT = 56 * 1024 * 1024
DFT_N2 = 128
ATTN_TQ = 512
ATTN_TK = 256
ATTN_UNROLL = 10

LOG2E = 1.4426950408889634


def _cparams(*sem):
    return pltpu.CompilerParams(dimension_semantics=sem, vmem_limit_bytes=VMEM_LIMIT)


def _dot(a, b):
    return jnp.dot(a, b, preferred_element_type=F32)


def _dot3(a, b):
    ah = a.astype(BF16)
    al = (a - ah.astype(F32)).astype(BF16)
    bh = b.astype(BF16)
    bl = (b - bh.astype(F32)).astype(BF16)
    return _dot(ah, bh) + _dot(al, bh) + _dot(ah, bl)


def _rms(x, g):
    return x * lax.rsqrt(jnp.mean(x * x, axis=-1, keepdims=True) + EPS) * g


def _ada_kernel(c_ref, w_ref, b_ref, o_ref):
    c = c_ref[...]
    s = c / (1.0 + jnp.exp(-c))
    o_ref[0] = _dot3(s, w_ref[0]) + b_ref[0]


def _ada(cvec, ada_w, ada_b):
    depth, d, n6 = ada_w.shape
    tn = 1536
    return pl.pallas_call(
        _ada_kernel,
        grid=(depth, n6 // tn),
        in_specs=[
            pl.BlockSpec((8, d), lambda i, j: (0, 0)),
            pl.BlockSpec((1, d, tn), lambda i, j: (i, 0, j)),
            pl.BlockSpec((1, 1, tn), lambda i, j: (i, 0, j)),
        ],
        out_specs=pl.BlockSpec((1, 8, tn), lambda i, j: (i, 0, j)),
        out_shape=jax.ShapeDtypeStruct((depth, 8, n6), F32),
        compiler_params=_cparams("parallel", "parallel"),
        name="ada",
    )(cvec, ada_w, ada_b.reshape(depth, 1, n6))


def _qkv_kernel(x_ref, sh_ref, sc_ref, g_ref, wd_ref, gq_ref, gkv_ref, wqa_ref, wqb_ref, wk_ref,
                wvt_ref, cos_ref, sin_ref, cost_ref, sint_ref, q_ref, k_ref, vt_ref, *, qscale):
    h = _rms(x_ref[0], g_ref[...]) * (1.0 + sc_ref[0]) + sh_ref[0]
    t = _dot(h.astype(BF16), wd_ref[...])
    cq = _rms(t[:, :Q_RANK], gq_ref[...]).astype(BF16)
    ckv = _rms(t[:, Q_RANK:Q_RANK + KV_RANK], gkv_ref[...]).astype(BF16)
    cos = cos_ref[...]
    sin = sin_ref[...]
    o = Q_RANK + KV_RANK
    kr = t[:, o:o + LANES] * cos + t[:, o + LANES:o + 2 * LANES] * sin
    nt = (((1,), (1,)), ((), ()))
    qa = lax.dot_general(wqa_ref[...], cq, nt, preferred_element_type=F32)
    qb = lax.dot_general(wqb_ref[...], cq, nt, preferred_element_type=F32)
    kn = _dot(ckv, wk_ref[...])
    cost = cost_ref[...]
    sint = sint_ref[...]
    for hd in range(MLA_HEADS):
        sl = slice(hd * HEAD_PAD, (hd + 1) * HEAD_PAD)
        q_ref[0, sl, :] = ((qa[sl] * cost + qb[sl] * sint) * qscale).astype(BF16)
        k_ref[0, hd] = (kn[:, sl] + kr).astype(BF16)
    vt = lax.dot_general(wvt_ref[...], ckv, nt, preferred_element_type=F32)
    tm = vt.shape[1]
    vt = vt.reshape(MLA_HEADS, VT_ROWS, tm)
    ones = lax.broadcasted_iota(jnp.int32, vt.shape, 1) == ONES_ROW
    vt_ref[0, 0] = jnp.where(ones, 1.0, vt).astype(BF16)


def _qkv(x, sh, sc, g, w, cos_t, sin_t, tm):
    b, l, d = x.shape
    nt = l // tm
    hw = MLA_HEADS * HEAD_PAD
    qscale = LOG2E / math.sqrt(QK_NOPE + QK_ROPE)
    full = lambda a: pl.BlockSpec(a.shape, lambda bi, i: (0,) * a.ndim)
    return pl.pallas_call(
        functools.partial(_qkv_kernel, qscale=qscale),
        grid=(b, nt),
        in_specs=[
            pl.BlockSpec((1, tm, d), lambda bi, i: (bi, i, 0)),
            pl.BlockSpec((1, 1, d), lambda bi, i: (bi, 0, 0)),
            pl.BlockSpec((1, 1, d), lambda bi, i: (bi, 0, 0)),
            full(g), full(w["wd"]), full(w["gq"]), full(w["gkv"]), full(w["wqa"]), full(w["wqb"]),
            full(w["wk"]), full(w["wvt"]),
            pl.BlockSpec((tm, LANES), lambda bi, i: (i, 0)),
            pl.BlockSpec((tm, LANES), lambda bi, i: (i, 0)),
            pl.BlockSpec((LANES, tm), lambda bi, i: (0, i)),
            pl.BlockSpec((LANES, tm), lambda bi, i: (0, i)),
        ],
        out_specs=[
            pl.BlockSpec((1, hw, tm), lambda bi, i: (bi, 0, i)),
            pl.BlockSpec((1, MLA_HEADS, tm, HEAD_PAD), lambda bi, i: (bi, 0, i, 0)),
            pl.BlockSpec((1, 1, MLA_HEADS, VT_ROWS, tm), lambda bi, i: (bi, i, 0, 0, 0)),
        ],
        out_shape=[
            jax.ShapeDtypeStruct((b, hw, l), BF16),
            jax.ShapeDtypeStruct((b, MLA_HEADS, l, HEAD_PAD), BF16),
            jax.ShapeDtypeStruct((b, nt, MLA_HEADS, VT_ROWS, tm), BF16),
        ],
        compiler_params=_cparams("parallel", "parallel"),
        name="mla_qkv",
    )(x, sh, sc, g, w["wd"], w["gq"], w["gkv"], w["wqa"], w["wqb"], w["wk"], w["wvt"], cos_t, sin_t, cos_t.T, sin_t.T)


def _mla_weights(w_dq, g_q, w_uq, w_dkv, g_kv, w_ukv):
    d = w_dq.shape[0]
    hq = QK_NOPE + QK_ROPE
    half = QK_ROPE // 2
    w_rope = w_dkv[:, KV_RANK:]
    w_rope_sw = jnp.concatenate([w_rope[:, half:], w_rope[:, :half]], axis=1)
    zl = jnp.zeros((d, QK_NOPE), F32)
    zr = jnp.zeros((d, HEAD_PAD - hq), F32)
    wd = jnp.concatenate([w_dq, w_dkv[:, :KV_RANK], zl, w_rope, zr, zl, w_rope_sw, zr], axis=1)
    wq = w_uq.reshape(Q_RANK, MLA_HEADS, hq)
    zq = jnp.zeros((Q_RANK, MLA_HEADS, HEAD_PAD - hq), F32)
    wqa = jnp.concatenate([wq, zq], axis=2).reshape(Q_RANK, MLA_HEADS * HEAD_PAD)
    zn = jnp.zeros((Q_RANK, MLA_HEADS, QK_NOPE), F32)
    wqb = jnp.concatenate([zn, wq[:, :, QK_NOPE + half:], wq[:, :, QK_NOPE:QK_NOPE + half], zq],
                          axis=2).reshape(Q_RANK, MLA_HEADS * HEAD_PAD)
    wkv = w_ukv.reshape(KV_RANK, MLA_HEADS, QK_NOPE + V_DIM)
    zk = jnp.zeros((KV_RANK, MLA_HEADS, HEAD_PAD - QK_NOPE), F32)
    wk = jnp.concatenate([wkv[:, :, :QK_NOPE], zk], axis=2).reshape(KV_RANK, MLA_HEADS * HEAD_PAD)
    wv = jnp.transpose(wkv[:, :, QK_NOPE:], (1, 2, 0))
    wvt = jnp.concatenate([wv, jnp.zeros((MLA_HEADS, VT_ROWS - V_DIM, KV_RANK), F32)], axis=1)
    return {
        "wd": wd.astype(BF16), "gq": g_q.reshape(1, -1), "gkv": g_kv.reshape(1, -1),
        "wqa": wqa.T.astype(BF16), "wqb": wqb.T.astype(BF16), "wk": wk.astype(BF16),
        "wvt": wvt.reshape(MLA_HEADS * VT_ROWS, KV_RANK).astype(BF16),
    }


def _rope_tables(l):
    t = jnp.arange(l)
    row = (t // GRID_W).astype(F32)
    col = (t % GRID_W).astype(F32)
    n_freq = QK_ROPE // 4
    inv = ROPE_BASE ** (-jnp.arange(n_freq, dtype=F32) / n_freq)
    ang = jnp.concatenate([row[:, None] * inv, col[:, None] * inv], axis=-1)
    cos, sin = jnp.cos(ang), jnp.sin(ang)
    ones = jnp.ones((l, QK_NOPE), F32)
    zl = jnp.zeros((l, QK_NOPE), F32)
    zr = jnp.zeros((l, HEAD_PAD - QK_NOPE - QK_ROPE), F32)
    cos_t = jnp.concatenate([ones, cos, cos, zr], axis=1)
    sin_t = jnp.concatenate([zl, -sin, sin, zr], axis=1)
    return cos_t, sin_t


def _no_rope_tables(l):
    keep = jnp.concatenate([jnp.ones((l, QK_NOPE + QK_ROPE), F32),
                            jnp.zeros((l, HEAD_PAD - QK_NOPE - QK_ROPE), F32)], axis=1)
    return keep, jnp.zeros((l, HEAD_PAD), F32)


def _attn_kernel(*refs, n_lat_chunks, tk, unroll):
    heads = (0, 1)
    sls = [slice(hd * HEAD_PAD, (hd + 1) * HEAD_PAD) for hd in heads]
    if n_lat_chunks:
        q_ref, kc_ref, vtc_ref, k_ref, vt_ref, o_ref = refs[:6]
        scr = refs[6:]
        s_scr = [scr[0:2], scr[2:4]]
        p_scr = [scr[4:6], scr[6:8]]
        acc_scr = scr[8:10]
    else:
        q_ref, kc_ref, vtc_ref, o_ref = refs

    def scores(k, hd):
        return _dot(k, q_ref[0, sls[hd], :])

    init = []
    for hd in heads:
        s = scores(kc_ref[0, hd], hd)
        m = jnp.max(s, axis=0, keepdims=True)
        p = jnp.exp2(s - m).astype(BF16)
        init.append((m, _dot(vtc_ref[0, 0, hd], p)))

    if not n_lat_chunks:
        outs = [acc[:V_DIM] / acc[ONES_ROW:ONES_ROW + 1] for _, acc in init]
        o_ref[0] = jnp.concatenate(outs, axis=0).T.astype(BF16)
        return

    def stage_a(hd, slot, c):
        start = pl.multiple_of(c * tk, tk)
        s = scores(k_ref[0, hd, pl.ds(start, tk), :], hd)
        s_scr[hd][slot][...] = s
        return jnp.max(s.reshape(tk // 8, 8, s.shape[1]), axis=0)

    def stage_b(hd, slot, cm, m_old):
        m_new = jnp.maximum(m_old, jnp.max(cm, axis=0, keepdims=True))
        p_scr[hd][slot][...] = jnp.exp2(s_scr[hd][slot][...] - m_new).astype(BF16)
        return m_new, jnp.exp2(m_old - m_new)

    def stage_c(hd, slot, c, alpha):
        acc_scr[hd][...] = acc_scr[hd][...] * alpha + _dot(vt_ref[0, c, hd], p_scr[hd][slot][...])

    carry = []
    for hd in heads:
        m, acc = init[hd]
        acc_scr[hd][...] = acc
        cm0 = stage_a(hd, 0, 0)
        cm1 = stage_a(hd, 1, 1)
        m, al = stage_b(hd, 0, cm0, m)
        carry += [m, al, cm1]

    def steps(carry, c0, count):
        carry = list(carry)
        for u in range(count):
            for hd in heads:
                m, al, cm = carry[3 * hd:3 * hd + 3]
                stage_c(hd, u % 2, c0 + u, al)
                m, al = stage_b(hd, (u + 1) % 2, cm, m)
                cm = stage_a(hd, u % 2, c0 + u + 2)
                carry[3 * hd:3 * hd + 3] = [m, al, cm]
        return tuple(carry)

    peel = (n_lat_chunks - 2) % unroll
    carry = steps(carry, 0, peel)
    carry = lax.fori_loop(0, (n_lat_chunks - 2) // unroll,
                          lambda j, cr: steps(cr, peel + unroll * j, unroll), carry)
    outs = []
    for hd in heads:
        m, al, cm = carry[3 * hd:3 * hd + 3]
        stage_c(hd, 0, n_lat_chunks - 2, al)
        m, al = stage_b(hd, 1, cm, m)
        stage_c(hd, 1, n_lat_chunks - 1, al)
        acc = acc_scr[hd][...]
        outs.append(acc[:V_DIM] / acc[ONES_ROW:ONES_ROW + 1])
    o_ref[0] = jnp.concatenate(outs, axis=0).T.astype(BF16)


def _attention(q, kc, vtc, k=None, vt=None, *, tq):
    b, hw, lq = q.shape
    cl = kc.shape[2]
    nq = lq // tq
    hp = 2 * HEAD_PAD
    in_specs = [
        pl.BlockSpec((1, hp, tq), lambda bi, h, i: (bi, h, i)),
        pl.BlockSpec((1, 2, cl, HEAD_PAD), lambda bi, h, i: (bi, h, 0, 0)),
        pl.BlockSpec((1, 1, 2, VT_ROWS, cl), lambda bi, h, i: (bi, 0, h, 0, 0)),
    ]
    args = [q, kc, vtc]
    n_chunks, tk, unroll = 0, 0, 2
    scratch = []
    if k is not None:
        lk = k.shape[2]
        n_chunks, tk = vt.shape[1], vt.shape[4]
        assert n_chunks >= 2 and n_chunks % 2 == 0
        unroll = min(ATTN_UNROLL, n_chunks - 2) if n_chunks > 2 else 2
        in_specs += [
            pl.BlockSpec((1, 2, lk, HEAD_PAD), lambda bi, h, i: (bi, h, 0, 0)),
            pl.BlockSpec((1, n_chunks, 2, VT_ROWS, tk), lambda bi, h, i: (bi, 0, h, 0, 0)),
        ]
        args += [k, vt]
        scratch = ([pltpu.VMEM((tk, tq), F32)] * 4 + [pltpu.VMEM((tk, tq), BF16)] * 4
                   + [pltpu.VMEM((VT_ROWS, tq), F32)] * 2)
    return pl.pallas_call(
        functools.partial(_attn_kernel, n_lat_chunks=n_chunks, tk=tk, unroll=unroll),
        grid=(b, MLA_HEADS // 2, nq),
        in_specs=in_specs,
        out_specs=pl.BlockSpec((1, tq, 2 * V_DIM), lambda bi, h, i: (bi, i, h)),
        out_shape=jax.ShapeDtypeStruct((b, lq, MLA_HEADS * V_DIM), BF16),
        scratch_shapes=scratch,
        compiler_params=_cparams("parallel", "parallel", "arbitrary"),
        name="attention" if n_chunks else "attention_ctx",
    )(*args)


def _proj_post_kernel(a_ref, w_ref, b_ref, x_ref, gt_ref, g_ref, o_ref):
    y = _dot(a_ref[0], w_ref[...]) + b_ref[...]
    o_ref[0] = x_ref[0] + gt_ref[0] * _rms(y, g_ref[...])


def _proj_post(a, w, bias, x, gt, g, tm):
    b, l, d = x.shape
    din = a.shape[2]
    return pl.pallas_call(
        _proj_post_kernel,
        grid=(b, l // tm),
        in_specs=[
            pl.BlockSpec((1, tm, din), lambda bi, i: (bi, i, 0)),
            pl.BlockSpec((din, d), lambda bi, i: (0, 0)),
            pl.BlockSpec((1, d), lambda bi, i: (0, 0)),
            pl.BlockSpec((1, tm, d), lambda bi, i: (bi, i, 0)),
            pl.BlockSpec((1, 1, d), lambda bi, i: (bi, 0, 0)),
            pl.BlockSpec((1, d), lambda bi, i: (0, 0)),
        ],
        out_specs=pl.BlockSpec((1, tm, d), lambda bi, i: (bi, i, 0)),
        out_shape=jax.ShapeDtypeStruct((b, l, d), F32),
        compiler_params=_cparams("parallel", "parallel"),
        name="proj_post",
    )(a, w, bias, x, gt, g)


def _mlp_kernel(x_ref, sh_ref, sc_ref, gt_ref, g1_ref, g2_ref, w1_ref, w2_ref, o_ref, *, ff_chunk):
    x = x_ref[0]
    h = (_rms(x, g1_ref[...]) * (1.0 + sc_ref[0]) + sh_ref[0]).astype(BF16)
    dff = w1_ref.shape[1]
    m = None
    for c in range(dff // ff_chunk):
        u = _dot(h, w1_ref[:, c * ff_chunk:(c + 1) * ff_chunk])
        u = jnp.maximum(u, 0.0)
        part = _dot((u * u).astype(BF16), w2_ref[c * ff_chunk:(c + 1) * ff_chunk, :])
        m = part if m is None else m + part
    o_ref[0] = x + gt_ref[0] * _rms(m, g2_ref[...])


def _mlp(x, sh, sc, gt, g1, g2, w1, w2, tm):
    b, l, d = x.shape
    dff = w1.shape[1]
    vec = pl.BlockSpec((1, 1, d), lambda bi, i: (bi, 0, 0))
    row = pl.BlockSpec((1, d), lambda bi, i: (0, 0))
    return pl.pallas_call(
        functools.partial(_mlp_kernel, ff_chunk=1024),
        grid=(b, l // tm),
        in_specs=[
            pl.BlockSpec((1, tm, d), lambda bi, i: (bi, i, 0)),
            vec, vec, vec, row, row,
            pl.BlockSpec((d, dff), lambda bi, i: (0, 0)),
            pl.BlockSpec((dff, d), lambda bi, i: (0, 0)),
        ],
        out_specs=pl.BlockSpec((1, tm, d), lambda bi, i: (bi, i, 0)),
        out_shape=jax.ShapeDtypeStruct((b, l, d), F32),
        compiler_params=_cparams("parallel", "parallel"),
        name="mlp",
    )(x, sh, sc, gt, g1, g2, w1, w2)


HALO = 8


def _hy_in_kernel(x_ref, xp_ref, xn_ref, sh_ref, sc_ref, g_ref, w_ref, b_ref, ws_ref, bs_ref,
                  x1_ref, x2_ref, v_ref, u_scr):
    i = pl.program_id(1)
    n = pl.num_programs(1)
    tm = x_ref.shape[1]
    d = x_ref.shape[2]
    xa = jnp.concatenate([xp_ref[0], x_ref[0], xn_ref[0]], axis=0)
    h = (_rms(xa, g_ref[...]) * (1.0 + sc_ref[0]) + sh_ref[0]).astype(BF16)
    u_scr[...] = _dot(h, w_ref[...]) + b_ref[...]

    @pl.when(i == 0)
    def _():
        u_scr[0:HALO, :] = jnp.zeros((HALO, 3 * d), F32)

    @pl.when(i == n - 1)
    def _():
        u_scr[tm + HALO:tm + 2 * HALO, :] = jnp.zeros((HALO, 3 * d), F32)

    for j, o_ref in enumerate((x1_ref, x2_ref, v_ref)):
        sl = slice(j * d, (j + 1) * d)
        y = (u_scr[HALO - 1:HALO - 1 + tm, sl] * ws_ref[0:1, sl]
             + u_scr[HALO:HALO + tm, sl] * ws_ref[1:2, sl]
             + u_scr[HALO + 1:HALO + 1 + tm, sl] * ws_ref[2:3, sl]
             + bs_ref[:, sl])
        o_ref[0] = y.astype(BF16)


def _hy_in(x, sh, sc, g, w_in, b_in, w_short, b_short, tm):
    b, l, d = x.shape
    nt = l // tm
    tb = tm // HALO
    nb = l // HALO
    vec = pl.BlockSpec((1, 1, d), lambda bi, i: (bi, 0, 0))
    out = pl.BlockSpec((1, tm, d), lambda bi, i: (bi, i, 0))
    return pl.pallas_call(
        _hy_in_kernel,
        grid=(b, nt),
        in_specs=[
            pl.BlockSpec((1, tm, d), lambda bi, i: (bi, i, 0)),
            pl.BlockSpec((1, HALO, d), lambda bi, i: (bi, jnp.maximum(i * tb - 1, 0), 0)),
            pl.BlockSpec((1, HALO, d), lambda bi, i: (bi, jnp.minimum((i + 1) * tb, nb - 1), 0)),
            vec, vec,
            pl.BlockSpec((1, d), lambda bi, i: (0, 0)),
            pl.BlockSpec((d, 3 * d), lambda bi, i: (0, 0)),
            pl.BlockSpec((1, 3 * d), lambda bi, i: (0, 0)),
            pl.BlockSpec((3, 3 * d), lambda bi, i: (0, 0)),
            pl.BlockSpec((1, 3 * d), lambda bi, i: (0, 0)),
        ],
        out_specs=[out, out, out],
        out_shape=[jax.ShapeDtypeStruct((b, l, d), BF16)] * 3,
        scratch_shapes=[pltpu.VMEM((tm + 2 * HALO, 3 * d), F32)],
        compiler_params=_cparams("parallel", "parallel"),
        name="hyena_in",
    )(x, x, x, sh, sc, g, w_in, b_in, w_short, b_short)


def _filt_kernel(z_ref, w1_ref, b1_ref, w2_ref, b2_ref, w3_ref, fr_ref, dl_ref, o_ref, *, l):
    tl = z_ref.shape[0]
    d = dl_ref.shape[1]
    fr = fr_ref[...]
    hdn = jnp.sin(fr * (_dot3(z_ref[...], w1_ref[...]) + b1_ref[...]))
    hdn = jnp.sin(fr * (_dot3(hdn, w2_ref[...]) + b2_ref[...]))
    hf = _dot3(hdn, w3_ref[...])
    rows = pl.program_id(0) * tl + lax.broadcasted_iota(jnp.int32, (tl, d), 0)
    t = rows.astype(F32) * (1.0 / (l - 1))
    decay = jnp.exp(-t * jnp.abs(dl_ref[...]))
    for j in range(2 * HY_ORDER):
        f = hf[:, j * d:(j + 1) * d] * decay
        if j % 2 == 1:
            f = jnp.where(rows == 0, 0.0, f)
        o_ref[j] = f.astype(BF16)


def _filters(l, d, f_w1, f_b1, f_w2, f_b2, f_w3, f_freq):
    t = jnp.linspace(0.0, 1.0, l, dtype=F32)[:, None]
    w = (2.0 * math.pi) * jnp.arange(l, dtype=F32)[:, None] / l
    f = jnp.linspace(1e-4, HY_BANDS - 1, HY_BANDS, dtype=F32)[None, :]
    kz = HY_FILTER_HIDDEN
    z = jnp.concatenate([t, jnp.cos(f * w), -jnp.sin(f * w), jnp.zeros((l, kz - HY_EMB), F32)], axis=-1)
    w1 = jnp.concatenate([f_w1, jnp.zeros((kz - HY_EMB, HY_FILTER_HIDDEN), F32)], axis=0)
    min_decay = math.log(HY_TARGET) / HY_SLOW_PCT
    max_decay = math.log(HY_TARGET) / HY_FAST_PCT
    deltas = jnp.linspace(min_decay, max_decay, d, dtype=F32)[None, :]
    tl = 256
    full = lambda a: pl.BlockSpec(a.shape, lambda i: (0,) * a.ndim)
    ops = [w1, f_b1.reshape(1, -1), f_w2, f_b2.reshape(1, -1), f_w3, f_freq.reshape(1, -1), deltas]
    return pl.pallas_call(
        functools.partial(_filt_kernel, l=l),
        grid=(l // tl,),
        in_specs=[pl.BlockSpec((tl, kz), lambda i: (i, 0))] + [full(a) for a in ops],
        out_specs=pl.BlockSpec((2 * HY_ORDER, tl, d), lambda i: (0, i, 0)),
        out_shape=jax.ShapeDtypeStruct((2 * HY_ORDER, l, d), BF16),
        compiler_params=_cparams("parallel"),
        name="hyena_filters",
    )(z, *ops)


def _dft_tables(l):
    n = 2 * l
    n2 = DFT_N2
    n1 = n // n2
    n1h = n1 // 2
    slots = n1h + 8
    k1 = jnp.arange(slots)
    valid = (k1 <= n1h)[:, None]
    a = jnp.arange(n1h)
    ang1 = (2.0 * math.pi / n1) * ((k1[:, None] * a[None, :]) % n1).astype(F32)
    f1 = jnp.concatenate([jnp.where(valid, jnp.cos(ang1), 0.0), jnp.where(valid, -jnp.sin(ang1), 0.0)], axis=0)
    kk = jnp.arange(n1h)
    wgt = jnp.where(kk == 0, 1.0, 2.0)[None, :] / n
    ang2 = (2.0 * math.pi / n1) * ((a[:, None] * kk[None, :]) % n1).astype(F32)
    f2 = jnp.concatenate([wgt * jnp.cos(ang2), -wgt * jnp.sin(ang2)], axis=1)
    kg = jnp.arange(n1h + 1)
    k2 = jnp.arange(n2)
    j = jnp.arange(n2)
    prod = (j[None, None, :] * (kg[:, None, None] + n1 * k2[None, :, None])) % n
    th = (2.0 * math.pi / n) * prod.astype(F32)
    c, s = jnp.cos(th), jnp.sin(th)
    g = jnp.concatenate([jnp.concatenate([c, s], axis=2), jnp.concatenate([-s, c], axis=2)], axis=1)
    ct, st = jnp.swapaxes(c, 1, 2), jnp.swapaxes(s, 1, 2)
    gi = jnp.concatenate([jnp.concatenate([ct, -st], axis=2), jnp.concatenate([st, ct], axis=2)], axis=1)
    return {"n": n, "n1h": n1h, "slots": slots,
            "f1": f1.astype(BF16), "f2": f2.astype(BF16), "g": g.astype(BF16), "gi": gi.astype(BF16)}


def _stage1_kernel(f_ref, u_ref, a_ref):
    a_ref[0] = _dot(f_ref[...], u_ref[0]).astype(BF16)


def _stage1(u, tab):
    b, l, d = u.shape
    n1h, slots = tab["n1h"], tab["slots"]
    cols = DFT_N2 * d
    tc = 4096
    a = pl.pallas_call(
        _stage1_kernel,
        grid=(b, cols // tc),
        in_specs=[
            pl.BlockSpec((2 * slots, n1h), lambda bi, j: (0, 0)),
            pl.BlockSpec((1, n1h, tc), lambda bi, j: (bi, 0, j)),
        ],
        out_specs=pl.BlockSpec((1, 2 * slots, tc), lambda bi, j: (bi, 0, j)),
        out_shape=jax.ShapeDtypeStruct((b, 2 * slots, cols), BF16),
        compiler_params=_cparams("parallel", "parallel"),
        name="dft_stage1",
    )(tab["f1"], u.reshape(b, n1h, cols))
    return a.reshape(b, 2, slots, DFT_N2, d)


def _cplx_rows(ref):
    n2, d = ref.shape[-2], ref.shape[-1]
    return ref[...].reshape(2 * n2, d)


def _filt_spec_kernel(g_ref, af_ref, ab_ref, bias_ref, k_ref):
    n2 = af_ref.shape[-2]
    xf = _dot(g_ref[0], _cplx_rows(af_ref))
    xb = _dot(g_ref[0], _cplx_rows(ab_ref))
    k_ref[0, 0, 0] = (xf[:n2] + xb[:n2] + bias_ref[0]).astype(BF16)
    k_ref[0, 1, 0] = (xf[n2:] - xb[n2:]).astype(BF16)


def _filt_spectrum(a_filt, bias, tab):
    _, _, slots, n2, d = a_filt.shape
    n1h = tab["n1h"]
    blk = lambda seq: pl.BlockSpec((1, 2, 1, n2, d), lambda o, k: (2 * o + seq, 0, k, 0, 0))
    return pl.pallas_call(
        _filt_spec_kernel,
        grid=(HY_ORDER, slots),
        in_specs=[
            pl.BlockSpec((1, 2 * n2, 2 * n2), lambda o, k: (jnp.minimum(k, n1h), 0, 0)),
            blk(0), blk(1),
            pl.BlockSpec((1, 1, d), lambda o, k: (o, 0, 0)),
        ],
        out_specs=pl.BlockSpec((1, 2, 1, n2, d), lambda o, k: (o, 0, k, 0, 0)),
        out_shape=jax.ShapeDtypeStruct((HY_ORDER, 2, slots, n2, d), BF16),
        compiler_params=_cparams("parallel", "parallel"),
        name="hyena_filter_spectrum",
    )(tab["g"], a_filt, a_filt, bias.reshape(HY_ORDER, 1, d))


def _mid_kernel(g_ref, gi_ref, a_ref, k_ref, z_ref):
    n2 = a_ref.shape[-2]
    x = _dot(g_ref[0], _cplx_rows(a_ref))
    xr, xi = x[:n2], x[n2:]
    kr = k_ref[0, 0, 0].astype(F32)
    ki = k_ref[0, 1, 0].astype(F32)
    y = jnp.concatenate([xr * kr - xi * ki, xr * ki + xi * kr], axis=0).astype(BF16)
    z = _dot(gi_ref[0], y)
    z_ref[0, 0, 0] = z[:n2].astype(BF16)
    z_ref[0, 1, 0] = z[n2:].astype(BF16)


def _mid(a, kf, order, tab):
    b, _, slots, n2, d = a.shape
    n1h = tab["n1h"]
    gspec = pl.BlockSpec((1, 2 * n2, 2 * n2), lambda bi, k: (jnp.minimum(k, n1h), 0, 0))
    return pl.pallas_call(
        _mid_kernel,
        grid=(b, slots),
        in_specs=[
            gspec, gspec,
            pl.BlockSpec((1, 2, 1, n2, d), lambda bi, k: (bi, 0, k, 0, 0)),
            pl.BlockSpec((1, 2, 1, n2, d), lambda bi, k: (order, 0, k, 0, 0)),
        ],
        out_specs=pl.BlockSpec((1, 2, 1, n2, d), lambda bi, k: (bi, 0, k, 0, 0)),
        out_shape=jax.ShapeDtypeStruct((b, 2, slots, n2, d), BF16),
        compiler_params=_cparams("parallel", "parallel"),
        name="dft_mid",
    )(tab["g"], tab["gi"], a, kf)


def _stage2_kernel(f_ref, zr_ref, zi_ref, zt_ref, x_ref, o_ref, *, inv_n):
    z = jnp.concatenate([zr_ref[0, 0], zi_ref[0, 0]], axis=0)
    y = _dot(f_ref[...], z)
    rows = lax.broadcasted_iota(jnp.int32, y.shape, 0)
    nyq = zt_ref[0, 0, 0:1, :].astype(F32) * inv_n
    y = y + jnp.where(rows % 2 == 0, nyq, -nyq)
    o_ref[0] = (y * x_ref[0].astype(F32)).astype(BF16)


def _stage2(z, gate, tab):
    b, _, slots, n2, d = z.shape
    n1h = tab["n1h"]
    cols = n2 * d
    tc = 4096
    zf = z.reshape(b, 2, slots, cols)
    l = n1h * n2
    out = pl.pallas_call(
        functools.partial(_stage2_kernel, inv_n=1.0 / tab["n"]),
        grid=(b, cols // tc),
        in_specs=[
            pl.BlockSpec((n1h, 2 * n1h), lambda bi, j: (0, 0)),
            pl.BlockSpec((1, 1, n1h, tc), lambda bi, j: (bi, 0, 0, j)),
            pl.BlockSpec((1, 1, n1h, tc), lambda bi, j: (bi, 1, 0, j)),
            pl.BlockSpec((1, 1, 8, tc), lambda bi, j: (bi, 0, n1h // 8, j)),
            pl.BlockSpec((1, n1h, tc), lambda bi, j: (bi, 0, j)),
        ],
        out_specs=pl.BlockSpec((1, n1h, tc), lambda bi, j: (bi, 0, j)),
        out_shape=jax.ShapeDtypeStruct((b, n1h, cols), BF16),
        compiler_params=_cparams("parallel", "parallel"),
        name="dft_stage2",
    )(tab["f2"], zf, zf, zf, gate.reshape(b, n1h, cols))
    return out.reshape(b, l, d)


def _small_dft_tables(l):
    n = 2 * l
    k = jnp.arange(l)
    t = jnp.arange(l)
    ang = (2.0 * math.pi / n) * ((k[:, None] * t[None, :]) % n).astype(F32)
    sign = jnp.where(t % 2 == 0, 1.0, -1.0)[None, :]
    fwd = jnp.concatenate([jnp.cos(ang), -jnp.sin(ang), sign, jnp.zeros((7, l), F32)], axis=0)
    wgt = jnp.where(k == 0, 1.0, 2.0)[None, :] / n
    ang_t = ang.T
    inv = jnp.concatenate([wgt * jnp.cos(ang_t), -wgt * jnp.sin(ang_t)], axis=1)
    return {"n": n, "fwd": fwd.astype(BF16), "inv": inv.astype(BF16)}


def _small_spec_kernel(f_ref, hf_ref, hb_ref, bias_ref, k_ref):
    l = hf_ref.shape[1]
    xf = _dot(f_ref[...], hf_ref[0])
    xb = _dot(f_ref[...], hb_ref[0])
    bias = bias_ref[0]
    k_ref[0, 0:l, :] = (xf[:l] + xb[:l] + bias).astype(BF16)
    k_ref[0, l:2 * l, :] = (xf[l:2 * l] - xb[l:2 * l]).astype(BF16)
    k_ref[0, 2 * l:, :] = (xf[2 * l:] + xb[2 * l:] + bias).astype(BF16)


def _small_filt_spectrum(filt, bias, tab):
    _, l, d = filt.shape
    rows = 2 * l + 8
    tc = 512
    return pl.pallas_call(
        _small_spec_kernel,
        grid=(HY_ORDER, d // tc),
        in_specs=[
            pl.BlockSpec((rows, l), lambda o, j: (0, 0)),
            pl.BlockSpec((1, l, tc), lambda o, j: (2 * o, 0, j)),
            pl.BlockSpec((1, l, tc), lambda o, j: (2 * o + 1, 0, j)),
            pl.BlockSpec((1, 1, tc), lambda o, j: (o, 0, j)),
        ],
        out_specs=pl.BlockSpec((1, rows, tc), lambda o, j: (o, 0, j)),
        out_shape=jax.ShapeDtypeStruct((HY_ORDER, rows, d), BF16),
        compiler_params=_cparams("parallel", "parallel"),
        name="hyena_filter_spectrum_ctx",
    )(tab["fwd"], filt, filt, bias.reshape(HY_ORDER, 1, d))


def _small_conv_kernel(f_ref, fi_ref, u_ref, k_ref, x_ref, o_ref, *, inv_n):
    l = u_ref.shape[1]
    x = _dot(f_ref[...], u_ref[0])
    xr, xi = x[:l], x[l:2 * l]
    kr = k_ref[0, 0:l, :].astype(F32)
    ki = k_ref[0, l:2 * l, :].astype(F32)
    y = jnp.concatenate([xr * kr - xi * ki, xr * ki + xi * kr], axis=0).astype(BF16)
    out = _dot(fi_ref[...], y)
    nyq = x[2 * l:2 * l + 1] * k_ref[0, 2 * l:2 * l + 1, :].astype(F32) * inv_n
    rows = lax.broadcasted_iota(jnp.int32, out.shape, 0)
    out = out + jnp.where(rows % 2 == 0, nyq, -nyq)
    o_ref[0] = (out * x_ref[0].astype(F32)).astype(BF16)


def _small_conv(u, kf, order, gate, tab):
    b, l, d = u.shape
    rows = 2 * l + 8
    tc = 512
    return pl.pallas_call(
        functools.partial(_small_conv_kernel, inv_n=1.0 / tab["n"]),
        grid=(b, d // tc),
        in_specs=[
            pl.BlockSpec((rows, l), lambda bi, j: (0, 0)),
            pl.BlockSpec((l, 2 * l), lambda bi, j: (0, 0)),
            pl.BlockSpec((1, l, tc), lambda bi, j: (bi, 0, j)),
            pl.BlockSpec((1, rows, tc), lambda bi, j: (order, 0, j)),
            pl.BlockSpec((1, l, tc), lambda bi, j: (bi, 0, j)),
        ],
        out_specs=pl.BlockSpec((1, l, tc), lambda bi, j: (bi, 0, j)),
        out_shape=jax.ShapeDtypeStruct((b, l, d), BF16),
        compiler_params=_cparams("parallel", "parallel"),
        name="long_conv_ctx",
    )(tab["fwd"], tab["inv"], u, kf, gate)


def _row_tile(l):
    return 512 if l % 512 == 0 else 256


def _hyena_mixer_lat(x, sh, sc, g, p, tabs):
    b, l, d = x.shape
    x1, x2, v = _hy_in(x, sh, sc, g, p["w_in"], p["b_in"], p["w_short"], p["b_short"], _row_tile(l))
    filt = _filters(l, d, *p["filter"])
    kf = _filt_spectrum(_stage1(filt, tabs), p["bias"], tabs)
    z = _stage2(_mid(_stage1(v, tabs), kf, 0, tabs), x1, tabs)
    z = _stage2(_mid(_stage1(z, tabs), kf, 1, tabs), x2, tabs)
    return z


def _hyena_mixer_ctx(x, sh, sc, g, p, tabs):
    b, l, d = x.shape
    x1, x2, v = _hy_in(x, sh, sc, g, p["w_in"], p["b_in"], p["w_short"], p["b_short"], l)
    filt = _filters(l, d, *p["filter"])
    kf = _small_filt_spectrum(filt, p["bias"], tabs)
    z = _small_conv(v, kf, 0, x1, tabs)
    z = _small_conv(z, kf, 1, x2, tabs)
    return z


def kernel(x, c, ctx, c_ctx, ada_w, ada_b, norm_g, mla_w_dq, mla_g_q, mla_w_uq, mla_w_dkv, mla_g_kv,
           mla_w_ukv, mla_w_o, hy_w_in, hy_b_in, hy_w_short, hy_b_short, hy_f_w1, hy_f_b1, hy_f_w2,
           hy_f_b2, hy_f_w3, hy_f_freq, hy_bias, hy_w_out, hy_b_out, mlp_w1, mlp_w2):
    b, l, d = x.shape
    cl = ctx.shape[1]
    depth = ada_w.shape[0]
    assert b + 1 <= 8 and l % 256 == 0 and cl % 256 == 0

    cvec = jnp.concatenate([c, c_ctx[None, :], jnp.zeros((8 - b - 1, d), F32)], axis=0)
    mods = _ada(cvec, ada_w, ada_b)

    rope_lat = _rope_tables(l)
    rope_ctx = _no_rope_tables(cl)
    tabs_lat = _dft_tables(l)
    tabs_ctx = _small_dft_tables(cl)
    zero_bias = jnp.zeros((1, d), F32)
    tm = _row_tile(l)

    xc = ctx
    for i in range(depth):
        last = i == depth - 1
        j = i // 2
        g = norm_g[i].reshape(4, 1, d)
        m_lat = mods[i, :b].reshape(b, 6, 1, d)
        m_ctx = jnp.broadcast_to(mods[i, b].reshape(1, 6, 1, d), (b, 6, 1, d))
        lat = [m_lat[:, k] for k in range(6)]
        cx = [m_ctx[:, k] for k in range(6)]

        if i % 2 == 0:
            w = _mla_weights(mla_w_dq[j], mla_g_q[j], mla_w_uq[j], mla_w_dkv[j], mla_g_kv[j], mla_w_ukv[j])
            wo = mla_w_o[j].astype(BF16)
            qc, kc, vtc = _qkv(xc, cx[0], cx[1], g[0], w, *rope_ctx, cl)
            ql, kl, vtl = _qkv(x, lat[0], lat[1], g[0], w, *rope_lat, ATTN_TK)
            o_lat = _attention(ql, kc, vtc, kl, vtl, tq=min(ATTN_TQ, l))
            x = _proj_post(o_lat, wo, zero_bias, x, lat[2], g[1], tm)
            if not last:
                o_ctx = _attention(qc, kc, vtc, tq=cl)
                xc = _proj_post(o_ctx, wo, zero_bias, xc, cx[2], g[1], cl)
        else:
            p = {
                "w_in": hy_w_in[j].astype(BF16), "b_in": hy_b_in[j].reshape(1, -1),
                "w_short": hy_w_short[j], "b_short": hy_b_short[j].reshape(1, -1),
                "filter": (hy_f_w1[j], hy_f_b1[j], hy_f_w2[j], hy_f_b2[j], hy_f_w3[j], hy_f_freq[j]),
                "bias": hy_bias[j],
            }
            wo = hy_w_out[j].astype(BF16)
            bo = hy_b_out[j].reshape(1, d)
            z_lat = _hyena_mixer_lat(x, lat[0], lat[1], g[0], p, tabs_lat)
            x = _proj_post(z_lat, wo, bo, x, lat[2], g[1], tm)
            if not last:
                z_ctx = _hyena_mixer_ctx(xc, cx[0], cx[1], g[0], p, tabs_ctx)
                xc = _proj_post(z_ctx, wo, bo, xc, cx[2], g[1], cl)

        w1 = mlp_w1[i].astype(BF16)
        w2 = mlp_w2[i].astype(BF16)
        x = _mlp(x, lat[3], lat[4], lat[5], g[2], g[3], w1, w2, tm)
        if not last:
            xc = _mlp(xc, cx[3], cx[4], cx[5], g[2], g[3], w1, w2, cl)
    return x
```

```python
import functools
import math

import numpy as np
import jax
import jax.numpy as jnp
from jax import lax
from jax.experimental import pallas as pl
from jax.experimental.pallas import tpu as pltpu

F32 = jnp.float32
BF16 = jnp.bfloat16

GRID_W = 64
MLA_HEADS = 16
QK_NOPE = 64
QK_ROPE = 32
V_DIM = 64
Q_RANK = 256
KV_RANK = 128
ROPE_BASE = 10000.0
HY_ORDER = 2
HY_EMB = 33
HY_BANDS = (HY_EMB - 1) // 2
HY_FILTER_HIDDEN = 64
HY_TARGET = 1e-2
HY_FAST_PCT = 0.3
HY_SLOW_PCT = 1.5
EPS = 1e-6

LANES = 128
HEAD_PAD = LANES
VT_ROWS = 80
ONES_ROW = V_DIM
VMEM_LIMIT = 56 * 1024 * 1024
DFT_N2 = 128
ATTN_TQ = 512
ATTN_TK = 256
ATTN_UNROLL = 10

LOG2E = 1.4426950408889634


def _cparams(*sem):
    return pltpu.CompilerParams(dimension_semantics=sem, vmem_limit_bytes=VMEM_LIMIT)


def _dot(a, b):
    return jnp.dot(a, b, preferred_element_type=F32)


def _dot3(a, b):
    ah = a.astype(BF16)
    al = (a - ah.astype(F32)).astype(BF16)
    bh = b.astype(BF16)
    bl = (b - bh.astype(F32)).astype(BF16)
    return _dot(ah, bh) + _dot(al, bh) + _dot(ah, bl)


def _rms(x, g):
    return x * lax.rsqrt(jnp.mean(x * x, axis=-1, keepdims=True) + EPS) * g


def _ada_kernel(c_ref, w_ref, b_ref, o_ref):
    c = c_ref[...]
    s = c / (1.0 + jnp.exp(-c))
    o_ref[0] = _dot3(s, w_ref[0]) + b_ref[0]


def _ada(cvec, ada_w, ada_b):
    depth, d, n6 = ada_w.shape
    tn = 1536
    return pl.pallas_call(
        _ada_kernel,
        grid=(depth, n6 // tn),
        in_specs=[
            pl.BlockSpec((8, d), lambda i, j: (0, 0)),
            pl.BlockSpec((1, d, tn), lambda i, j: (i, 0, j)),
            pl.BlockSpec((1, 1, tn), lambda i, j: (i, 0, j)),
        ],
        out_specs=pl.BlockSpec((1, 8, tn), lambda i, j: (i, 0, j)),
        out_shape=jax.ShapeDtypeStruct((depth, 8, n6), F32),
        compiler_params=_cparams("parallel", "parallel"),
        name="ada",
    )(cvec, ada_w, ada_b.reshape(depth, 1, n6))


def _qkv_kernel(x_ref, sh_ref, sc_ref, g_ref, wd_ref, gq_ref, gkv_ref, wqa_ref, wqb_ref, wk_ref,
                wvt_ref, cos_ref, sin_ref, cost_ref, sint_ref, q_ref, k_ref, vt_ref, *, qscale):
    h = _rms(x_ref[0], g_ref[...]) * (1.0 + sc_ref[0]) + sh_ref[0]
    t = _dot(h.astype(BF16), wd_ref[...])
    cq = _rms(t[:, :Q_RANK], gq_ref[...]).astype(BF16)
    ckv = _rms(t[:, Q_RANK:Q_RANK + KV_RANK], gkv_ref[...]).astype(BF16)
    cos = cos_ref[...]
    sin = sin_ref[...]
    o = Q_RANK + KV_RANK
    kr = t[:, o:o + LANES] * cos + t[:, o + LANES:o + 2 * LANES] * sin
    nt = (((1,), (1,)), ((), ()))
    qa = lax.dot_general(wqa_ref[...], cq, nt, preferred_element_type=F32)
    qb = lax.dot_general(wqb_ref[...], cq, nt, preferred_element_type=F32)
    kn = _dot(ckv, wk_ref[...])
    cost = cost_ref[...]
    sint = sint_ref[...]
    for hd in range(MLA_HEADS):
        sl = slice(hd * HEAD_PAD, (hd + 1) * HEAD_PAD)
        q_ref[0, sl, :] = ((qa[sl] * cost + qb[sl] * sint) * qscale).astype(BF16)
        k_ref[0, hd] = (kn[:, sl] + kr).astype(BF16)
    vt = lax.dot_general(wvt_ref[...], ckv, nt, preferred_element_type=F32)
    tm = vt.shape[1]
    vt = vt.reshape(MLA_HEADS, VT_ROWS, tm)
    ones = lax.broadcasted_iota(jnp.int32, vt.shape, 1) == ONES_ROW
    vt_ref[0, 0] = jnp.where(ones, 1.0, vt).astype(BF16)


def _qkv(x, sh, sc, g, w, cos_t, sin_t, tm):
    b, l, d = x.shape
    nt = l // tm
    hw = MLA_HEADS * HEAD_PAD
    qscale = LOG2E / math.sqrt(QK_NOPE + QK_ROPE)
    full = lambda a: pl.BlockSpec(a.shape, lambda bi, i: (0,) * a.ndim)
    return pl.pallas_call(
        functools.partial(_qkv_kernel, qscale=qscale),
        grid=(b, nt),
        in_specs=[
            pl.BlockSpec((1, tm, d), lambda bi, i: (bi, i, 0)),
            pl.BlockSpec((1, 1, d), lambda bi, i: (bi, 0, 0)),
            pl.BlockSpec((1, 1, d), lambda bi, i: (bi, 0, 0)),
            full(g), full(w["wd"]), full(w["gq"]), full(w["gkv"]), full(w["wqa"]), full(w["wqb"]),
            full(w["wk"]), full(w["wvt"]),
            pl.BlockSpec((tm, LANES), lambda bi, i: (i, 0)),
            pl.BlockSpec((tm, LANES), lambda bi, i: (i, 0)),
            pl.BlockSpec((LANES, tm), lambda bi, i: (0, i)),
            pl.BlockSpec((LANES, tm), lambda bi, i: (0, i)),
        ],
        out_specs=[
            pl.BlockSpec((1, hw, tm), lambda bi, i: (bi, 0, i)),
            pl.BlockSpec((1, MLA_HEADS, tm, HEAD_PAD), lambda bi, i: (bi, 0, i, 0)),
            pl.BlockSpec((1, 1, MLA_HEADS, VT_ROWS, tm), lambda bi, i: (bi, i, 0, 0, 0)),
        ],
        out_shape=[
            jax.ShapeDtypeStruct((b, hw, l), BF16),
            jax.ShapeDtypeStruct((b, MLA_HEADS, l, HEAD_PAD), BF16),
            jax.ShapeDtypeStruct((b, nt, MLA_HEADS, VT_ROWS, tm), BF16),
        ],
        compiler_params=_cparams("parallel", "parallel"),
        name="mla_qkv",
    )(x, sh, sc, g, w["wd"], w["gq"], w["gkv"], w["wqa"], w["wqb"], w["wk"], w["wvt"], cos_t, sin_t, cos_t.T, sin_t.T)


def _mla_weights(w_dq, g_q, w_uq, w_dkv, g_kv, w_ukv):
    d = w_dq.shape[0]
    hq = QK_NOPE + QK_ROPE
    half = QK_ROPE // 2
    w_rope = w_dkv[:, KV_RANK:]
    w_rope_sw = jnp.concatenate([w_rope[:, half:], w_rope[:, :half]], axis=1)
    zl = jnp.zeros((d, QK_NOPE), F32)
    zr = jnp.zeros((d, HEAD_PAD - hq), F32)
    wd = jnp.concatenate([w_dq, w_dkv[:, :KV_RANK], zl, w_rope, zr, zl, w_rope_sw, zr], axis=1)
    wq = w_uq.reshape(Q_RANK, MLA_HEADS, hq)
    zq = jnp.zeros((Q_RANK, MLA_HEADS, HEAD_PAD - hq), F32)
    wqa = jnp.concatenate([wq, zq], axis=2).reshape(Q_RANK, MLA_HEADS * HEAD_PAD)
    zn = jnp.zeros((Q_RANK, MLA_HEADS, QK_NOPE), F32)
    wqb = jnp.concatenate([zn, wq[:, :, QK_NOPE + half:], wq[:, :, QK_NOPE:QK_NOPE + half], zq],
                          axis=2).reshape(Q_RANK, MLA_HEADS * HEAD_PAD)
    wkv = w_ukv.reshape(KV_RANK, MLA_HEADS, QK_NOPE + V_DIM)
    zk = jnp.zeros((KV_RANK, MLA_HEADS, HEAD_PAD - QK_NOPE), F32)
    wk = jnp.concatenate([wkv[:, :, :QK_NOPE], zk], axis=2).reshape(KV_RANK, MLA_HEADS * HEAD_PAD)
    wv = jnp.transpose(wkv[:, :, QK_NOPE:], (1, 2, 0))
    wvt = jnp.concatenate([wv, jnp.zeros((MLA_HEADS, VT_ROWS - V_DIM, KV_RANK), F32)], axis=1)
    return {
        "wd": wd.astype(BF16), "gq": g_q.reshape(1, -1), "gkv": g_kv.reshape(1, -1),
        "wqa": wqa.T.astype(BF16), "wqb": wqb.T.astype(BF16), "wk": wk.astype(BF16),
        "wvt": wvt.reshape(MLA_HEADS * VT_ROWS, KV_RANK).astype(BF16),
    }


def _rope_tables(l):
    t = jnp.arange(l)
    row = (t // GRID_W).astype(F32)
    col = (t % GRID_W).astype(F32)
    n_freq = QK_ROPE // 4
    inv = ROPE_BASE ** (-jnp.arange(n_freq, dtype=F32) / n_freq)
    ang = jnp.concatenate([row[:, None] * inv, col[:, None] * inv], axis=-1)
    cos, sin = jnp.cos(ang), jnp.sin(ang)
    ones = jnp.ones((l, QK_NOPE), F32)
    zl = jnp.zeros((l, QK_NOPE), F32)
    zr = jnp.zeros((l, HEAD_PAD - QK_NOPE - QK_ROPE), F32)
    cos_t = jnp.concatenate([ones, cos, cos, zr], axis=1)
    sin_t = jnp.concatenate([zl, -sin, sin, zr], axis=1)
    return cos_t, sin_t


def _no_rope_tables(l):
    keep = jnp.concatenate([jnp.ones((l, QK_NOPE + QK_ROPE), F32),
                            jnp.zeros((l, HEAD_PAD - QK_NOPE - QK_ROPE), F32)], axis=1)
    return keep, jnp.zeros((l, HEAD_PAD), F32)


def _attn_kernel(*refs, n_lat_chunks, tk, unroll):
    heads = (0, 1)
    sls = [slice(hd * HEAD_PAD, (hd + 1) * HEAD_PAD) for hd in heads]
    if n_lat_chunks:
        q_ref, kc_ref, vtc_ref, k_ref, vt_ref, o_ref = refs[:6]
        scr = refs[6:]
        s_scr = [scr[0:2], scr[2:4]]
        p_scr = [scr[4:6], scr[6:8]]
        acc_scr = scr[8:10]
    else:
        q_ref, kc_ref, vtc_ref, o_ref = refs

    def scores(k, hd):
        return _dot(k, q_ref[0, sls[hd], :])

    init = []
    for hd in heads:
        s = scores(kc_ref[0, hd], hd)
        m = jnp.max(s, axis=0, keepdims=True)
        p = jnp.exp2(s - m).astype(BF16)
        init.append((m, _dot(vtc_ref[0, 0, hd], p)))

    if not n_lat_chunks:
        outs = [acc[:V_DIM] / acc[ONES_ROW:ONES_ROW + 1] for _, acc in init]
        o_ref[0] = jnp.concatenate(outs, axis=0).T.astype(BF16)
        return

    def stage_a(hd, slot, c):
        start = pl.multiple_of(c * tk, tk)
        s = scores(k_ref[0, hd, pl.ds(start, tk), :], hd)
        s_scr[hd][slot][...] = s
        return jnp.max(s.reshape(tk // 8, 8, s.shape[1]), axis=0)

    def stage_b(hd, slot, cm, m_old):
        m_new = jnp.maximum(m_old, jnp.max(cm, axis=0, keepdims=True))
        p_scr[hd][slot][...] = jnp.exp2(s_scr[hd][slot][...] - m_new).astype(BF16)
        return m_new, jnp.exp2(m_old - m_new)

    def stage_c(hd, slot, c, alpha):
        acc_scr[hd][...] = acc_scr[hd][...] * alpha + _dot(vt_ref[0, c, hd], p_scr[hd][slot][...])

    carry = []
    for hd in heads:
        m, acc = init[hd]
        acc_scr[hd][...] = acc
        cm0 = stage_a(hd, 0, 0)
        cm1 = stage_a(hd, 1, 1)
        m, al = stage_b(hd, 0, cm0, m)
        carry += [m, al, cm1]

    def steps(carry, c0, count):
        carry = list(carry)
        for u in range(count):
            for hd in heads:
                m, al, cm = carry[3 * hd:3 * hd + 3]
                stage_c(hd, u % 2, c0 + u, al)
                m, al = stage_b(hd, (u + 1) % 2, cm, m)
                cm = stage_a(hd, u % 2, c0 + u + 2)
                carry[3 * hd:3 * hd + 3] = [m, al, cm]
        return tuple(carry)

    peel = (n_lat_chunks - 2) % unroll
    carry = steps(carry, 0, peel)
    carry = lax.fori_loop(0, (n_lat_chunks - 2) // unroll,
                          lambda j, cr: steps(cr, peel + unroll * j, unroll), carry)
    outs = []
    for hd in heads:
        m, al, cm = carry[3 * hd:3 * hd + 3]
        stage_c(hd, 0, n_lat_chunks - 2, al)
        m, al = stage_b(hd, 1, cm, m)
        stage_c(hd, 1, n_lat_chunks - 1, al)
        acc = acc_scr[hd][...]
        outs.append(acc[:V_DIM] / acc[ONES_ROW:ONES_ROW + 1])
    o_ref[0] = jnp.concatenate(outs, axis=0).T.astype(BF16)


def _attention(q, kc, vtc, k=None, vt=None, *, tq):
    b, hw, lq = q.shape
    cl = kc.shape[2]
    nq = lq // tq
    hp = 2 * HEAD_PAD
    in_specs = [
        pl.BlockSpec((1, hp, tq), lambda bi, h, i: (bi, h, i)),
        pl.BlockSpec((1, 2, cl, HEAD_PAD), lambda bi, h, i: (bi, h, 0, 0)),
        pl.BlockSpec((1, 1, 2, VT_ROWS, cl), lambda bi, h, i: (bi, 0, h, 0, 0)),
    ]
    args = [q, kc, vtc]
    n_chunks, tk, unroll = 0, 0, 2
    scratch = []
    if k is not None:
        lk = k.shape[2]
        n_chunks, tk = vt.shape[1], vt.shape[4]
        assert n_chunks >= 2 and n_chunks % 2 == 0
        unroll = min(ATTN_UNROLL, n_chunks - 2) if n_chunks > 2 else 2
        in_specs += [
            pl.BlockSpec((1, 2, lk, HEAD_PAD), lambda bi, h, i: (bi, h, 0, 0)),
            pl.BlockSpec((1, n_chunks, 2, VT_ROWS, tk), lambda bi, h, i: (bi, 0, h, 0, 0)),
        ]
        args += [k, vt]
        scratch = ([pltpu.VMEM((tk, tq), F32)] * 4 + [pltpu.VMEM((tk, tq), BF16)] * 4
                   + [pltpu.VMEM((VT_ROWS, tq), F32)] * 2)
    return pl.pallas_call(
        functools.partial(_attn_kernel, n_lat_chunks=n_chunks, tk=tk, unroll=unroll),
        grid=(b, MLA_HEADS // 2, nq),
        in_specs=in_specs,
        out_specs=pl.BlockSpec((1, tq, 2 * V_DIM), lambda bi, h, i: (bi, i, h)),
        out_shape=jax.ShapeDtypeStruct((b, lq, MLA_HEADS * V_DIM), BF16),
        scratch_shapes=scratch,
        compiler_params=_cparams("parallel", "parallel", "arbitrary"),
        name="attention" if n_chunks else "attention_ctx",
    )(*args)


def _proj_post_kernel(a_ref, w_ref, b_ref, x_ref, gt_ref, g_ref, o_ref):
    y = _dot(a_ref[0], w_ref[...]) + b_ref[...]
    o_ref[0] = x_ref[0] + gt_ref[0] * _rms(y, g_ref[...])


def _proj_post(a, w, bias, x, gt, g, tm):
    b, l, d = x.shape
    din = a.shape[2]
    return pl.pallas_call(
        _proj_post_kernel,
        grid=(b, l // tm),
        in_specs=[
            pl.BlockSpec((1, tm, din), lambda bi, i: (bi, i, 0)),
            pl.BlockSpec((din, d), lambda bi, i: (0, 0)),
            pl.BlockSpec((1, d), lambda bi, i: (0, 0)),
            pl.BlockSpec((1, tm, d), lambda bi, i: (bi, i, 0)),
            pl.BlockSpec((1, 1, d), lambda bi, i: (bi, 0, 0)),
            pl.BlockSpec((1, d), lambda bi, i: (0, 0)),
        ],
        out_specs=pl.BlockSpec((1, tm, d), lambda bi, i: (bi, i, 0)),
        out_shape=jax.ShapeDtypeStruct((b, l, d), F32),
        compiler_params=_cparams("parallel", "parallel"),
        name="proj_post",
    )(a, w, bias, x, gt, g)


def _mlp_kernel(x_ref, sh_ref, sc_ref, gt_ref, g1_ref, g2_ref, w1_ref, w2_ref, o_ref, *, ff_chunk):
    x = x_ref[0]
    h = (_rms(x, g1_ref[...]) * (1.0 + sc_ref[0]) + sh_ref[0]).astype(BF16)
    dff = w1_ref.shape[1]
    m = None
    for c in range(dff // ff_chunk):
        u = _dot(h, w1_ref[:, c * ff_chunk:(c + 1) * ff_chunk])
        u = jnp.maximum(u, 0.0)
        part = _dot((u * u).astype(BF16), w2_ref[c * ff_chunk:(c + 1) * ff_chunk, :])
        m = part if m is None else m + part
    o_ref[0] = x + gt_ref[0] * _rms(m, g2_ref[...])


def _mlp(x, sh, sc, gt, g1, g2, w1, w2, tm):
    b, l, d = x.shape
    dff = w1.shape[1]
    vec = pl.BlockSpec((1, 1, d), lambda bi, i: (bi, 0, 0))
    row = pl.BlockSpec((1, d), lambda bi, i: (0, 0))
    return pl.pallas_call(
        functools.partial(_mlp_kernel, ff_chunk=1024),
        grid=(b, l // tm),
        in_specs=[
            pl.BlockSpec((1, tm, d), lambda bi, i: (bi, i, 0)),
            vec, vec, vec, row, row,
            pl.BlockSpec((d, dff), lambda bi, i: (0, 0)),
            pl.BlockSpec((dff, d), lambda bi, i: (0, 0)),
        ],
        out_specs=pl.BlockSpec((1, tm, d), lambda bi, i: (bi, i, 0)),
        out_shape=jax.ShapeDtypeStruct((b, l, d), F32),
        compiler_params=_cparams("parallel", "parallel"),
        name="mlp",
    )(x, sh, sc, gt, g1, g2, w1, w2)


HALO = 8


def _hy_in_kernel(x_ref, xp_ref, xn_ref, sh_ref, sc_ref, g_ref, w_ref, b_ref, ws_ref, bs_ref,
                  x1_ref, x2_ref, v_ref, u_scr):
    i = pl.program_id(1)
    n = pl.num_programs(1)
    tm = x_ref.shape[1]
    d = x_ref.shape[2]
    xa = jnp.concatenate([xp_ref[0], x_ref[0], xn_ref[0]], axis=0)
    h = (_rms(xa, g_ref[...]) * (1.0 + sc_ref[0]) + sh_ref[0]).astype(BF16)
    u_scr[...] = _dot(h, w_ref[...]) + b_ref[...]

    @pl.when(i == 0)
    def _():
        u_scr[0:HALO, :] = jnp.zeros((HALO, 3 * d), F32)

    @pl.when(i == n - 1)
    def _():
        u_scr[tm + HALO:tm + 2 * HALO, :] = jnp.zeros((HALO, 3 * d), F32)

    for j, o_ref in enumerate((x1_ref, x2_ref, v_ref)):
        sl = slice(j * d, (j + 1) * d)
        y = (u_scr[HALO - 1:HALO - 1 + tm, sl] * ws_ref[0:1, sl]
             + u_scr[HALO:HALO + tm, sl] * ws_ref[1:2, sl]
             + u_scr[HALO + 1:HALO + 1 + tm, sl] * ws_ref[2:3, sl]
             + bs_ref[:, sl])
        o_ref[0] = y.astype(BF16)


def _hy_in(x, sh, sc, g, w_in, b_in, w_short, b_short, tm):
    b, l, d = x.shape
    nt = l // tm
    tb = tm // HALO
    nb = l // HALO
    vec = pl.BlockSpec((1, 1, d), lambda bi, i: (bi, 0, 0))
    out = pl.BlockSpec((1, tm, d), lambda bi, i: (bi, i, 0))
    return pl.pallas_call(
        _hy_in_kernel,
        grid=(b, nt),
        in_specs=[
            pl.BlockSpec((1, tm, d), lambda bi, i: (bi, i, 0)),
            pl.BlockSpec((1, HALO, d), lambda bi, i: (bi, jnp.maximum(i * tb - 1, 0), 0)),
            pl.BlockSpec((1, HALO, d), lambda bi, i: (bi, jnp.minimum((i + 1) * tb, nb - 1), 0)),
            vec, vec,
            pl.BlockSpec((1, d), lambda bi, i: (0, 0)),
            pl.BlockSpec((d, 3 * d), lambda bi, i: (0, 0)),
            pl.BlockSpec((1, 3 * d), lambda bi, i: (0, 0)),
            pl.BlockSpec((3, 3 * d), lambda bi, i: (0, 0)),
            pl.BlockSpec((1, 3 * d), lambda bi, i: (0, 0)),
        ],
        out_specs=[out, out, out],
        out_shape=[jax.ShapeDtypeStruct((b, l, d), BF16)] * 3,
        scratch_shapes=[pltpu.VMEM((tm + 2 * HALO, 3 * d), F32)],
        compiler_params=_cparams("parallel", "parallel"),
        name="hyena_in",
    )(x, x, x, sh, sc, g, w_in, b_in, w_short, b_short)


def _filt_kernel(z_ref, w1_ref, b1_ref, w2_ref, b2_ref, w3_ref, fr_ref, dl_ref, o_ref, *, l):
    tl = z_ref.shape[0]
    d = dl_ref.shape[1]
    fr = fr_ref[...]
    hdn = jnp.sin(fr * (_dot3(z_ref[...], w1_ref[...]) + b1_ref[...]))
    hdn = jnp.sin(fr * (_dot3(hdn, w2_ref[...]) + b2_ref[...]))
    hf = _dot3(hdn, w3_ref[...])
    rows = pl.program_id(0) * tl + lax.broadcasted_iota(jnp.int32, (tl, d), 0)
    t = rows.astype(F32) * (1.0 / (l - 1))
    decay = jnp.exp(-t * jnp.abs(dl_ref[...]))
    for j in range(2 * HY_ORDER):
        f = hf[:, j * d:(j + 1) * d] * decay
        if j % 2 == 1:
            f = jnp.where(rows == 0, 0.0, f)
        o_ref[j] = f.astype(BF16)


def _filters(l, d, f_w1, f_b1, f_w2, f_b2, f_w3, f_freq):
    t = jnp.linspace(0.0, 1.0, l, dtype=F32)[:, None]
    w = (2.0 * math.pi) * jnp.arange(l, dtype=F32)[:, None] / l
    f = jnp.linspace(1e-4, HY_BANDS - 1, HY_BANDS, dtype=F32)[None, :]
    kz = HY_FILTER_HIDDEN
    z = jnp.concatenate([t, jnp.cos(f * w), -jnp.sin(f * w), jnp.zeros((l, kz - HY_EMB), F32)], axis=-1)
    w1 = jnp.concatenate([f_w1, jnp.zeros((kz - HY_EMB, HY_FILTER_HIDDEN), F32)], axis=0)
    min_decay = math.log(HY_TARGET) / HY_SLOW_PCT
    max_decay = math.log(HY_TARGET) / HY_FAST_PCT
    deltas = jnp.linspace(min_decay, max_decay, d, dtype=F32)[None, :]
    tl = 256
    full = lambda a: pl.BlockSpec(a.shape, lambda i: (0,) * a.ndim)
    ops = [w1, f_b1.reshape(1, -1), f_w2, f_b2.reshape(1, -1), f_w3, f_freq.reshape(1, -1), deltas]
    return pl.pallas_call(
        functools.partial(_filt_kernel, l=l),
        grid=(l // tl,),
        in_specs=[pl.BlockSpec((tl, kz), lambda i: (i, 0))] + [full(a) for a in ops],
        out_specs=pl.BlockSpec((2 * HY_ORDER, tl, d), lambda i: (0, i, 0)),
        out_shape=jax.ShapeDtypeStruct((2 * HY_ORDER, l, d), BF16),
        compiler_params=_cparams("parallel"),
        name="hyena_filters",
    )(z, *ops)


SLOT_BLOCK = 16
J_BLOCK = 16
STAGE1_TD = 256
STAGE2_TD = 128
MID_TD = 512


def _dft_tables(l):
    n = 2 * l
    n2 = DFT_N2
    n1 = n // n2
    n1h = n1 // 2
    slots = -(-(n1h + 1) // SLOT_BLOCK) * SLOT_BLOCK
    k1 = jnp.arange(slots)
    valid = (k1 <= n1h)[:, None]
    a = jnp.arange(n1h)
    ang1 = (2.0 * math.pi / n1) * ((k1[:, None] * a[None, :]) % n1).astype(F32)
    f1 = jnp.concatenate([jnp.where(valid, jnp.cos(ang1), 0.0), jnp.where(valid, -jnp.sin(ang1), 0.0)], axis=0)
    kk = jnp.arange(n1h)
    wgt = jnp.where(kk == 0, 1.0, 2.0)[None, :] / n
    ang2 = (2.0 * math.pi / n1) * ((a[:, None] * kk[None, :]) % n1).astype(F32)
    f2 = jnp.concatenate([wgt * jnp.cos(ang2), -wgt * jnp.sin(ang2)], axis=1)
    k2 = jnp.arange(n2)
    j = jnp.arange(n2)
    prod = (j[None, None, :] * (k1[:, None, None] + n1 * k2[None, :, None])) % n
    th = (2.0 * math.pi / n) * prod.astype(F32)
    live = valid[:, :, None]
    c, s = jnp.where(live, jnp.cos(th), 0.0), jnp.where(live, jnp.sin(th), 0.0)
    g = jnp.concatenate([jnp.concatenate([c, s], axis=2), jnp.concatenate([-s, c], axis=2)], axis=1)
    ct, st = jnp.swapaxes(c, 1, 2), jnp.swapaxes(s, 1, 2)
    gi = jnp.concatenate([jnp.concatenate([ct, -st], axis=2), jnp.concatenate([st, ct], axis=2)], axis=1)
    return {"n": n, "n1h": n1h, "slots": slots,
            "f1": f1.astype(BF16), "f2": f2.astype(BF16), "g": g.astype(BF16), "gi": gi.astype(BF16)}


def _swap_rows(x):
    return pltpu.einshape("abc->bac", x)


def _stage1_kernel(f_ref, u_ref, a_ref, xs_scr):
    jb = pl.program_id(2)
    n2, n1h, td = xs_scr.shape

    @pl.when(jb == 0)
    def _():
        xs_scr[...] = _swap_rows(u_ref[0].reshape(n1h, n2, td))

    for jj in range(J_BLOCK):
        a_ref[0, jj] = _dot(f_ref[...], xs_scr[jb * J_BLOCK + jj]).astype(BF16)


def _stage1(u, tab):
    b, l, d = u.shape
    n1h, slots = tab["n1h"], tab["slots"]
    n2 = DFT_N2
    td = STAGE1_TD
    return pl.pallas_call(
        _stage1_kernel,
        grid=(b, d // td, n2 // J_BLOCK),
        in_specs=[
            pl.BlockSpec((2 * slots, n1h), lambda bi, ci, jb: (0, 0)),
            pl.BlockSpec((1, l, td), lambda bi, ci, jb: (bi, 0, ci)),
        ],
        out_specs=pl.BlockSpec((1, J_BLOCK, 2 * slots, td), lambda bi, ci, jb: (bi, jb, 0, ci)),
        out_shape=jax.ShapeDtypeStruct((b, n2, 2 * slots, d), BF16),
        scratch_shapes=[pltpu.VMEM((n2, n1h, td), BF16)],
        compiler_params=_cparams("parallel", "parallel", "arbitrary"),
        name="dft_stage1",
    )(tab["f1"], u)


def _slot_rows(ref):
    n2, _, sb, td = ref.shape
    return _swap_rows(ref[...].reshape(n2, 2 * sb, td))


def _filt_spec_kernel(g_ref, af_ref, ab_ref, bias_ref, k_ref, f_scr, b_scr):
    n2 = f_scr.shape[1]
    f_scr[...] = _slot_rows(af_ref.at[0])
    b_scr[...] = _slot_rows(ab_ref.at[0])
    bias = bias_ref[0]

    def body(s, carry):
        g = g_ref[s]
        xf = _dot(g, jnp.concatenate([f_scr[s], f_scr[SLOT_BLOCK + s]], axis=0))
        xb = _dot(g, jnp.concatenate([b_scr[s], b_scr[SLOT_BLOCK + s]], axis=0))
        k_ref[0, 0, s] = (xf[:n2] + xb[:n2] + bias).astype(BF16)
        k_ref[0, 1, s] = (xf[n2:] - xb[n2:]).astype(BF16)
        return carry

    lax.fori_loop(0, SLOT_BLOCK, body, 0, unroll=4)


def _filt_spectrum(a_filt, bias, tab):
    _, n2, _, d = a_filt.shape
    slots = tab["slots"]
    td = MID_TD
    a5 = a_filt.reshape(a_filt.shape[0], n2, 2, slots, d)
    blk = lambda seq: pl.BlockSpec((1, n2, 2, SLOT_BLOCK, td), lambda o, kb, ci: (2 * o + seq, 0, 0, kb, ci))
    return pl.pallas_call(
        _filt_spec_kernel,
        grid=(HY_ORDER, slots // SLOT_BLOCK, d // td),
        in_specs=[
            pl.BlockSpec((SLOT_BLOCK, 2 * n2, 2 * n2), lambda o, kb, ci: (kb, 0, 0)),
            blk(0), blk(1),
            pl.BlockSpec((1, 1, td), lambda o, kb, ci: (o, 0, ci)),
        ],
        out_specs=pl.BlockSpec((1, 2, SLOT_BLOCK, n2, td), lambda o, kb, ci: (o, 0, kb, 0, ci)),
        out_shape=jax.ShapeDtypeStruct((HY_ORDER, 2, slots, n2, d), BF16),
        scratch_shapes=[pltpu.VMEM((2 * SLOT_BLOCK, n2, td), BF16)] * 2,
        compiler_params=_cparams("parallel", "parallel", "parallel"),
        name="hyena_filter_spectrum",
    )(tab["g"], a5, a5, bias.reshape(HY_ORDER, 1, d))


def _mid_kernel(g_ref, gi_ref, a_ref, k_ref, z_ref, a_scr, z_scr):
    n2 = a_scr.shape[1]
    a_scr[...] = _slot_rows(a_ref.at[0])

    def body(s, carry):
        x = _dot(g_ref[s], jnp.concatenate([a_scr[s], a_scr[SLOT_BLOCK + s]], axis=0))
        xr, xi = x[:n2], x[n2:]
        kr = k_ref[0, 0, s].astype(F32)
        ki = k_ref[0, 1, s].astype(F32)
        y = jnp.concatenate([xr * kr - xi * ki, xr * ki + xi * kr], axis=0).astype(BF16)
        z = _dot(gi_ref[s], y)
        z_scr[s] = z[:n2].astype(BF16)
        z_scr[SLOT_BLOCK + s] = z[n2:].astype(BF16)
        return carry

    lax.fori_loop(0, SLOT_BLOCK, body, 0, unroll=4)
    td = z_scr.shape[2]
    z_ref[0] = _swap_rows(z_scr[...]).reshape(n2, 2, SLOT_BLOCK, td)


def _mid(a, kf, order, tab):
    b, n2, _, d = a.shape
    slots = tab["slots"]
    td = MID_TD
    a5 = a.reshape(b, n2, 2, slots, d)
    gspec = pl.BlockSpec((SLOT_BLOCK, 2 * n2, 2 * n2), lambda bi, kb, ci: (kb, 0, 0))
    aspec = pl.BlockSpec((1, n2, 2, SLOT_BLOCK, td), lambda bi, kb, ci: (bi, 0, 0, kb, ci))
    z = pl.pallas_call(
        _mid_kernel,
        grid=(b, slots // SLOT_BLOCK, d // td),
        in_specs=[
            gspec, gspec, aspec,
            pl.BlockSpec((1, 2, SLOT_BLOCK, n2, td), lambda bi, kb, ci: (order, 0, kb, 0, ci)),
        ],
        out_specs=aspec,
        out_shape=jax.ShapeDtypeStruct((b, n2, 2, slots, d), BF16),
        scratch_shapes=[pltpu.VMEM((2 * SLOT_BLOCK, n2, td), BF16)] * 2,
        compiler_params=_cparams("parallel", "parallel", "parallel"),
        name="dft_mid",
    )(tab["g"], tab["gi"], a5, kf)
    return z.reshape(b, n2, 2 * slots, d)


def _stage2_kernel(f_ref, z_ref, x_ref, o_ref, ys_scr, *, slots, inv_n):
    jb = pl.program_id(2)
    n2, n1h, td = ys_scr.shape
    rows = lax.broadcasted_iota(jnp.int32, (n1h, td), 0)
    for jj in range(J_BLOCK):
        z = z_ref[0, jj]
        y = _dot(f_ref[...], jnp.concatenate([z[0:n1h], z[slots:slots + n1h]], axis=0))
        nyq = z[n1h:n1h + 1].astype(F32) * inv_n
        ys_scr[jb * J_BLOCK + jj] = (y + jnp.where(rows % 2 == 0, nyq, -nyq)).astype(BF16)

    @pl.when(jb == pl.num_programs(2) - 1)
    def _():
        o_ref[0] = _swap_rows(ys_scr[...]).reshape(n1h * n2, td) * x_ref[0]


def _stage2(z, gate, tab):
    b, n2, _, d = z.shape
    n1h, slots = tab["n1h"], tab["slots"]
    l = n1h * n2
    td = STAGE2_TD
    return pl.pallas_call(
        functools.partial(_stage2_kernel, slots=slots, inv_n=1.0 / tab["n"]),
        grid=(b, d // td, n2 // J_BLOCK),
        in_specs=[
            pl.BlockSpec((n1h, 2 * n1h), lambda bi, ci, jb: (0, 0)),
            pl.BlockSpec((1, J_BLOCK, 2 * slots, td), lambda bi, ci, jb: (bi, jb, 0, ci)),
            pl.BlockSpec((1, l, td), lambda bi, ci, jb: (bi, 0, ci)),
        ],
        out_specs=pl.BlockSpec((1, l, td), lambda bi, ci, jb: (bi, 0, ci)),
        out_shape=jax.ShapeDtypeStruct((b, l, d), BF16),
        scratch_shapes=[pltpu.VMEM((n2, n1h, td), BF16)],
        compiler_params=_cparams("parallel", "parallel", "arbitrary"),
        name="dft_stage2",
    )(tab["f2"], z, gate)


def _small_dft_tables(l):
    n = 2 * l
    k = jnp.arange(l)
    t = jnp.arange(l)
    ang = (2.0 * math.pi / n) * ((k[:, None] * t[None, :]) % n).astype(F32)
    sign = jnp.where(t % 2 == 0, 1.0, -1.0)[None, :]
    fwd = jnp.concatenate([jnp.cos(ang), -jnp.sin(ang), sign, jnp.zeros((7, l), F32)], axis=0)
    wgt = jnp.where(k == 0, 1.0, 2.0)[None, :] / n
    ang_t = ang.T
    inv = jnp.concatenate([wgt * jnp.cos(ang_t), -wgt * jnp.sin(ang_t)], axis=1)
    return {"n": n, "fwd": fwd.astype(BF16), "inv": inv.astype(BF16)}


def _small_spec_kernel(f_ref, hf_ref, hb_ref, bias_ref, k_ref):
    l = hf_ref.shape[1]
    xf = _dot(f_ref[...], hf_ref[0])
    xb = _dot(f_ref[...], hb_ref[0])
    bias = bias_ref[0]
    k_ref[0, 0:l, :] = (xf[:l] + xb[:l] + bias).astype(BF16)
    k_ref[0, l:2 * l, :] = (xf[l:2 * l] - xb[l:2 * l]).astype(BF16)
    k_ref[0, 2 * l:, :] = (xf[2 * l:] + xb[2 * l:] + bias).astype(BF16)


def _small_filt_spectrum(filt, bias, tab):
    _, l, d = filt.shape
    rows = 2 * l + 8
    tc = 512
    return pl.pallas_call(
        _small_spec_kernel,
        grid=(HY_ORDER, d // tc),
        in_specs=[
            pl.BlockSpec((rows, l), lambda o, j: (0, 0)),
            pl.BlockSpec((1, l, tc), lambda o, j: (2 * o, 0, j)),
            pl.BlockSpec((1, l, tc), lambda o, j: (2 * o + 1, 0, j)),
            pl.BlockSpec((1, 1, tc), lambda o, j: (o, 0, j)),
        ],
        out_specs=pl.BlockSpec((1, rows, tc), lambda o, j: (o, 0, j)),
        out_shape=jax.ShapeDtypeStruct((HY_ORDER, rows, d), BF16),
        compiler_params=_cparams("parallel", "parallel"),
        name="hyena_filter_spectrum_ctx",
    )(tab["fwd"], filt, filt, bias.reshape(HY_ORDER, 1, d))


def _small_conv_kernel(f_ref, fi_ref, u_ref, k_ref, x_ref, o_ref, *, inv_n):
    l = u_ref.shape[1]
    x = _dot(f_ref[...], u_ref[0])
    xr, xi = x[:l], x[l:2 * l]
    kr = k_ref[0, 0:l, :].astype(F32)
    ki = k_ref[0, l:2 * l, :].astype(F32)
    y = jnp.concatenate([xr * kr - xi * ki, xr * ki + xi * kr], axis=0).astype(BF16)
    out = _dot(fi_ref[...], y)
    nyq = x[2 * l:2 * l + 1] * k_ref[0, 2 * l:2 * l + 1, :].astype(F32) * inv_n
    rows = lax.broadcasted_iota(jnp.int32, out.shape, 0)
    out = out + jnp.where(rows % 2 == 0, nyq, -nyq)
    o_ref[0] = (out * x_ref[0].astype(F32)).astype(BF16)


def _small_conv(u, kf, order, gate, tab):
    b, l, d = u.shape
    rows = 2 * l + 8
    tc = 512
    return pl.pallas_call(
        functools.partial(_small_conv_kernel, inv_n=1.0 / tab["n"]),
        grid=(b, d // tc),
        in_specs=[
            pl.BlockSpec((rows, l), lambda bi, j: (0, 0)),
            pl.BlockSpec((l, 2 * l), lambda bi, j: (0, 0)),
            pl.BlockSpec((1, l, tc), lambda bi, j: (bi, 0, j)),
            pl.BlockSpec((1, rows, tc), lambda bi, j: (order, 0, j)),
            pl.BlockSpec((1, l, tc), lambda bi, j: (bi, 0, j)),
        ],
        out_specs=pl.BlockSpec((1, l, tc), lambda bi, j: (bi, 0, j)),
        out_shape=jax.ShapeDtypeStruct((b, l, d), BF16),
        compiler_params=_cparams("parallel", "parallel"),
        name="long_conv_ctx",
    )(tab["fwd"], tab["inv"], u, kf, gate)


def _row_tile(l):
    return 512 if l % 512 == 0 else 256


def _hyena_mixer_lat(x, sh, sc, g, p, tabs):
    b, l, d = x.shape
    x1, x2, v = _hy_in(x, sh, sc, g, p["w_in"], p["b_in"], p["w_short"], p["b_short"], _row_tile(l))
    filt = _filters(l, d, *p["filter"])
    kf = _filt_spectrum(_stage1(filt, tabs), p["bias"], tabs)
    z = _stage2(_mid(_stage1(v, tabs), kf, 0, tabs), x1, tabs)
    z = _stage2(_mid(_stage1(z, tabs), kf, 1, tabs), x2, tabs)
    return z


def _hyena_mixer_ctx(x, sh, sc, g, p, tabs):
    b, l, d = x.shape
    x1, x2, v = _hy_in(x, sh, sc, g, p["w_in"], p["b_in"], p["w_short"], p["b_short"], l)
    filt = _filters(l, d, *p["filter"])
    kf = _small_filt_spectrum(filt, p["bias"], tabs)
    z = _small_conv(v, kf, 0, x1, tabs)
    z = _small_conv(z, kf, 1, x2, tabs)
    return z


def kernel(x, c, ctx, c_ctx, ada_w, ada_b, norm_g, mla_w_dq, mla_g_q, mla_w_uq, mla_w_dkv, mla_g_kv,
           mla_w_ukv, mla_w_o, hy_w_in, hy_b_in, hy_w_short, hy_b_short, hy_f_w1, hy_f_b1, hy_f_w2,
           hy_f_b2, hy_f_w3, hy_f_freq, hy_bias, hy_w_out, hy_b_out, mlp_w1, mlp_w2):
    b, l, d = x.shape
    cl = ctx.shape[1]
    depth = ada_w.shape[0]
    assert b + 1 <= 8 and l % 256 == 0 and cl % 256 == 0

    cvec = jnp.concatenate([c, c_ctx[None, :], jnp.zeros((8 - b - 1, d), F32)], axis=0)
    mods = _ada(cvec, ada_w, ada_b)

    rope_lat = _rope_tables(l)
    rope_ctx = _no_rope_tables(cl)
    tabs_lat = _dft_tables(l)
    tabs_ctx = _small_dft_tables(cl)
    zero_bias = jnp.zeros((1, d), F32)
    tm = _row_tile(l)

    xc = ctx
    for i in range(depth):
        last = i == depth - 1
        j = i // 2
        g = norm_g[i].reshape(4, 1, d)
        m_lat = mods[i, :b].reshape(b, 6, 1, d)
        m_ctx = jnp.broadcast_to(mods[i, b].reshape(1, 6, 1, d), (b, 6, 1, d))
        lat = [m_lat[:, k] for k in range(6)]
        cx = [m_ctx[:, k] for k in range(6)]

        if i % 2 == 0:
            w = _mla_weights(mla_w_dq[j], mla_g_q[j], mla_w_uq[j], mla_w_dkv[j], mla_g_kv[j], mla_w_ukv[j])
            wo = mla_w_o[j].astype(BF16)
            qc, kc, vtc = _qkv(xc, cx[0], cx[1], g[0], w, *rope_ctx, cl)
            ql, kl, vtl = _qkv(x, lat[0], lat[1], g[0], w, *rope_lat, ATTN_TK)
            o_lat = _attention(ql, kc, vtc, kl, vtl, tq=min(ATTN_TQ, l))
            x = _proj_post(o_lat, wo, zero_bias, x, lat[2], g[1], tm)
            if not last:
                o_ctx = _attention(qc, kc, vtc, tq=cl)
                xc = _proj_post(o_ctx, wo, zero_bias, xc, cx[2], g[1], cl)
        else:
            p = {
                "w_in": hy_w_in[j].astype(BF16), "b_in": hy_b_in[j].reshape(1, -1),
                "w_short": hy_w_short[j], "b_short": hy_b_short[j].reshape(1, -1),
                "filter": (hy_f_w1[j], hy_f_b1[j], hy_f_w2[j], hy_f_b2[j], hy_f_w3[j], hy_f_freq[j]),
                "bias": hy_bias[j],
            }
            wo = hy_w_out[j].astype(BF16)
            bo = hy_b_out[j].reshape(1, d)
            z_lat = _hyena_mixer_lat(x, lat[0], lat[1], g[0], p, tabs_lat)
            x = _proj_post(z_lat, wo, bo, x, lat[2], g[1], tm)
            if not last:
                z_ctx = _hyena_mixer_ctx(xc, cx[0], cx[1], g[0], p, tabs_ctx)
                xc = _proj_post(z_ctx, wo, bo, xc, cx[2], g[1], cl)

        w1 = mlp_w1[i].astype(BF16)
        w2 = mlp_w2[i].astype(BF16)
        x = _mlp(x, lat[3], lat[4], lat[5], g[2], g[3], w1, w2, tm)
        if not last:
            xc = _mlp(xc, cx[3], cx[4], cx[5], g[2], g[3], w1, w2, cl)
    return x
```

```python
import functools
import math

import numpy as np
import jax
import jax.numpy as jnp
from jax import lax
from jax.experimental import pallas as pl
from jax.experimental.pallas import tpu as pltpu

F32 = jnp.float32
BF16 = jnp.bfloat16

GRID_W = 64
MLA_HEADS = 16
QK_NOPE = 64
QK_ROPE = 32
V_DIM = 64
Q_RANK = 256
KV_RANK = 128
ROPE_BASE = 10000.0
HY_ORDER = 2
HY_EMB = 33
HY_BANDS = (HY_EMB - 1) // 2
HY_FILTER_HIDDEN = 64
HY_TARGET = 1e-2
HY_FAST_PCT = 0.3
HY_SLOW_PCT = 1.5
EPS = 1e-6

LANES = 128
HEAD_PAD = LANES
VT_ROWS = 80
ONES_ROW = V_DIM
VMEM_LIMIT = 56 * 1024 * 1024
DFT_N2 = 128
ATTN_TQ = 512
ATTN_TK = 256
ATTN_UNROLL = 10

LOG2E = 1.4426950408889634


def _cparams(*sem):
    return pltpu.CompilerParams(dimension_semantics=sem, vmem_limit_bytes=VMEM_LIMIT)


def _dot(a, b):
    return jnp.dot(a, b, preferred_element_type=F32)


def _dot3(a, b):
    ah = a.astype(BF16)
    al = (a - ah.astype(F32)).astype(BF16)
    bh = b.astype(BF16)
    bl = (b - bh.astype(F32)).astype(BF16)
    return _dot(ah, bh) + _dot(al, bh) + _dot(ah, bl)


def _rms(x, g):
    return x * lax.rsqrt(jnp.mean(x * x, axis=-1, keepdims=True) + EPS) * g


def _ada_kernel(c_ref, w_ref, b_ref, o_ref):
    c = c_ref[...]
    s = c / (1.0 + jnp.exp(-c))
    o_ref[0] = _dot3(s, w_ref[0]) + b_ref[0]


def _ada(cvec, ada_w, ada_b):
    depth, d, n6 = ada_w.shape
    tn = 1536
    return pl.pallas_call(
        _ada_kernel,
        grid=(depth, n6 // tn),
        in_specs=[
            pl.BlockSpec((8, d), lambda i, j: (0, 0)),
            pl.BlockSpec((1, d, tn), lambda i, j: (i, 0, j)),
            pl.BlockSpec((1, 1, tn), lambda i, j: (i, 0, j)),
        ],
        out_specs=pl.BlockSpec((1, 8, tn), lambda i, j: (i, 0, j)),
        out_shape=jax.ShapeDtypeStruct((depth, 8, n6), F32),
        compiler_params=_cparams("parallel", "parallel"),
        name="ada",
    )(cvec, ada_w, ada_b.reshape(depth, 1, n6))


def _qkv_kernel(x_ref, sh_ref, sc_ref, g_ref, wd_ref, gq_ref, gkv_ref, wqa_ref, wqb_ref, wk_ref,
                wvt_ref, cos_ref, sin_ref, cost_ref, sint_ref, q_ref, k_ref, vt_ref, *, qscale):
    h = _rms(x_ref[0], g_ref[...]) * (1.0 + sc_ref[0]) + sh_ref[0]
    t = _dot(h.astype(BF16), wd_ref[...])
    cq = _rms(t[:, :Q_RANK], gq_ref[...]).astype(BF16)
    ckv = _rms(t[:, Q_RANK:Q_RANK + KV_RANK], gkv_ref[...]).astype(BF16)
    cos = cos_ref[...]
    sin = sin_ref[...]
    o = Q_RANK + KV_RANK
    kr = t[:, o:o + LANES] * cos + t[:, o + LANES:o + 2 * LANES] * sin
    nt = (((1,), (1,)), ((), ()))
    qa = lax.dot_general(wqa_ref[...], cq, nt, preferred_element_type=F32)
    qb = lax.dot_general(wqb_ref[...], cq, nt, preferred_element_type=F32)
    kn = _dot(ckv, wk_ref[...])
    cost = cost_ref[...]
    sint = sint_ref[...]
    for hd in range(MLA_HEADS):
        sl = slice(hd * HEAD_PAD, (hd + 1) * HEAD_PAD)
        q_ref[0, sl, :] = ((qa[sl] * cost + qb[sl] * sint) * qscale).astype(BF16)
        k_ref[0, hd] = (kn[:, sl] + kr).astype(BF16)
    vt = lax.dot_general(wvt_ref[...], ckv, nt, preferred_element_type=F32)
    tm = vt.shape[1]
    vt = vt.reshape(MLA_HEADS, VT_ROWS, tm)
    ones = lax.broadcasted_iota(jnp.int32, vt.shape, 1) == ONES_ROW
    vt_ref[0, 0] = jnp.where(ones, 1.0, vt).astype(BF16)


def _qkv(x, sh, sc, g, w, cos_t, sin_t, tm):
    b, l, d = x.shape
    nt = l // tm
    hw = MLA_HEADS * HEAD_PAD
    qscale = LOG2E / math.sqrt(QK_NOPE + QK_ROPE)
    full = lambda a: pl.BlockSpec(a.shape, lambda bi, i: (0,) * a.ndim)
    return pl.pallas_call(
        functools.partial(_qkv_kernel, qscale=qscale),
        grid=(b, nt),
        in_specs=[
            pl.BlockSpec((1, tm, d), lambda bi, i: (bi, i, 0)),
            pl.BlockSpec((1, 1, d), lambda bi, i: (bi, 0, 0)),
            pl.BlockSpec((1, 1, d), lambda bi, i: (bi, 0, 0)),
            full(g), full(w["wd"]), full(w["gq"]), full(w["gkv"]), full(w["wqa"]), full(w["wqb"]),
            full(w["wk"]), full(w["wvt"]),
            pl.BlockSpec((tm, LANES), lambda bi, i: (i, 0)),
            pl.BlockSpec((tm, LANES), lambda bi, i: (i, 0)),
            pl.BlockSpec((LANES, tm), lambda bi, i: (0, i)),
            pl.BlockSpec((LANES, tm), lambda bi, i: (0, i)),
        ],
        out_specs=[
            pl.BlockSpec((1, hw, tm), lambda bi, i: (bi, 0, i)),
            pl.BlockSpec((1, MLA_HEADS, tm, HEAD_PAD), lambda bi, i: (bi, 0, i, 0)),
            pl.BlockSpec((1, 1, MLA_HEADS, VT_ROWS, tm), lambda bi, i: (bi, i, 0, 0, 0)),
        ],
        out_shape=[
            jax.ShapeDtypeStruct((b, hw, l), BF16),
            jax.ShapeDtypeStruct((b, MLA_HEADS, l, HEAD_PAD), BF16),
            jax.ShapeDtypeStruct((b, nt, MLA_HEADS, VT_ROWS, tm), BF16),
        ],
        compiler_params=_cparams("parallel", "parallel"),
        name="mla_qkv",
    )(x, sh, sc, g, w["wd"], w["gq"], w["gkv"], w["wqa"], w["wqb"], w["wk"], w["wvt"], cos_t, sin_t, cos_t.T, sin_t.T)


def _mla_weights(w_dq, g_q, w_uq, w_dkv, g_kv, w_ukv):
    d = w_dq.shape[0]
    hq = QK_NOPE + QK_ROPE
    half = QK_ROPE // 2
    w_rope = w_dkv[:, KV_RANK:]
    w_rope_sw = jnp.concatenate([w_rope[:, half:], w_rope[:, :half]], axis=1)
    zl = jnp.zeros((d, QK_NOPE), F32)
    zr = jnp.zeros((d, HEAD_PAD - hq), F32)
    wd = jnp.concatenate([w_dq, w_dkv[:, :KV_RANK], zl, w_rope, zr, zl, w_rope_sw, zr], axis=1)
    wq = w_uq.reshape(Q_RANK, MLA_HEADS, hq)
    zq = jnp.zeros((Q_RANK, MLA_HEADS, HEAD_PAD - hq), F32)
    wqa = jnp.concatenate([wq, zq], axis=2).reshape(Q_RANK, MLA_HEADS * HEAD_PAD)
    zn = jnp.zeros((Q_RANK, MLA_HEADS, QK_NOPE), F32)
    wqb = jnp.concatenate([zn, wq[:, :, QK_NOPE + half:], wq[:, :, QK_NOPE:QK_NOPE + half], zq],
                          axis=2).reshape(Q_RANK, MLA_HEADS * HEAD_PAD)
    wkv = w_ukv.reshape(KV_RANK, MLA_HEADS, QK_NOPE + V_DIM)
    zk = jnp.zeros((KV_RANK, MLA_HEADS, HEAD_PAD - QK_NOPE), F32)
    wk = jnp.concatenate([wkv[:, :, :QK_NOPE], zk], axis=2).reshape(KV_RANK, MLA_HEADS * HEAD_PAD)
    wv = jnp.transpose(wkv[:, :, QK_NOPE:], (1, 2, 0))
    wvt = jnp.concatenate([wv, jnp.zeros((MLA_HEADS, VT_ROWS - V_DIM, KV_RANK), F32)], axis=1)
    return {
        "wd": wd.astype(BF16), "gq": g_q.reshape(1, -1), "gkv": g_kv.reshape(1, -1),
        "wqa": wqa.T.astype(BF16), "wqb": wqb.T.astype(BF16), "wk": wk.astype(BF16),
        "wvt": wvt.reshape(MLA_HEADS * VT_ROWS, KV_RANK).astype(BF16),
    }


def _rope_tables(l):
    t = jnp.arange(l)
    row = (t // GRID_W).astype(F32)
    col = (t % GRID_W).astype(F32)
    n_freq = QK_ROPE // 4
    inv = ROPE_BASE ** (-jnp.arange(n_freq, dtype=F32) / n_freq)
    ang = jnp.concatenate([row[:, None] * inv, col[:, None] * inv], axis=-1)
    cos, sin = jnp.cos(ang), jnp.sin(ang)
    ones = jnp.ones((l, QK_NOPE), F32)
    zl = jnp.zeros((l, QK_NOPE), F32)
    zr = jnp.zeros((l, HEAD_PAD - QK_NOPE - QK_ROPE), F32)
    cos_t = jnp.concatenate([ones, cos, cos, zr], axis=1)
    sin_t = jnp.concatenate([zl, -sin, sin, zr], axis=1)
    return cos_t, sin_t


def _no_rope_tables(l):
    keep = jnp.concatenate([jnp.ones((l, QK_NOPE + QK_ROPE), F32),
                            jnp.zeros((l, HEAD_PAD - QK_NOPE - QK_ROPE), F32)], axis=1)
    return keep, jnp.zeros((l, HEAD_PAD), F32)


def _attn_ctx_kernel(q_ref, kc_ref, vtc_ref, o_ref):
    outs = []
    for hd in range(2):
        s = _dot(kc_ref[0, hd], q_ref[0, hd * HEAD_PAD:(hd + 1) * HEAD_PAD, :])
        p = jnp.exp2(s - jnp.max(s, axis=0, keepdims=True)).astype(BF16)
        acc = _dot(vtc_ref[0, 0, hd], p)
        outs.append(acc[:V_DIM] / acc[ONES_ROW:ONES_ROW + 1])
    o_ref[0] = jnp.concatenate(outs, axis=0).T.astype(BF16)


def _attn_kernel(q_ref, kc_ref, vtc_ref, k_ref, vt_ref, o_ref, *scr, n_lat_chunks, tk, unroll):
    heads = (0, 1)
    s_scr = [scr[0:2], scr[2:4]]
    p_scr = [scr[4:6], scr[6:8]]
    acc_scr = scr[8:10]
    tq = q_ref.shape[2]
    n_pos = n_lat_chunks + 1

    def stage_a(hd, slot, p):
        if isinstance(p, int) and p == 0:
            k = kc_ref[0, hd]
        else:
            start = (p - 1) * tk
            k = k_ref[0, hd, pl.ds(start if isinstance(p, int) else pl.multiple_of(start, tk), tk), :]
        s = _dot(k, q_ref[0, hd * HEAD_PAD:(hd + 1) * HEAD_PAD, :])
        s_scr[hd][slot][...] = s
        return jnp.max(s.reshape(tk // 8, 8, tq), axis=0)

    def stage_b(hd, slot, cm, m_old):
        m_new = jnp.maximum(m_old, jnp.max(cm, axis=0, keepdims=True))
        p_scr[hd][slot][...] = jnp.exp2(s_scr[hd][slot][...] - m_new).astype(BF16)
        return m_new, jnp.exp2(m_old - m_new)

    def stage_c(hd, slot, p, alpha):
        vt = vtc_ref[0, 0, hd] if isinstance(p, int) and p == 0 else vt_ref[0, p - 1, hd]
        acc_scr[hd][...] = acc_scr[hd][...] * alpha + _dot(vt, p_scr[hd][slot][...])

    carry = []
    for hd in heads:
        acc_scr[hd][...] = jnp.zeros(acc_scr[hd].shape, F32)
        cm0 = stage_a(hd, 0, 0)
        cm1 = stage_a(hd, 1, 1)
        m, al = stage_b(hd, 0, cm0, jnp.full((1, tq), -jnp.inf, F32))
        carry += [m, al, cm1]

    def steps(carry, p0, parity, count):
        carry = list(carry)
        for u in range(count):
            slot = (parity + u) % 2
            for hd in heads:
                m, al, cm = carry[3 * hd:3 * hd + 3]
                stage_c(hd, slot, p0 + u, al)
                m, al = stage_b(hd, 1 - slot, cm, m)
                cm = stage_a(hd, slot, p0 + u + 2)
                carry[3 * hd:3 * hd + 3] = [m, al, cm]
        return tuple(carry)

    n_steps = n_pos - 2
    peel = n_steps % unroll or min(unroll, n_steps)
    carry = steps(carry, 0, 0, peel)
    carry = lax.fori_loop(0, (n_steps - peel) // unroll,
                          lambda j, cr: steps(cr, peel + unroll * j, peel % 2, unroll), carry)
    outs = []
    last = (n_pos - 2) % 2
    for hd in heads:
        m, al, cm = carry[3 * hd:3 * hd + 3]
        stage_c(hd, last, n_pos - 2, al)
        m, al = stage_b(hd, 1 - last, cm, m)
        stage_c(hd, 1 - last, n_pos - 1, al)
        acc = acc_scr[hd][...]
        outs.append(acc[:V_DIM] / acc[ONES_ROW:ONES_ROW + 1])
    o_ref[0] = jnp.concatenate(outs, axis=0).T.astype(BF16)


def _attention(q, kc, vtc, k=None, vt=None, *, tq):
    b, hw, lq = q.shape
    cl = kc.shape[2]
    nq = lq // tq
    hp = 2 * HEAD_PAD
    in_specs = [
        pl.BlockSpec((1, hp, tq), lambda bi, h, i: (bi, h, i)),
        pl.BlockSpec((1, 2, cl, HEAD_PAD), lambda bi, h, i: (bi, h, 0, 0)),
        pl.BlockSpec((1, 1, 2, VT_ROWS, cl), lambda bi, h, i: (bi, 0, h, 0, 0)),
    ]
    args = [q, kc, vtc]
    n_chunks = 0
    scratch = []
    body = _attn_ctx_kernel
    if k is not None:
        lk = k.shape[2]
        n_chunks, tk = vt.shape[1], vt.shape[4]
        assert n_chunks >= 2 and cl == tk
        unroll = max(2, min(ATTN_UNROLL, (n_chunks - 2) // 2 * 2))
        body = functools.partial(_attn_kernel, n_lat_chunks=n_chunks, tk=tk, unroll=unroll)
        in_specs += [
            pl.BlockSpec((1, 2, lk, HEAD_PAD), lambda bi, h, i: (bi, h, 0, 0)),
            pl.BlockSpec((1, n_chunks, 2, VT_ROWS, tk), lambda bi, h, i: (bi, 0, h, 0, 0)),
        ]
        args += [k, vt]
        scratch = ([pltpu.VMEM((tk, tq), F32)] * 4 + [pltpu.VMEM((tk, tq), BF16)] * 4
                   + [pltpu.VMEM((VT_ROWS, tq), F32)] * 2)
    return pl.pallas_call(
        body,
        grid=(b, MLA_HEADS // 2, nq),
        in_specs=in_specs,
        out_specs=pl.BlockSpec((1, tq, 2 * V_DIM), lambda bi, h, i: (bi, i, h)),
        out_shape=jax.ShapeDtypeStruct((b, lq, MLA_HEADS * V_DIM), BF16),
        scratch_shapes=scratch,
        compiler_params=_cparams("parallel", "parallel", "arbitrary"),
        name="attention" if n_chunks else "attention_ctx",
    )(*args)


def _mlp_kernel(a_ref, wo_ref, bo_ref, gt0_ref, g0_ref, x_ref, sh_ref, sc_ref, gt_ref, g1_ref, g2_ref,
                w1_ref, w2_ref, o_ref, *, ff_chunk):
    y = _dot(a_ref[0], wo_ref[...]) + bo_ref[...]
    x = x_ref[0] + gt0_ref[0] * _rms(y, g0_ref[...])
    h = (_rms(x, g1_ref[...]) * (1.0 + sc_ref[0]) + sh_ref[0]).astype(BF16)
    dff = w1_ref.shape[1]
    m = None
    for c in range(dff // ff_chunk):
        u = _dot(h, w1_ref[:, c * ff_chunk:(c + 1) * ff_chunk])
        u = jnp.maximum(u, 0.0)
        part = _dot((u * u).astype(BF16), w2_ref[c * ff_chunk:(c + 1) * ff_chunk, :])
        m = part if m is None else m + part
    o_ref[0] = x + gt_ref[0] * _rms(m, g2_ref[...])


def _mixer_out_mlp(a, wo, bo, gt0, g0, x, sh, sc, gt, g1, g2, w1, w2, tm):
    b, l, d = x.shape
    din = a.shape[2]
    dff = w1.shape[1]
    vec = pl.BlockSpec((1, 1, d), lambda bi, i: (bi, 0, 0))
    row = pl.BlockSpec((1, d), lambda bi, i: (0, 0))
    const = lambda shape: pl.BlockSpec(shape, lambda bi, i: (0, 0), pipeline_mode=pl.Buffered(1))
    return pl.pallas_call(
        functools.partial(_mlp_kernel, ff_chunk=1024),
        grid=(b, l // tm),
        in_specs=[
            pl.BlockSpec((1, tm, din), lambda bi, i: (bi, i, 0)),
            const((din, d)), row, vec, row,
            pl.BlockSpec((1, tm, d), lambda bi, i: (bi, i, 0)),
            vec, vec, vec, row, row,
            const((d, dff)), const((dff, d)),
        ],
        out_specs=pl.BlockSpec((1, tm, d), lambda bi, i: (bi, i, 0)),
        out_shape=jax.ShapeDtypeStruct((b, l, d), F32),
        compiler_params=_cparams("parallel", "parallel"),
        name="mixer_out_mlp",
    )(a, wo, bo, gt0, g0, x, sh, sc, gt, g1, g2, w1, w2)


HALO = 8


def _hy_in_kernel(x_ref, xp_ref, xn_ref, sh_ref, sc_ref, g_ref, w_ref, b_ref, ws_ref, bs_ref,
                  x1_ref, x2_ref, v_ref, u_scr):
    i = pl.program_id(1)
    n = pl.num_programs(1)
    tm = x_ref.shape[1]
    d = x_ref.shape[2]
    xa = jnp.concatenate([xp_ref[0], x_ref[0], xn_ref[0]], axis=0)
    h = (_rms(xa, g_ref[...]) * (1.0 + sc_ref[0]) + sh_ref[0]).astype(BF16)
    u_scr[...] = _dot(h, w_ref[...]) + b_ref[...]

    @pl.when(i == 0)
    def _():
        u_scr[0:HALO, :] = jnp.zeros((HALO, 3 * d), F32)

    @pl.when(i == n - 1)
    def _():
        u_scr[tm + HALO:tm + 2 * HALO, :] = jnp.zeros((HALO, 3 * d), F32)

    for j, o_ref in enumerate((x1_ref, x2_ref, v_ref)):
        sl = slice(j * d, (j + 1) * d)
        u = u_scr[:, sl]
        rows = tm + 2 * HALO
        prev = pltpu.roll(u, 1, axis=0)[HALO:HALO + tm]
        nxt = pltpu.roll(u, rows - 1, axis=0)[HALO:HALO + tm]
        y = (prev * ws_ref[0:1, sl] + u[HALO:HALO + tm] * ws_ref[1:2, sl] + nxt * ws_ref[2:3, sl]
             + bs_ref[:, sl])
        o_ref[0] = y.astype(BF16)


def _hy_in(x, sh, sc, g, w_in, b_in, w_short, b_short, tm):
    b, l, d = x.shape
    nt = l // tm
    tb = tm // HALO
    nb = l // HALO
    vec = pl.BlockSpec((1, 1, d), lambda bi, i: (bi, 0, 0))
    out = pl.BlockSpec((1, tm, d), lambda bi, i: (bi, i, 0))
    return pl.pallas_call(
        _hy_in_kernel,
        grid=(b, nt),
        in_specs=[
            pl.BlockSpec((1, tm, d), lambda bi, i: (bi, i, 0)),
            pl.BlockSpec((1, HALO, d), lambda bi, i: (bi, jnp.maximum(i * tb - 1, 0), 0)),
            pl.BlockSpec((1, HALO, d), lambda bi, i: (bi, jnp.minimum((i + 1) * tb, nb - 1), 0)),
            vec, vec,
            pl.BlockSpec((1, d), lambda bi, i: (0, 0)),
            pl.BlockSpec((d, 3 * d), lambda bi, i: (0, 0)),
            pl.BlockSpec((1, 3 * d), lambda bi, i: (0, 0)),
            pl.BlockSpec((3, 3 * d), lambda bi, i: (0, 0)),
            pl.BlockSpec((1, 3 * d), lambda bi, i: (0, 0)),
        ],
        out_specs=[out, out, out],
        out_shape=[jax.ShapeDtypeStruct((b, l, d), BF16)] * 3,
        scratch_shapes=[pltpu.VMEM((tm + 2 * HALO, 3 * d), F32)],
        compiler_params=_cparams("parallel", "parallel"),
        name="hyena_in",
    )(x, x, x, sh, sc, g, w_in, b_in, w_short, b_short)


def _filt_kernel(z_ref, w1_ref, b1_ref, w2_ref, b2_ref, w3_ref, fr_ref, dl_ref, o_ref, *, l):
    tl = z_ref.shape[0]
    d = dl_ref.shape[1]
    fr = fr_ref[...]
    hdn = jnp.sin(fr * (_dot3(z_ref[...], w1_ref[...]) + b1_ref[...]))
    hdn = jnp.sin(fr * (_dot3(hdn, w2_ref[...]) + b2_ref[...]))
    hf = _dot3(hdn, w3_ref[...])
    rows = pl.program_id(0) * tl + lax.broadcasted_iota(jnp.int32, (tl, d), 0)
    t = rows.astype(F32) * (1.0 / (l - 1))
    decay = jnp.exp(-t * jnp.abs(dl_ref[...]))
    for j in range(2 * HY_ORDER):
        f = hf[:, j * d:(j + 1) * d] * decay
        if j % 2 == 1:
            f = jnp.where(rows == 0, 0.0, f)
        o_ref[j] = f.astype(BF16)


def _filters(l, d, f_w1, f_b1, f_w2, f_b2, f_w3, f_freq):
    t = jnp.linspace(0.0, 1.0, l, dtype=F32)[:, None]
    w = (2.0 * math.pi) * jnp.arange(l, dtype=F32)[:, None] / l
    f = jnp.linspace(1e-4, HY_BANDS - 1, HY_BANDS, dtype=F32)[None, :]
    kz = HY_FILTER_HIDDEN
    z = jnp.concatenate([t, jnp.cos(f * w), -jnp.sin(f * w), jnp.zeros((l, kz - HY_EMB), F32)], axis=-1)
    w1 = jnp.concatenate([f_w1, jnp.zeros((kz - HY_EMB, HY_FILTER_HIDDEN), F32)], axis=0)
    min_decay = math.log(HY_TARGET) / HY_SLOW_PCT
    max_decay = math.log(HY_TARGET) / HY_FAST_PCT
    deltas = jnp.linspace(min_decay, max_decay, d, dtype=F32)[None, :]
    tl = 256
    full = lambda a: pl.BlockSpec(a.shape, lambda i: (0,) * a.ndim)
    ops = [w1, f_b1.reshape(1, -1), f_w2, f_b2.reshape(1, -1), f_w3, f_freq.reshape(1, -1), deltas]
    return pl.pallas_call(
        functools.partial(_filt_kernel, l=l),
        grid=(l // tl,),
        in_specs=[pl.BlockSpec((tl, kz), lambda i: (i, 0))] + [full(a) for a in ops],
        out_specs=pl.BlockSpec((2 * HY_ORDER, tl, d), lambda i: (0, i, 0)),
        out_shape=jax.ShapeDtypeStruct((2 * HY_ORDER, l, d), BF16),
        compiler_params=_cparams("parallel"),
        name="hyena_filters",
    )(z, *ops)


SLOT_BLOCK = 16
J_BLOCK = 16
STAGE1_TD = 256
STAGE2_TD = 128
MID_TD = 512


def _dft_tables(l):
    n = 2 * l
    n2 = DFT_N2
    n1 = n // n2
    n1h = n1 // 2
    slots = -(-(n1h + 1) // SLOT_BLOCK) * SLOT_BLOCK
    k1 = jnp.arange(slots)
    valid = (k1 <= n1h)[:, None]
    a = jnp.arange(n1h)
    ang1 = (2.0 * math.pi / n1) * ((k1[:, None] * a[None, :]) % n1).astype(F32)
    f1 = jnp.concatenate([jnp.where(valid, jnp.cos(ang1), 0.0), jnp.where(valid, -jnp.sin(ang1), 0.0)], axis=0)
    kk = jnp.arange(n1h)
    wgt = jnp.where(kk == 0, 1.0, 2.0)[None, :] / n
    ang2 = (2.0 * math.pi / n1) * ((a[:, None] * kk[None, :]) % n1).astype(F32)
    f2 = jnp.concatenate([wgt * jnp.cos(ang2), -wgt * jnp.sin(ang2)], axis=1)
    k2 = jnp.arange(n2)
    j = jnp.arange(n2)
    prod = (j[None, None, :] * (k1[:, None, None] + n1 * k2[None, :, None])) % n
    th = (2.0 * math.pi / n) * prod.astype(F32)
    live = valid[:, :, None]
    c, s = jnp.where(live, jnp.cos(th), 0.0), jnp.where(live, jnp.sin(th), 0.0)
    g = jnp.concatenate([jnp.concatenate([c, s], axis=2), jnp.concatenate([-s, c], axis=2)], axis=1)
    ct, st = jnp.swapaxes(c, 1, 2), jnp.swapaxes(s, 1, 2)
    gi = jnp.concatenate([jnp.concatenate([ct, -st], axis=2), jnp.concatenate([st, ct], axis=2)], axis=1)
    return {"n": n, "n1h": n1h, "slots": slots,
            "f1": f1.astype(BF16), "f2": f2.astype(BF16), "g": g.astype(BF16), "gi": gi.astype(BF16)}


def _swap_rows(x):
    return pltpu.einshape("abc->bac", x)


def _stage1_kernel(f_ref, u_ref, a_ref, xs_scr):
    jb = pl.program_id(2)
    n2, n1h, td = xs_scr.shape

    @pl.when(jb == 0)
    def _():
        xs_scr[...] = _swap_rows(u_ref[0].reshape(n1h, n2, td))

    for jj in range(J_BLOCK):
        a_ref[0, jj] = _dot(f_ref[...], xs_scr[jb * J_BLOCK + jj]).astype(BF16)


def _stage1(u, tab):
    b, l, d = u.shape
    n1h, slots = tab["n1h"], tab["slots"]
    n2 = DFT_N2
    td = STAGE1_TD
    return pl.pallas_call(
        _stage1_kernel,
        grid=(b, d // td, n2 // J_BLOCK),
        in_specs=[
            pl.BlockSpec((2 * slots, n1h), lambda bi, ci, jb: (0, 0)),
            pl.BlockSpec((1, l, td), lambda bi, ci, jb: (bi, 0, ci)),
        ],
        out_specs=pl.BlockSpec((1, J_BLOCK, 2 * slots, td), lambda bi, ci, jb: (bi, jb, 0, ci)),
        out_shape=jax.ShapeDtypeStruct((b, n2, 2 * slots, d), BF16),
        scratch_shapes=[pltpu.VMEM((n2, n1h, td), BF16)],
        compiler_params=_cparams("parallel", "parallel", "arbitrary"),
        name="dft_stage1",
    )(tab["f1"], u)


def _slot_rows(ref):
    n2, _, sb, td = ref.shape
    return _swap_rows(ref[...].reshape(n2, 2 * sb, td))


def _filt_spec_kernel(g_ref, af_ref, ab_ref, bias_ref, k_ref, f_scr, b_scr):
    n2 = f_scr.shape[1]
    f_scr[...] = _slot_rows(af_ref.at[0])
    b_scr[...] = _slot_rows(ab_ref.at[0])
    bias = bias_ref[0]

    def body(s, carry):
        g = g_ref[s]
        xf = _dot(g, jnp.concatenate([f_scr[s], f_scr[SLOT_BLOCK + s]], axis=0))
        xb = _dot(g, jnp.concatenate([b_scr[s], b_scr[SLOT_BLOCK + s]], axis=0))
        k_ref[0, 0, s] = (xf[:n2] + xb[:n2] + bias).astype(BF16)
        k_ref[0, 1, s] = (xf[n2:] - xb[n2:]).astype(BF16)
        return carry

    lax.fori_loop(0, SLOT_BLOCK, body, 0, unroll=4)


def _filt_spectrum(a_filt, bias, tab):
    _, n2, _, d = a_filt.shape
    slots = tab["slots"]
    td = MID_TD
    a5 = a_filt.reshape(a_filt.shape[0], n2, 2, slots, d)
    blk = lambda seq: pl.BlockSpec((1, n2, 2, SLOT_BLOCK, td), lambda o, kb, ci: (2 * o + seq, 0, 0, kb, ci))
    return pl.pallas_call(
        _filt_spec_kernel,
        grid=(HY_ORDER, slots // SLOT_BLOCK, d // td),
        in_specs=[
            pl.BlockSpec((SLOT_BLOCK, 2 * n2, 2 * n2), lambda o, kb, ci: (kb, 0, 0)),
            blk(0), blk(1),
            pl.BlockSpec((1, 1, td), lambda o, kb, ci: (o, 0, ci)),
        ],
        out_specs=pl.BlockSpec((1, 2, SLOT_BLOCK, n2, td), lambda o, kb, ci: (o, 0, kb, 0, ci)),
        out_shape=jax.ShapeDtypeStruct((HY_ORDER, 2, slots, n2, d), BF16),
        scratch_shapes=[pltpu.VMEM((2 * SLOT_BLOCK, n2, td), BF16)] * 2,
        compiler_params=_cparams("parallel", "parallel", "parallel"),
        name="hyena_filter_spectrum",
    )(tab["g"], a5, a5, bias.reshape(HY_ORDER, 1, d))


def _mid_kernel(g_ref, gi_ref, a_ref, k_ref, z_ref, a_scr, z_scr):
    n2 = a_scr.shape[1]
    a_scr[...] = _slot_rows(a_ref.at[0])

    def body(s, carry):
        x = _dot(g_ref[s], jnp.concatenate([a_scr[s], a_scr[SLOT_BLOCK + s]], axis=0))
        xr, xi = x[:n2], x[n2:]
        kr = k_ref[0, 0, s].astype(F32)
        ki = k_ref[0, 1, s].astype(F32)
        y = jnp.concatenate([xr * kr - xi * ki, xr * ki + xi * kr], axis=0).astype(BF16)
        z = _dot(gi_ref[s], y)
        z_scr[s] = z[:n2].astype(BF16)
        z_scr[SLOT_BLOCK + s] = z[n2:].astype(BF16)
        return carry

    lax.fori_loop(0, SLOT_BLOCK, body, 0, unroll=4)
    td = z_scr.shape[2]
    z_ref[0] = _swap_rows(z_scr[...]).reshape(n2, 2, SLOT_BLOCK, td)


def _mid(a, kf, order, tab):
    b, n2, _, d = a.shape
    slots = tab["slots"]
    td = MID_TD
    a5 = a.reshape(b, n2, 2, slots, d)
    gspec = pl.BlockSpec((SLOT_BLOCK, 2 * n2, 2 * n2), lambda bi, kb, ci: (kb, 0, 0))
    aspec = pl.BlockSpec((1, n2, 2, SLOT_BLOCK, td), lambda bi, kb, ci: (bi, 0, 0, kb, ci))
    z = pl.pallas_call(
        _mid_kernel,
        grid=(b, slots // SLOT_BLOCK, d // td),
        in_specs=[
            gspec, gspec, aspec,
            pl.BlockSpec((1, 2, SLOT_BLOCK, n2, td), lambda bi, kb, ci: (order, 0, kb, 0, ci)),
        ],
        out_specs=aspec,
        out_shape=jax.ShapeDtypeStruct((b, n2, 2, slots, d), BF16),
        scratch_shapes=[pltpu.VMEM((2 * SLOT_BLOCK, n2, td), BF16)] * 2,
        compiler_params=_cparams("parallel", "parallel", "parallel"),
        name="dft_mid",
    )(tab["g"], tab["gi"], a5, kf)
    return z.reshape(b, n2, 2 * slots, d)


def _stage2_kernel(f_ref, z_ref, x_ref, o_ref, ys_scr, *, slots, inv_n):
    jb = pl.program_id(2)
    n2, n1h, td = ys_scr.shape
    rows = lax.broadcasted_iota(jnp.int32, (n1h, td), 0)
    for jj in range(J_BLOCK):
        z = z_ref[0, jj]
        y = _dot(f_ref[...], jnp.concatenate([z[0:n1h], z[slots:slots + n1h]], axis=0))
        nyq = z[n1h:n1h + 1].astype(F32) * inv_n
        ys_scr[jb * J_BLOCK + jj] = (y + jnp.where(rows % 2 == 0, nyq, -nyq)).astype(BF16)

    @pl.when(jb == pl.num_programs(2) - 1)
    def _():
        o_ref[0] = _swap_rows(ys_scr[...]).reshape(n1h * n2, td) * x_ref[0]


def _stage2(z, gate, tab):
    b, n2, _, d = z.shape
    n1h, slots = tab["n1h"], tab["slots"]
    l = n1h * n2
    td = STAGE2_TD
    return pl.pallas_call(
        functools.partial(_stage2_kernel, slots=slots, inv_n=1.0 / tab["n"]),
        grid=(b, d // td, n2 // J_BLOCK),
        in_specs=[
            pl.BlockSpec((n1h, 2 * n1h), lambda bi, ci, jb: (0, 0)),
            pl.BlockSpec((1, J_BLOCK, 2 * slots, td), lambda bi, ci, jb: (bi, jb, 0, ci)),
            pl.BlockSpec((1, l, td), lambda bi, ci, jb: (bi, 0, ci)),
        ],
        out_specs=pl.BlockSpec((1, l, td), lambda bi, ci, jb: (bi, 0, ci)),
        out_shape=jax.ShapeDtypeStruct((b, l, d), BF16),
        scratch_shapes=[pltpu.VMEM((n2, n1h, td), BF16)],
        compiler_params=_cparams("parallel", "parallel", "arbitrary"),
        name="dft_stage2",
    )(tab["f2"], z, gate)


def _small_dft_tables(l):
    n = 2 * l
    k = jnp.arange(l)
    t = jnp.arange(l)
    ang = (2.0 * math.pi / n) * ((k[:, None] * t[None, :]) % n).astype(F32)
    sign = jnp.where(t % 2 == 0, 1.0, -1.0)[None, :]
    fwd = jnp.concatenate([jnp.cos(ang), -jnp.sin(ang), sign, jnp.zeros((7, l), F32)], axis=0)
    wgt = jnp.where(k == 0, 1.0, 2.0)[None, :] / n
    ang_t = ang.T
    inv = jnp.concatenate([wgt * jnp.cos(ang_t), -wgt * jnp.sin(ang_t)], axis=1)
    return {"n": n, "fwd": fwd.astype(BF16), "inv": inv.astype(BF16)}


def _small_spec_kernel(f_ref, hf_ref, hb_ref, bias_ref, k_ref):
    l = hf_ref.shape[1]
    xf = _dot(f_ref[...], hf_ref[0])
    xb = _dot(f_ref[...], hb_ref[0])
    bias = bias_ref[0]
    k_ref[0, 0:l, :] = (xf[:l] + xb[:l] + bias).astype(BF16)
    k_ref[0, l:2 * l, :] = (xf[l:2 * l] - xb[l:2 * l]).astype(BF16)
    k_ref[0, 2 * l:, :] = (xf[2 * l:] + xb[2 * l:] + bias).astype(BF16)


def _small_filt_spectrum(filt, bias, tab):
    _, l, d = filt.shape
    rows = 2 * l + 8
    tc = 512
    return pl.pallas_call(
        _small_spec_kernel,
        grid=(HY_ORDER, d // tc),
        in_specs=[
            pl.BlockSpec((rows, l), lambda o, j: (0, 0)),
            pl.BlockSpec((1, l, tc), lambda o, j: (2 * o, 0, j)),
            pl.BlockSpec((1, l, tc), lambda o, j: (2 * o + 1, 0, j)),
            pl.BlockSpec((1, 1, tc), lambda o, j: (o, 0, j)),
        ],
        out_specs=pl.BlockSpec((1, rows, tc), lambda o, j: (o, 0, j)),
        out_shape=jax.ShapeDtypeStruct((HY_ORDER, rows, d), BF16),
        compiler_params=_cparams("parallel", "parallel"),
        name="hyena_filter_spectrum_ctx",
    )(tab["fwd"], filt, filt, bias.reshape(HY_ORDER, 1, d))


def _small_conv_kernel(f_ref, fi_ref, u_ref, k_ref, x_ref, o_ref, *, inv_n):
    l = u_ref.shape[1]
    x = _dot(f_ref[...], u_ref[0])
    xr, xi = x[:l], x[l:2 * l]
    kr = k_ref[0, 0:l, :].astype(F32)
    ki = k_ref[0, l:2 * l, :].astype(F32)
    y = jnp.concatenate([xr * kr - xi * ki, xr * ki + xi * kr], axis=0).astype(BF16)
    out = _dot(fi_ref[...], y)
    nyq = x[2 * l:2 * l + 1] * k_ref[0, 2 * l:2 * l + 1, :].astype(F32) * inv_n
    rows = lax.broadcasted_iota(jnp.int32, out.shape, 0)
    out = out + jnp.where(rows % 2 == 0, nyq, -nyq)
    o_ref[0] = (out * x_ref[0].astype(F32)).astype(BF16)


def _small_conv(u, kf, order, gate, tab):
    b, l, d = u.shape
    rows = 2 * l + 8
    tc = 512
    return pl.pallas_call(
        functools.partial(_small_conv_kernel, inv_n=1.0 / tab["n"]),
        grid=(b, d // tc),
        in_specs=[
            pl.BlockSpec((rows, l), lambda bi, j: (0, 0)),
            pl.BlockSpec((l, 2 * l), lambda bi, j: (0, 0)),
            pl.BlockSpec((1, l, tc), lambda bi, j: (bi, 0, j)),
            pl.BlockSpec((1, rows, tc), lambda bi, j: (order, 0, j)),
            pl.BlockSpec((1, l, tc), lambda bi, j: (bi, 0, j)),
        ],
        out_specs=pl.BlockSpec((1, l, tc), lambda bi, j: (bi, 0, j)),
        out_shape=jax.ShapeDtypeStruct((b, l, d), BF16),
        compiler_params=_cparams("parallel", "parallel"),
        name="long_conv_ctx",
    )(tab["fwd"], tab["inv"], u, kf, gate)


def _row_tile(l):
    return 512 if l % 512 == 0 else 256


def _hyena_mixer_lat(x, sh, sc, g, p, tabs):
    b, l, d = x.shape
    x1, x2, v = _hy_in(x, sh, sc, g, p["w_in"], p["b_in"], p["w_short"], p["b_short"], _row_tile(l))
    filt = _filters(l, d, *p["filter"])
    kf = _filt_spectrum(_stage1(filt, tabs), p["bias"], tabs)
    z = _stage2(_mid(_stage1(v, tabs), kf, 0, tabs), x1, tabs)
    z = _stage2(_mid(_stage1(z, tabs), kf, 1, tabs), x2, tabs)
    return z


def _hyena_mixer_ctx(x, sh, sc, g, p, tabs):
    b, l, d = x.shape
    x1, x2, v = _hy_in(x, sh, sc, g, p["w_in"], p["b_in"], p["w_short"], p["b_short"], l)
    filt = _filters(l, d, *p["filter"])
    kf = _small_filt_spectrum(filt, p["bias"], tabs)
    z = _small_conv(v, kf, 0, x1, tabs)
    z = _small_conv(z, kf, 1, x2, tabs)
    return z


def kernel(x, c, ctx, c_ctx, ada_w, ada_b, norm_g, mla_w_dq, mla_g_q, mla_w_uq, mla_w_dkv, mla_g_kv,
           mla_w_ukv, mla_w_o, hy_w_in, hy_b_in, hy_w_short, hy_b_short, hy_f_w1, hy_f_b1, hy_f_w2,
           hy_f_b2, hy_f_w3, hy_f_freq, hy_bias, hy_w_out, hy_b_out, mlp_w1, mlp_w2):
    b, l, d = x.shape
    cl = ctx.shape[1]
    depth = ada_w.shape[0]
    assert b + 1 <= 8 and l % 256 == 0 and cl % 256 == 0

    cvec = jnp.concatenate([c, c_ctx[None, :], jnp.zeros((8 - b - 1, d), F32)], axis=0)
    mods = _ada(cvec, ada_w, ada_b)

    rope_lat = _rope_tables(l)
    rope_ctx = _no_rope_tables(cl)
    tabs_lat = _dft_tables(l)
    tabs_ctx = _small_dft_tables(cl)
    zero_bias = jnp.zeros((1, d), F32)
    tm = _row_tile(l)

    xc = ctx
    for i in range(depth):
        last = i == depth - 1
        j = i // 2
        g = norm_g[i].reshape(4, 1, d)
        m_lat = mods[i, :b].reshape(b, 6, 1, d)
        m_ctx = jnp.broadcast_to(mods[i, b].reshape(1, 6, 1, d), (b, 6, 1, d))
        lat = [m_lat[:, k] for k in range(6)]
        cx = [m_ctx[:, k] for k in range(6)]

        if i % 2 == 0:
            w = _mla_weights(mla_w_dq[j], mla_g_q[j], mla_w_uq[j], mla_w_dkv[j], mla_g_kv[j], mla_w_ukv[j])
            wo = mla_w_o[j].astype(BF16)
            bo = zero_bias
            qc, kc, vtc = _qkv(xc, cx[0], cx[1], g[0], w, *rope_ctx, cl)
            ql, kl, vtl = _qkv(x, lat[0], lat[1], g[0], w, *rope_lat, ATTN_TK)
            y_lat = _attention(ql, kc, vtc, kl, vtl, tq=min(ATTN_TQ, l))
            y_ctx = None if last else _attention(qc, kc, vtc, tq=cl)
        else:
            p = {
                "w_in": hy_w_in[j].astype(BF16), "b_in": hy_b_in[j].reshape(1, -1),
                "w_short": hy_w_short[j], "b_short": hy_b_short[j].reshape(1, -1),
                "filter": (hy_f_w1[j], hy_f_b1[j], hy_f_w2[j], hy_f_b2[j], hy_f_w3[j], hy_f_freq[j]),
                "bias": hy_bias[j],
            }
            wo = hy_w_out[j].astype(BF16)
            bo = hy_b_out[j].reshape(1, d)
            y_lat = _hyena_mixer_lat(x, lat[0], lat[1], g[0], p, tabs_lat)
            y_ctx = None if last else _hyena_mixer_ctx(xc, cx[0], cx[1], g[0], p, tabs_ctx)

        w1 = mlp_w1[i].astype(BF16)
        w2 = mlp_w2[i].astype(BF16)
        x = _mixer_out_mlp(y_lat, wo, bo, lat[2], g[1], x, lat[3], lat[4], lat[5], g[2], g[3], w1, w2, tm)
        if not last:
            xc = _mixer_out_mlp(y_ctx, wo, bo, cx[2], g[1], xc, cx[3], cx[4], cx[5], g[2], g[3], w1, w2, cl)
    return x
```

```python
import functools
import math

import jax
import jax.numpy as jnp
from jax import lax
from jax.experimental import pallas as pl
from jax.experimental.pallas import tpu as pltpu

F32 = jnp.float32
BF16 = jnp.bfloat16

GRID_W = 64
MLA_HEADS = 16
QK_NOPE = 64
QK_ROPE = 32
V_DIM = 64
Q_RANK = 256
KV_RANK = 128
ROPE_BASE = 10000.0
HY_ORDER = 2
HY_EMB = 33
HY_BANDS = (HY_EMB - 1) // 2
HY_FILTER_HIDDEN = 64
HY_TARGET = 1e-2
HY_FAST_PCT = 0.3
HY_SLOW_PCT = 1.5
EPS = 1e-6

LANES = 128
SUBLANES = 8
HEAD_PAD = LANES
VT_ROWS = 80
ONES_ROW = V_DIM
VMEM_LIMIT = 56 * 1024 * 1024

ADA_TN = 1536
ATTN_TQ = 512
ATTN_TK = 256
ATTN_UNROLL = 10
MLP_ROWS = 1024
MLP_FF_CHUNK = 1024
ROW_TILE = 512
HALO = SUBLANES
FILTER_ROWS = 256
DFT_N2 = 128
SLOT_BLOCK = 16
J_BLOCK = 16
STAGE1_TD = 256
STAGE2_TD = 128
MID_TD = 512
SMALL_TD = 512

LOG2E = 1.4426950408889634


def _cparams(*sem):
    return pltpu.CompilerParams(dimension_semantics=sem, vmem_limit_bytes=VMEM_LIMIT)


def _dot(a, b):
    return jnp.dot(a, b, preferred_element_type=F32)


def _dot3(a, b):
    ah = a.astype(BF16)
    al = (a - ah.astype(F32)).astype(BF16)
    bh = b.astype(BF16)
    bl = (b - bh.astype(F32)).astype(BF16)
    return _dot(ah, bh) + _dot(al, bh) + _dot(ah, bl)


def _rms(x, g):
    return x * lax.rsqrt(jnp.mean(x * x, axis=-1, keepdims=True) + EPS) * g


def _ada_kernel(c_ref, w_ref, b_ref, o_ref):
    c = c_ref[...]
    s = c / (1.0 + jnp.exp(-c))
    o_ref[0] = _dot3(s, w_ref[0]) + b_ref[0]


def _ada(cvec, ada_w, ada_b):
    depth, d, n6 = ada_w.shape
    rows = cvec.shape[0]
    return pl.pallas_call(
        _ada_kernel,
        grid=(depth, n6 // ADA_TN),
        in_specs=[
            pl.BlockSpec((rows, d), lambda i, j: (0, 0)),
            pl.BlockSpec((1, d, ADA_TN), lambda i, j: (i, 0, j)),
            pl.BlockSpec((1, 1, ADA_TN), lambda i, j: (i, 0, j)),
        ],
        out_specs=pl.BlockSpec((1, rows, ADA_TN), lambda i, j: (i, 0, j)),
        out_shape=jax.ShapeDtypeStruct((depth, rows, n6), F32),
        compiler_params=_cparams("parallel", "parallel"),
        name="ada",
    )(cvec, ada_w, ada_b.reshape(depth, 1, n6))


def _qkv_kernel(x_ref, sh_ref, sc_ref, g_ref, wd_ref, gq_ref, gkv_ref, wqa_ref, wqb_ref, wk_ref,
                wvt_ref, cos_ref, sin_ref, cost_ref, sint_ref, q_ref, k_ref, vt_ref, *, qscale):
    h = _rms(x_ref[0], g_ref[...]) * (1.0 + sc_ref[0]) + sh_ref[0]
    t = _dot(h.astype(BF16), wd_ref[...])
    cq = _rms(t[:, :Q_RANK], gq_ref[...]).astype(BF16)
    ckv = _rms(t[:, Q_RANK:Q_RANK + KV_RANK], gkv_ref[...]).astype(BF16)
    cos = cos_ref[...]
    sin = sin_ref[...]
    o = Q_RANK + KV_RANK
    kr = t[:, o:o + LANES] * cos + t[:, o + LANES:o + 2 * LANES] * sin
    nt = (((1,), (1,)), ((), ()))
    qa = lax.dot_general(wqa_ref[...], cq, nt, preferred_element_type=F32)
    qb = lax.dot_general(wqb_ref[...], cq, nt, preferred_element_type=F32)
    kn = _dot(ckv, wk_ref[...])
    cost = cost_ref[...]
    sint = sint_ref[...]
    for hd in range(MLA_HEADS):
        sl = slice(hd * HEAD_PAD, (hd + 1) * HEAD_PAD)
        q_ref[0, sl, :] = ((qa[sl] * cost + qb[sl] * sint) * qscale).astype(BF16)
        k_ref[0, hd] = (kn[:, sl] + kr).astype(BF16)
    vt = lax.dot_general(wvt_ref[...], ckv, nt, preferred_element_type=F32)
    tm = vt.shape[1]
    vt = vt.reshape(MLA_HEADS, VT_ROWS, tm)
    ones = lax.broadcasted_iota(jnp.int32, vt.shape, 1) == ONES_ROW
    vt_ref[0, 0] = jnp.where(ones, 1.0, vt).astype(BF16)


def _qkv(x, sh, sc, g, w, cos_t, sin_t, tm):
    b, l, d = x.shape
    nt = l // tm
    hw = MLA_HEADS * HEAD_PAD
    qscale = LOG2E / math.sqrt(QK_NOPE + QK_ROPE)
    full = lambda a: pl.BlockSpec(a.shape, lambda bi, i: (0,) * a.ndim)
    return pl.pallas_call(
        functools.partial(_qkv_kernel, qscale=qscale),
        grid=(b, nt),
        in_specs=[
            pl.BlockSpec((1, tm, d), lambda bi, i: (bi, i, 0)),
            pl.BlockSpec((1, 1, d), lambda bi, i: (bi, 0, 0)),
            pl.BlockSpec((1, 1, d), lambda bi, i: (bi, 0, 0)),
            full(g), full(w["wd"]), full(w["gq"]), full(w["gkv"]), full(w["wqa"]), full(w["wqb"]),
            full(w["wk"]), full(w["wvt"]),
            pl.BlockSpec((tm, LANES), lambda bi, i: (i, 0)),
            pl.BlockSpec((tm, LANES), lambda bi, i: (i, 0)),
            pl.BlockSpec((LANES, tm), lambda bi, i: (0, i)),
            pl.BlockSpec((LANES, tm), lambda bi, i: (0, i)),
        ],
        out_specs=[
            pl.BlockSpec((1, hw, tm), lambda bi, i: (bi, 0, i)),
            pl.BlockSpec((1, MLA_HEADS, tm, HEAD_PAD), lambda bi, i: (bi, 0, i, 0)),
            pl.BlockSpec((1, 1, MLA_HEADS, VT_ROWS, tm), lambda bi, i: (bi, i, 0, 0, 0)),
        ],
        out_shape=[
            jax.ShapeDtypeStruct((b, hw, l), BF16),
            jax.ShapeDtypeStruct((b, MLA_HEADS, l, HEAD_PAD), BF16),
            jax.ShapeDtypeStruct((b, nt, MLA_HEADS, VT_ROWS, tm), BF16),
        ],
        compiler_params=_cparams("parallel", "parallel"),
        name="mla_qkv",
    )(x, sh, sc, g, w["wd"], w["gq"], w["gkv"], w["wqa"], w["wqb"], w["wk"], w["wvt"], cos_t, sin_t, cos_t.T, sin_t.T)


def _mla_weights(w_dq, g_q, w_uq, w_dkv, g_kv, w_ukv):
    d = w_dq.shape[0]
    hq = QK_NOPE + QK_ROPE
    half = QK_ROPE // 2
    w_rope = w_dkv[:, KV_RANK:]
    w_rope_sw = jnp.concatenate([w_rope[:, half:], w_rope[:, :half]], axis=1)
    zl = jnp.zeros((d, QK_NOPE), F32)
    zr = jnp.zeros((d, HEAD_PAD - hq), F32)
    wd = jnp.concatenate([w_dq, w_dkv[:, :KV_RANK], zl, w_rope, zr, zl, w_rope_sw, zr], axis=1)
    wq = w_uq.reshape(Q_RANK, MLA_HEADS, hq)
    zq = jnp.zeros((Q_RANK, MLA_HEADS, HEAD_PAD - hq), F32)
    wqa = jnp.concatenate([wq, zq], axis=2).reshape(Q_RANK, MLA_HEADS * HEAD_PAD)
    zn = jnp.zeros((Q_RANK, MLA_HEADS, QK_NOPE), F32)
    wqb = jnp.concatenate([zn, wq[:, :, QK_NOPE + half:], wq[:, :, QK_NOPE:QK_NOPE + half], zq],
                          axis=2).reshape(Q_RANK, MLA_HEADS * HEAD_PAD)
    wkv = w_ukv.reshape(KV_RANK, MLA_HEADS, QK_NOPE + V_DIM)
    zk = jnp.zeros((KV_RANK, MLA_HEADS, HEAD_PAD - QK_NOPE), F32)
    wk = jnp.concatenate([wkv[:, :, :QK_NOPE], zk], axis=2).reshape(KV_RANK, MLA_HEADS * HEAD_PAD)
    wv = jnp.transpose(wkv[:, :, QK_NOPE:], (1, 2, 0))
    wvt = jnp.concatenate([wv, jnp.zeros((MLA_HEADS, VT_ROWS - V_DIM, KV_RANK), F32)], axis=1)
    return {
        "wd": wd.astype(BF16), "gq": g_q.reshape(1, -1), "gkv": g_kv.reshape(1, -1),
        "wqa": wqa.T.astype(BF16), "wqb": wqb.T.astype(BF16), "wk": wk.astype(BF16),
        "wvt": wvt.reshape(MLA_HEADS * VT_ROWS, KV_RANK).astype(BF16),
    }


def _rope_tables(l):
    t = jnp.arange(l)
    row = (t // GRID_W).astype(F32)
    col = (t % GRID_W).astype(F32)
    n_freq = QK_ROPE // 4
    inv = ROPE_BASE ** (-jnp.arange(n_freq, dtype=F32) / n_freq)
    ang = jnp.concatenate([row[:, None] * inv, col[:, None] * inv], axis=-1)
    cos, sin = jnp.cos(ang), jnp.sin(ang)
    ones = jnp.ones((l, QK_NOPE), F32)
    zl = jnp.zeros((l, QK_NOPE), F32)
    zr = jnp.zeros((l, HEAD_PAD - QK_NOPE - QK_ROPE), F32)
    cos_t = jnp.concatenate([ones, cos, cos, zr], axis=1)
    sin_t = jnp.concatenate([zl, -sin, sin, zr], axis=1)
    return cos_t, sin_t


def _no_rope_tables(l):
    keep = jnp.concatenate([jnp.ones((l, QK_NOPE + QK_ROPE), F32),
                            jnp.zeros((l, HEAD_PAD - QK_NOPE - QK_ROPE), F32)], axis=1)
    return keep, jnp.zeros((l, HEAD_PAD), F32)


def _attn_ctx_kernel(q_ref, kc_ref, vtc_ref, o_ref):
    outs = []
    for hd in range(2):
        s = _dot(kc_ref[0, hd], q_ref[0, hd * HEAD_PAD:(hd + 1) * HEAD_PAD, :])
        p = jnp.exp2(s - jnp.max(s, axis=0, keepdims=True)).astype(BF16)
        acc = _dot(vtc_ref[0, 0, hd], p)
        outs.append(acc[:V_DIM] / acc[ONES_ROW:ONES_ROW + 1])
    o_ref[0] = jnp.concatenate(outs, axis=0).T.astype(BF16)


def _attn_kernel(q_ref, kc_ref, vtc_ref, k_ref, vt_ref, o_ref, *scr, n_lat_chunks, tk, unroll):
    heads = (0, 1)
    s_scr = [scr[0:2], scr[2:4]]
    p_scr = [scr[4:6], scr[6:8]]
    acc_scr = scr[8:10]
    tq = q_ref.shape[2]
    n_pos = n_lat_chunks + 1

    def stage_a(hd, slot, p):
        if isinstance(p, int) and p == 0:
            k = kc_ref[0, hd]
        else:
            start = (p - 1) * tk
            k = k_ref[0, hd, pl.ds(start if isinstance(p, int) else pl.multiple_of(start, tk), tk), :]
        s = _dot(k, q_ref[0, hd * HEAD_PAD:(hd + 1) * HEAD_PAD, :])
        s_scr[hd][slot][...] = s
        return jnp.max(s.reshape(tk // SUBLANES, SUBLANES, tq), axis=0)

    def stage_b(hd, slot, cm, m_old):
        m_new = jnp.maximum(m_old, jnp.max(cm, axis=0, keepdims=True))
        p_scr[hd][slot][...] = jnp.exp2(s_scr[hd][slot][...] - m_new).astype(BF16)
        return m_new, jnp.exp2(m_old - m_new)

    def stage_c(hd, slot, p, alpha):
        vt = vtc_ref[0, 0, hd] if isinstance(p, int) and p == 0 else vt_ref[0, p - 1, hd]
        acc_scr[hd][...] = acc_scr[hd][...] * alpha + _dot(vt, p_scr[hd][slot][...])

    carry = []
    for hd in heads:
        acc_scr[hd][...] = jnp.zeros(acc_scr[hd].shape, F32)
        cm0 = stage_a(hd, 0, 0)
        cm1 = stage_a(hd, 1, 1)
        m, al = stage_b(hd, 0, cm0, jnp.full((1, tq), -jnp.inf, F32))
        carry += [m, al, cm1]

    def steps(carry, p0, parity, count):
        carry = list(carry)
        for u in range(count):
            slot = (parity + u) % 2
            for hd in heads:
                m, al, cm = carry[3 * hd:3 * hd + 3]
                stage_c(hd, slot, p0 + u, al)
                m, al = stage_b(hd, 1 - slot, cm, m)
                cm = stage_a(hd, slot, p0 + u + 2)
                carry[3 * hd:3 * hd + 3] = [m, al, cm]
        return tuple(carry)

    n_steps = n_pos - 2
    peel = n_steps % unroll or min(unroll, n_steps)
    carry = steps(carry, 0, 0, peel)
    carry = lax.fori_loop(0, (n_steps - peel) // unroll,
                          lambda j, cr: steps(cr, peel + unroll * j, peel % 2, unroll), carry)
    outs = []
    last = (n_pos - 2) % 2
    for hd in heads:
        m, al, cm = carry[3 * hd:3 * hd + 3]
        stage_c(hd, last, n_pos - 2, al)
        m, al = stage_b(hd, 1 - last, cm, m)
        stage_c(hd, 1 - last, n_pos - 1, al)
        acc = acc_scr[hd][...]
        outs.append(acc[:V_DIM] / acc[ONES_ROW:ONES_ROW + 1])
    o_ref[0] = jnp.concatenate(outs, axis=0).T.astype(BF16)


def _attention(q, kc, vtc, k=None, vt=None, *, tq):
    b, _, lq = q.shape
    cl = kc.shape[2]
    nq = lq // tq
    hp = 2 * HEAD_PAD
    in_specs = [
        pl.BlockSpec((1, hp, tq), lambda bi, h, i: (bi, h, i)),
        pl.BlockSpec((1, 2, cl, HEAD_PAD), lambda bi, h, i: (bi, h, 0, 0)),
        pl.BlockSpec((1, 1, 2, VT_ROWS, cl), lambda bi, h, i: (bi, 0, h, 0, 0)),
    ]
    args = [q, kc, vtc]
    n_chunks = 0
    scratch = []
    body = _attn_ctx_kernel
    if k is not None:
        lk = k.shape[2]
        n_chunks, tk = vt.shape[1], vt.shape[4]
        assert n_chunks >= 2 and cl == tk
        unroll = max(2, min(ATTN_UNROLL, (n_chunks - 2) // 2 * 2))
        body = functools.partial(_attn_kernel, n_lat_chunks=n_chunks, tk=tk, unroll=unroll)
        in_specs += [
            pl.BlockSpec((1, 2, lk, HEAD_PAD), lambda bi, h, i: (bi, h, 0, 0)),
            pl.BlockSpec((1, n_chunks, 2, VT_ROWS, tk), lambda bi, h, i: (bi, 0, h, 0, 0)),
        ]
        args += [k, vt]
        scratch = ([pltpu.VMEM((tk, tq), F32)] * 4 + [pltpu.VMEM((tk, tq), BF16)] * 4
                   + [pltpu.VMEM((VT_ROWS, tq), F32)] * 2)
    return pl.pallas_call(
        body,
        grid=(b, MLA_HEADS // 2, nq),
        in_specs=in_specs,
        out_specs=pl.BlockSpec((1, tq, 2 * V_DIM), lambda bi, h, i: (bi, i, h)),
        out_shape=jax.ShapeDtypeStruct((b, lq, MLA_HEADS * V_DIM), BF16),
        scratch_shapes=scratch,
        compiler_params=_cparams("parallel", "parallel", "arbitrary"),
        name="attention" if n_chunks else "attention_ctx",
    )(*args)


def _mlp_kernel(a_ref, wo_ref, bo_ref, gt0_ref, g0_ref, x_ref, sh_ref, sc_ref, gt_ref, g1_ref, g2_ref,
                w1_ref, w2_ref, o_ref):
    y = _dot(a_ref[0], wo_ref[...]) + bo_ref[...]
    x = x_ref[0] + gt0_ref[0] * _rms(y, g0_ref[...])
    h = (_rms(x, g1_ref[...]) * (1.0 + sc_ref[0]) + sh_ref[0]).astype(BF16)
    dff = w1_ref.shape[1]
    m = None
    for c in range(dff // MLP_FF_CHUNK):
        cols = slice(c * MLP_FF_CHUNK, (c + 1) * MLP_FF_CHUNK)
        u = jnp.maximum(_dot(h, w1_ref[:, cols]), 0.0)
        part = _dot((u * u).astype(BF16), w2_ref[cols, :])
        m = part if m is None else m + part
    o_ref[0] = x + gt_ref[0] * _rms(m, g2_ref[...])


def _mixer_out_mlp(a, wo, bo, gt0, g0, x, sh, sc, gt, g1, g2, w1, w2, tm):
    b, l, d = x.shape
    din = a.shape[2]
    dff = w1.shape[1]
    vec = pl.BlockSpec((1, 1, d), lambda bi, i: (bi, 0, 0))
    row = pl.BlockSpec((1, d), lambda bi, i: (0, 0))
    const = lambda shape: pl.BlockSpec(shape, lambda bi, i: (0, 0), pipeline_mode=pl.Buffered(1))
    return pl.pallas_call(
        _mlp_kernel,
        grid=(b, l // tm),
        in_specs=[
            pl.BlockSpec((1, tm, din), lambda bi, i: (bi, i, 0)),
            const((din, d)), row, vec, row,
            pl.BlockSpec((1, tm, d), lambda bi, i: (bi, i, 0)),
            vec, vec, vec, row, row,
            const((d, dff)), const((dff, d)),
        ],
        out_specs=pl.BlockSpec((1, tm, d), lambda bi, i: (bi, i, 0)),
        out_shape=jax.ShapeDtypeStruct((b, l, d), F32),
        compiler_params=_cparams("parallel", "parallel"),
        name="mixer_out_mlp",
    )(a, wo, bo, gt0, g0, x, sh, sc, gt, g1, g2, w1, w2)


def _hy_in_kernel(x_ref, xp_ref, xn_ref, sh_ref, sc_ref, g_ref, w_ref, b_ref, ws_ref, bs_ref,
                  x1_ref, x2_ref, v_ref, u_scr):
    i = pl.program_id(1)
    n = pl.num_programs(1)
    tm = x_ref.shape[1]
    d = x_ref.shape[2]
    xa = jnp.concatenate([xp_ref[0], x_ref[0], xn_ref[0]], axis=0)
    h = (_rms(xa, g_ref[...]) * (1.0 + sc_ref[0]) + sh_ref[0]).astype(BF16)
    u_scr[...] = _dot(h, w_ref[...]) + b_ref[...]

    @pl.when(i == 0)
    def _():
        u_scr[0:HALO, :] = jnp.zeros((HALO, 3 * d), F32)

    @pl.when(i == n - 1)
    def _():
        u_scr[tm + HALO:tm + 2 * HALO, :] = jnp.zeros((HALO, 3 * d), F32)

    rows = tm + 2 * HALO
    for j, o_ref in enumerate((x1_ref, x2_ref, v_ref)):
        sl = slice(j * d, (j + 1) * d)
        u = u_scr[:, sl]
        prev = pltpu.roll(u, 1, axis=0)[HALO:HALO + tm]
        nxt = pltpu.roll(u, rows - 1, axis=0)[HALO:HALO + tm]
        y = (prev * ws_ref[0:1, sl] + u[HALO:HALO + tm] * ws_ref[1:2, sl] + nxt * ws_ref[2:3, sl]
             + bs_ref[:, sl])
        o_ref[0] = y.astype(BF16)


def _hy_in(x, sh, sc, g, w_in, b_in, w_short, b_short, tm):
    b, l, d = x.shape
    nt = l // tm
    tb = tm // HALO
    nb = l // HALO
    vec = pl.BlockSpec((1, 1, d), lambda bi, i: (bi, 0, 0))
    out = pl.BlockSpec((1, tm, d), lambda bi, i: (bi, i, 0))
    return pl.pallas_call(
        _hy_in_kernel,
        grid=(b, nt),
        in_specs=[
            pl.BlockSpec((1, tm, d), lambda bi, i: (bi, i, 0)),
            pl.BlockSpec((1, HALO, d), lambda bi, i: (bi, jnp.maximum(i * tb - 1, 0), 0)),
            pl.BlockSpec((1, HALO, d), lambda bi, i: (bi, jnp.minimum((i + 1) * tb, nb - 1), 0)),
            vec, vec,
            pl.BlockSpec((1, d), lambda bi, i: (0, 0)),
            pl.BlockSpec((d, 3 * d), lambda bi, i: (0, 0)),
            pl.BlockSpec((1, 3 * d), lambda bi, i: (0, 0)),
            pl.BlockSpec((3, 3 * d), lambda bi, i: (0, 0)),
            pl.BlockSpec((1, 3 * d), lambda bi, i: (0, 0)),
        ],
        out_specs=[out, out, out],
        out_shape=[jax.ShapeDtypeStruct((b, l, d), BF16)] * 3,
        scratch_shapes=[pltpu.VMEM((tm + 2 * HALO, 3 * d), F32)],
        compiler_params=_cparams("parallel", "parallel"),
        name="hyena_in",
    )(x, x, x, sh, sc, g, w_in, b_in, w_short, b_short)


def _filt_kernel(z_ref, w1_ref, b1_ref, w2_ref, b2_ref, w3_ref, fr_ref, dl_ref, o_ref, *, l):
    tl = z_ref.shape[0]
    d = dl_ref.shape[1]
    fr = fr_ref[...]
    hdn = jnp.sin(fr * (_dot3(z_ref[...], w1_ref[...]) + b1_ref[...]))
    hdn = jnp.sin(fr * (_dot3(hdn, w2_ref[...]) + b2_ref[...]))
    hf = _dot3(hdn, w3_ref[...])
    rows = pl.program_id(0) * tl + lax.broadcasted_iota(jnp.int32, (tl, d), 0)
    t = rows.astype(F32) * (1.0 / (l - 1))
    decay = jnp.exp(-t * jnp.abs(dl_ref[...]))
    for j in range(2 * HY_ORDER):
        f = hf[:, j * d:(j + 1) * d] * decay
        if j % 2 == 1:
            f = jnp.where(rows == 0, 0.0, f)
        o_ref[j] = f.astype(BF16)


def _filters(l, d, f_w1, f_b1, f_w2, f_b2, f_w3, f_freq):
    t = jnp.linspace(0.0, 1.0, l, dtype=F32)[:, None]
    w = (2.0 * math.pi) * jnp.arange(l, dtype=F32)[:, None] / l
    f = jnp.linspace(1e-4, HY_BANDS - 1, HY_BANDS, dtype=F32)[None, :]
    kz = HY_FILTER_HIDDEN
    z = jnp.concatenate([t, jnp.cos(f * w), -jnp.sin(f * w), jnp.zeros((l, kz - HY_EMB), F32)], axis=-1)
    w1 = jnp.concatenate([f_w1, jnp.zeros((kz - HY_EMB, HY_FILTER_HIDDEN), F32)], axis=0)
    min_decay = math.log(HY_TARGET) / HY_SLOW_PCT
    max_decay = math.log(HY_TARGET) / HY_FAST_PCT
    deltas = jnp.linspace(min_decay, max_decay, d, dtype=F32)[None, :]
    tl = FILTER_ROWS
    full = lambda a: pl.BlockSpec(a.shape, lambda i: (0,) * a.ndim)
    ops = [w1, f_b1.reshape(1, -1), f_w2, f_b2.reshape(1, -1), f_w3, f_freq.reshape(1, -1), deltas]
    return pl.pallas_call(
        functools.partial(_filt_kernel, l=l),
        grid=(l // tl,),
        in_specs=[pl.BlockSpec((tl, kz), lambda i: (i, 0))] + [full(a) for a in ops],
        out_specs=pl.BlockSpec((2 * HY_ORDER, tl, d), lambda i: (0, i, 0)),
        out_shape=jax.ShapeDtypeStruct((2 * HY_ORDER, l, d), BF16),
        compiler_params=_cparams("parallel"),
        name="hyena_filters",
    )(z, *ops)


def _dft_tables(l):
    n = 2 * l
    n2 = DFT_N2
    n1 = n // n2
    n1h = n1 // 2
    slots = -(-(n1h + 1) // SLOT_BLOCK) * SLOT_BLOCK
    k1 = jnp.arange(slots)
    valid = (k1 <= n1h)[:, None]
    a = jnp.arange(n1h)
    ang1 = (2.0 * math.pi / n1) * ((k1[:, None] * a[None, :]) % n1).astype(F32)
    f1 = jnp.concatenate([jnp.where(valid, jnp.cos(ang1), 0.0), jnp.where(valid, -jnp.sin(ang1), 0.0)], axis=0)
    kk = jnp.arange(n1h)
    wgt = jnp.where(kk == 0, 1.0, 2.0)[None, :] / n
    ang2 = (2.0 * math.pi / n1) * ((a[:, None] * kk[None, :]) % n1).astype(F32)
    f2 = jnp.concatenate([wgt * jnp.cos(ang2), -wgt * jnp.sin(ang2)], axis=1)
    k2 = jnp.arange(n2)
    j = jnp.arange(n2)
    prod = (j[None, None, :] * (k1[:, None, None] + n1 * k2[None, :, None])) % n
    th = (2.0 * math.pi / n) * prod.astype(F32)
    live = valid[:, :, None]
    c, s = jnp.where(live, jnp.cos(th), 0.0), jnp.where(live, jnp.sin(th), 0.0)
    g = jnp.concatenate([jnp.concatenate([c, s], axis=2), jnp.concatenate([-s, c], axis=2)], axis=1)
    ct, st = jnp.swapaxes(c, 1, 2), jnp.swapaxes(s, 1, 2)
    gi = jnp.concatenate([jnp.concatenate([ct, -st], axis=2), jnp.concatenate([st, ct], axis=2)], axis=1)
    return {"n": n, "n1h": n1h, "slots": slots,
            "f1": f1.astype(BF16), "f2": f2.astype(BF16), "g": g.astype(BF16), "gi": gi.astype(BF16)}


def _swap_rows(x):
    return pltpu.einshape("abc->bac", x)


def _stage1_kernel(f_ref, u_ref, a_ref, xs_scr):
    jb = pl.program_id(2)
    n2, n1h, td = xs_scr.shape

    @pl.when(jb == 0)
    def _():
        xs_scr[...] = _swap_rows(u_ref[0].reshape(n1h, n2, td))

    for jj in range(J_BLOCK):
        a_ref[0, jj] = _dot(f_ref[...], xs_scr[jb * J_BLOCK + jj]).astype(BF16)


def _stage1(u, tab):
    b, l, d = u.shape
    n1h, slots = tab["n1h"], tab["slots"]
    n2 = DFT_N2
    td = STAGE1_TD
    return pl.pallas_call(
        _stage1_kernel,
        grid=(b, d // td, n2 // J_BLOCK),
        in_specs=[
            pl.BlockSpec((2 * slots, n1h), lambda bi, ci, jb: (0, 0)),
            pl.BlockSpec((1, l, td), lambda bi, ci, jb: (bi, 0, ci)),
        ],
        out_specs=pl.BlockSpec((1, J_BLOCK, 2 * slots, td), lambda bi, ci, jb: (bi, jb, 0, ci)),
        out_shape=jax.ShapeDtypeStruct((b, n2, 2 * slots, d), BF16),
        scratch_shapes=[pltpu.VMEM((n2, n1h, td), BF16)],
        compiler_params=_cparams("parallel", "parallel", "arbitrary"),
        name="dft_stage1",
    )(tab["f1"], u)


def _slot_rows(ref):
    n2, _, sb, td = ref.shape
    return _swap_rows(ref[...].reshape(n2, 2 * sb, td))


def _filt_spec_kernel(g_ref, af_ref, ab_ref, bias_ref, k_ref, f_scr, b_scr):
    n2 = f_scr.shape[1]
    f_scr[...] = _slot_rows(af_ref.at[0])
    b_scr[...] = _slot_rows(ab_ref.at[0])
    bias = bias_ref[0]

    def body(s, carry):
        g = g_ref[s]
        xf = _dot(g, jnp.concatenate([f_scr[s], f_scr[SLOT_BLOCK + s]], axis=0))
        xb = _dot(g, jnp.concatenate([b_scr[s], b_scr[SLOT_BLOCK + s]], axis=0))
        k_ref[0, 0, s] = (xf[:n2] + xb[:n2] + bias).astype(BF16)
        k_ref[0, 1, s] = (xf[n2:] - xb[n2:]).astype(BF16)
        return carry

    lax.fori_loop(0, SLOT_BLOCK, body, 0, unroll=4)


def _filt_spectrum(a_filt, bias, tab):
    _, n2, _, d = a_filt.shape
    slots = tab["slots"]
    td = MID_TD
    a5 = a_filt.reshape(a_filt.shape[0], n2, 2, slots, d)
    blk = lambda seq: pl.BlockSpec((1, n2, 2, SLOT_BLOCK, td), lambda o, kb, ci: (2 * o + seq, 0, 0, kb, ci))
    return pl.pallas_call(
        _filt_spec_kernel,
        grid=(HY_ORDER, slots // SLOT_BLOCK, d // td),
        in_specs=[
            pl.BlockSpec((SLOT_BLOCK, 2 * n2, 2 * n2), lambda o, kb, ci: (kb, 0, 0)),
            blk(0), blk(1),
            pl.BlockSpec((1, 1, td), lambda o, kb, ci: (o, 0, ci)),
        ],
        out_specs=pl.BlockSpec((1, 2, SLOT_BLOCK, n2, td), lambda o, kb, ci: (o, 0, kb, 0, ci)),
        out_shape=jax.ShapeDtypeStruct((HY_ORDER, 2, slots, n2, d), BF16),
        scratch_shapes=[pltpu.VMEM((2 * SLOT_BLOCK, n2, td), BF16)] * 2,
        compiler_params=_cparams("parallel", "parallel", "parallel"),
        name="hyena_filter_spectrum",
    )(tab["g"], a5, a5, bias.reshape(HY_ORDER, 1, d))


def _mid_kernel(g_ref, gi_ref, a_ref, k_ref, z_ref, a_scr, z_scr):
    n2 = a_scr.shape[1]
    a_scr[...] = _slot_rows(a_ref.at[0])

    def body(s, carry):
        x = _dot(g_ref[s], jnp.concatenate([a_scr[s], a_scr[SLOT_BLOCK + s]], axis=0))
        xr, xi = x[:n2], x[n2:]
        kr = k_ref[0, 0, s].astype(F32)
        ki = k_ref[0, 1, s].astype(F32)
        y = jnp.concatenate([xr * kr - xi * ki, xr * ki + xi * kr], axis=0).astype(BF16)
        z = _dot(gi_ref[s], y)
        z_scr[s] = z[:n2].astype(BF16)
        z_scr[SLOT_BLOCK + s] = z[n2:].astype(BF16)
        return carry

    lax.fori_loop(0, SLOT_BLOCK, body, 0, unroll=4)
    td = z_scr.shape[2]
    z_ref[0] = _swap_rows(z_scr[...]).reshape(n2, 2, SLOT_BLOCK, td)


def _mid(a, kf, order, tab):
    b, n2, _, d = a.shape
    slots = tab["slots"]
    td = MID_TD
    a5 = a.reshape(b, n2, 2, slots, d)
    gspec = pl.BlockSpec((SLOT_BLOCK, 2 * n2, 2 * n2), lambda bi, kb, ci: (kb, 0, 0))
    aspec = pl.BlockSpec((1, n2, 2, SLOT_BLOCK, td), lambda bi, kb, ci: (bi, 0, 0, kb, ci))
    z = pl.pallas_call(
        _mid_kernel,
        grid=(b, slots // SLOT_BLOCK, d // td),
        in_specs=[
            gspec, gspec, aspec,
            pl.BlockSpec((1, 2, SLOT_BLOCK, n2, td), lambda bi, kb, ci: (order, 0, kb, 0, ci)),
        ],
        out_specs=aspec,
        out_shape=jax.ShapeDtypeStruct((b, n2, 2, slots, d), BF16),
        scratch_shapes=[pltpu.VMEM((2 * SLOT_BLOCK, n2, td), BF16)] * 2,
        compiler_params=_cparams("parallel", "parallel", "parallel"),
        name="dft_mid",
    )(tab["g"], tab["gi"], a5, kf)
    return z.reshape(b, n2, 2 * slots, d)


def _stage2_kernel(f_ref, z_ref, x_ref, o_ref, ys_scr, *, slots, inv_n):
    jb = pl.program_id(2)
    n2, n1h, td = ys_scr.shape
    rows = lax.broadcasted_iota(jnp.int32, (n1h, td), 0)
    for jj in range(J_BLOCK):
        z = z_ref[0, jj]
        y = _dot(f_ref[...], jnp.concatenate([z[0:n1h], z[slots:slots + n1h]], axis=0))
        nyq = z[n1h:n1h + 1].astype(F32) * inv_n
        ys_scr[jb * J_BLOCK + jj] = (y + jnp.where(rows % 2 == 0, nyq, -nyq)).astype(BF16)

    @pl.when(jb == pl.num_programs(2) - 1)
    def _():
        o_ref[0] = _swap_rows(ys_scr[...]).reshape(n1h * n2, td) * x_ref[0]


def _stage2(z, gate, tab):
    b, n2, _, d = z.shape
    n1h, slots = tab["n1h"], tab["slots"]
    l = n1h * n2
    td = STAGE2_TD
    return pl.pallas_call(
        functools.partial(_stage2_kernel, slots=slots, inv_n=1.0 / tab["n"]),
        grid=(b, d // td, n2 // J_BLOCK),
        in_specs=[
            pl.BlockSpec((n1h, 2 * n1h), lambda bi, ci, jb: (0, 0)),
            pl.BlockSpec((1, J_BLOCK, 2 * slots, td), lambda bi, ci, jb: (bi, jb, 0, ci)),
            pl.BlockSpec((1, l, td), lambda bi, ci, jb: (bi, 0, ci)),
        ],
        out_specs=pl.BlockSpec((1, l, td), lambda bi, ci, jb: (bi, 0, ci)),
        out_shape=jax.ShapeDtypeStruct((b, l, d), BF16),
        scratch_shapes=[pltpu.VMEM((n2, n1h, td), BF16)],
        compiler_params=_cparams("parallel", "parallel", "arbitrary"),
        name="dft_stage2",
    )(tab["f2"], z, gate)


def _small_dft_tables(l):
    n = 2 * l
    k = jnp.arange(l)
    t = jnp.arange(l)
    ang = (2.0 * math.pi / n) * ((k[:, None] * t[None, :]) % n).astype(F32)
    sign = jnp.where(t % 2 == 0, 1.0, -1.0)[None, :]
    pad = jnp.zeros((SUBLANES - 1, l), F32)
    fwd = jnp.concatenate([jnp.cos(ang), -jnp.sin(ang), sign, pad], axis=0)
    wgt = jnp.where(k == 0, 1.0, 2.0)[None, :] / n
    ang_t = ang.T
    inv = jnp.concatenate([wgt * jnp.cos(ang_t), -wgt * jnp.sin(ang_t)], axis=1)
    return {"n": n, "fwd": fwd.astype(BF16), "inv": inv.astype(BF16)}


def _small_spec_kernel(f_ref, hf_ref, hb_ref, bias_ref, k_ref):
    l = hf_ref.shape[1]
    xf = _dot(f_ref[...], hf_ref[0])
    xb = _dot(f_ref[...], hb_ref[0])
    bias = bias_ref[0]
    k_ref[0, 0:l, :] = (xf[:l] + xb[:l] + bias).astype(BF16)
    k_ref[0, l:2 * l, :] = (xf[l:2 * l] - xb[l:2 * l]).astype(BF16)
    k_ref[0, 2 * l:, :] = (xf[2 * l:] + xb[2 * l:] + bias).astype(BF16)


def _small_filt_spectrum(filt, bias, tab):
    _, l, d = filt.shape
    rows = 2 * l + SUBLANES
    tc = SMALL_TD
    return pl.pallas_call(
        _small_spec_kernel,
        grid=(HY_ORDER, d // tc),
        in_specs=[
            pl.BlockSpec((rows, l), lambda o, j: (0, 0)),
            pl.BlockSpec((1, l, tc), lambda o, j: (2 * o, 0, j)),
            pl.BlockSpec((1, l, tc), lambda o, j: (2 * o + 1, 0, j)),
            pl.BlockSpec((1, 1, tc), lambda o, j: (o, 0, j)),
        ],
        out_specs=pl.BlockSpec((1, rows, tc), lambda o, j: (o, 0, j)),
        out_shape=jax.ShapeDtypeStruct((HY_ORDER, rows, d), BF16),
        compiler_params=_cparams("parallel", "parallel"),
        name="hyena_filter_spectrum_ctx",
    )(tab["fwd"], filt, filt, bias.reshape(HY_ORDER, 1, d))


def _small_conv_kernel(f_ref, fi_ref, u_ref, k_ref, x_ref, o_ref, *, inv_n):
    l = u_ref.shape[1]
    x = _dot(f_ref[...], u_ref[0])
    xr, xi = x[:l], x[l:2 * l]
    kr = k_ref[0, 0:l, :].astype(F32)
    ki = k_ref[0, l:2 * l, :].astype(F32)
    y = jnp.concatenate([xr * kr - xi * ki, xr * ki + xi * kr], axis=0).astype(BF16)
    out = _dot(fi_ref[...], y)
    nyq = x[2 * l:2 * l + 1] * k_ref[0, 2 * l:2 * l + 1, :].astype(F32) * inv_n
    rows = lax.broadcasted_iota(jnp.int32, out.shape, 0)
    out = out + jnp.where(rows % 2 == 0, nyq, -nyq)
    o_ref[0] = (out * x_ref[0].astype(F32)).astype(BF16)


def _small_conv(u, kf, order, gate, tab):
    b, l, d = u.shape
    rows = 2 * l + SUBLANES
    tc = SMALL_TD
    return pl.pallas_call(
        functools.partial(_small_conv_kernel, inv_n=1.0 / tab["n"]),
        grid=(b, d // tc),
        in_specs=[
            pl.BlockSpec((rows, l), lambda bi, j: (0, 0)),
            pl.BlockSpec((l, 2 * l), lambda bi, j: (0, 0)),
            pl.BlockSpec((1, l, tc), lambda bi, j: (bi, 0, j)),
            pl.BlockSpec((1, rows, tc), lambda bi, j: (order, 0, j)),
            pl.BlockSpec((1, l, tc), lambda bi, j: (bi, 0, j)),
        ],
        out_specs=pl.BlockSpec((1, l, tc), lambda bi, j: (bi, 0, j)),
        out_shape=jax.ShapeDtypeStruct((b, l, d), BF16),
        compiler_params=_cparams("parallel", "parallel"),
        name="long_conv_ctx",
    )(tab["fwd"], tab["inv"], u, kf, gate)


def _tile(l, want):
    while l % want:
        want //= 2
    return want


def _hyena_mixer_lat(x, sh, sc, g, p, tabs):
    b, l, d = x.shape
    x1, x2, v = _hy_in(x, sh, sc, g, p["w_in"], p["b_in"], p["w_short"], p["b_short"], _tile(l, ROW_TILE))
    filt = _filters(l, d, *p["filter"])
    kf = _filt_spectrum(_stage1(filt, tabs), p["bias"], tabs)
    z = _stage2(_mid(_stage1(v, tabs), kf, 0, tabs), x1, tabs)
    z = _stage2(_mid(_stage1(z, tabs), kf, 1, tabs), x2, tabs)
    return z


def _hyena_mixer_ctx(x, sh, sc, g, p, tabs):
    b, l, d = x.shape
    x1, x2, v = _hy_in(x, sh, sc, g, p["w_in"], p["b_in"], p["w_short"], p["b_short"], l)
    filt = _filters(l, d, *p["filter"])
    kf = _small_filt_spectrum(filt, p["bias"], tabs)
    z = _small_conv(v, kf, 0, x1, tabs)
    z = _small_conv(z, kf, 1, x2, tabs)
    return z


def kernel(x, c, ctx, c_ctx, ada_w, ada_b, norm_g, mla_w_dq, mla_g_q, mla_w_uq, mla_w_dkv, mla_g_kv,
           mla_w_ukv, mla_w_o, hy_w_in, hy_b_in, hy_w_short, hy_b_short, hy_f_w1, hy_f_b1, hy_f_w2,
           hy_f_b2, hy_f_w3, hy_f_freq, hy_bias, hy_w_out, hy_b_out, mlp_w1, mlp_w2):
    b, l, d = x.shape
    cl = ctx.shape[1]
    depth = ada_w.shape[0]
    assert b + 1 <= SUBLANES and l % ATTN_TK == 0 and cl == ATTN_TK

    cvec = jnp.concatenate([c, c_ctx[None, :], jnp.zeros((SUBLANES - b - 1, d), F32)], axis=0)
    mods = _ada(cvec, ada_w, ada_b)

    rope_lat = _rope_tables(l)
    rope_ctx = _no_rope_tables(cl)
    tabs_lat = _dft_tables(l)
    tabs_ctx = _small_dft_tables(cl)
    zero_bias = jnp.zeros((1, d), F32)

    xc = ctx
    for i in range(depth):
        last = i == depth - 1
        j = i // 2
        g = norm_g[i].reshape(4, 1, d)
        m_lat = mods[i, :b].reshape(b, 6, 1, d)
        m_ctx = jnp.broadcast_to(mods[i, b].reshape(1, 6, 1, d), (b, 6, 1, d))
        lat = [m_lat[:, k] for k in range(6)]
        cx = [m_ctx[:, k] for k in range(6)]

        if i % 2 == 0:
            w = _mla_weights(mla_w_dq[j], mla_g_q[j], mla_w_uq[j], mla_w_dkv[j], mla_g_kv[j], mla_w_ukv[j])
            wo = mla_w_o[j].astype(BF16)
            bo = zero_bias
            qc, kc, vtc = _qkv(xc, cx[0], cx[1], g[0], w, *rope_ctx, cl)
            ql, kl, vtl = _qkv(x, lat[0], lat[1], g[0], w, *rope_lat, ATTN_TK)
            y_lat = _attention(ql, kc, vtc, kl, vtl, tq=_tile(l, ATTN_TQ))
            y_ctx = None if last else _attention(qc, kc, vtc, tq=cl)
        else:
            p = {
                "w_in": hy_w_in[j].astype(BF16), "b_in": hy_b_in[j].reshape(1, -1),
                "w_short": hy_w_short[j], "b_short": hy_b_short[j].reshape(1, -1),
                "filter": (hy_f_w1[j], hy_f_b1[j], hy_f_w2[j], hy_f_b2[j], hy_f_w3[j], hy_f_freq[j]),
                "bias": hy_bias[j],
            }
            wo = hy_w_out[j].astype(BF16)
            bo = hy_b_out[j].reshape(1, d)
            y_lat = _hyena_mixer_lat(x, lat[0], lat[1], g[0], p, tabs_lat)
            y_ctx = None if last else _hyena_mixer_ctx(xc, cx[0], cx[1], g[0], p, tabs_ctx)

        w1 = mlp_w1[i].astype(BF16)
        w2 = mlp_w2[i].astype(BF16)
        x = _mixer_out_mlp(y_lat, wo, bo, lat[2], g[1], x, lat[3], lat[4], lat[5], g[2], g[3], w1, w2,
                           _tile(l, MLP_ROWS))
        if not last:
            xc = _mixer_out_mlp(y_ctx, wo, bo, cx[2], g[1], xc, cx[3], cx[4], cx[5], g[2], g[3], w1, w2, cl)
    return x
```

```python
import functools
import math

import jax
import jax.numpy as jnp
from jax import lax
from jax.experimental import pallas as pl
from jax.experimental.pallas import tpu as pltpu

F32 = jnp.float32
BF16 = jnp.bfloat16

GRID_W = 64
MLA_HEADS = 16
QK_NOPE = 64
QK_ROPE = 32
V_DIM = 64
Q_RANK = 256
KV_RANK = 128
ROPE_BASE = 10000.0
HY_ORDER = 2
HY_EMB = 33
HY_BANDS = (HY_EMB - 1) // 2
HY_FILTER_HIDDEN = 64
HY_TARGET = 1e-2
HY_FAST_PCT = 0.3
HY_SLOW_PCT = 1.5
EPS = 1e-6

LANES = 128
SUBLANES = 8
HEAD_PAD = LANES
VT_ROWS = 80
ONES_ROW = V_DIM
VMEM_LIMIT = 56 * 1024 * 1024

ADA_TN = 1536
ATTN_TQ = 512
ATTN_TK = 256
ATTN_UNROLL = 10
MLP_ROWS = 1024
MLP_FF_CHUNK = 1024
ROW_TILE = 1024
HALO = SUBLANES
FILTER_ROWS = 256
DFT_N2 = 128
SLOT_BLOCK = 16
J_BLOCK = 32
STAGE1_TD = 256
STAGE2_TD = 256
MID_TD = 512
SMALL_TD = 512

LOG2E = 1.4426950408889634


def _cparams(*sem):
    return pltpu.CompilerParams(dimension_semantics=sem, vmem_limit_bytes=VMEM_LIMIT)


def _dot(a, b):
    return jnp.dot(a, b, preferred_element_type=F32)


def _dot3(a, b):
    ah = a.astype(BF16)
    al = (a - ah.astype(F32)).astype(BF16)
    bh = b.astype(BF16)
    bl = (b - bh.astype(F32)).astype(BF16)
    return _dot(ah, bh) + _dot(al, bh) + _dot(ah, bl)


def _rms(x, g):
    return x * lax.rsqrt(jnp.mean(x * x, axis=-1, keepdims=True) + EPS) * g


def _ada_kernel(c_ref, w_ref, b_ref, o_ref):
    c = c_ref[...]
    s = c / (1.0 + jnp.exp(-c))
    o_ref[0] = _dot3(s, w_ref[0]) + b_ref[0]


def _ada(cvec, ada_w, ada_b):
    depth, d, n6 = ada_w.shape
    rows = cvec.shape[0]
    return pl.pallas_call(
        _ada_kernel,
        grid=(depth, n6 // ADA_TN),
        in_specs=[
            pl.BlockSpec((rows, d), lambda i, j: (0, 0)),
            pl.BlockSpec((1, d, ADA_TN), lambda i, j: (i, 0, j)),
            pl.BlockSpec((1, 1, ADA_TN), lambda i, j: (i, 0, j)),
        ],
        out_specs=pl.BlockSpec((1, rows, ADA_TN), lambda i, j: (i, 0, j)),
        out_shape=jax.ShapeDtypeStruct((depth, rows, n6), F32),
        compiler_params=_cparams("parallel", "parallel"),
        name="ada",
    )(cvec, ada_w, ada_b.reshape(depth, 1, n6))


def _qkv_kernel(x_ref, sh_ref, sc_ref, g_ref, wd_ref, gq_ref, gkv_ref, wqa_ref, wqb_ref, wk_ref,
                wvt_ref, cos_ref, sin_ref, cost_ref, sint_ref, q_ref, k_ref, vt_ref, *, qscale):
    h = _rms(x_ref[0], g_ref[...]) * (1.0 + sc_ref[0]) + sh_ref[0]
    t = _dot(h.astype(BF16), wd_ref[...])
    cq = _rms(t[:, :Q_RANK], gq_ref[...]).astype(BF16)
    ckv = _rms(t[:, Q_RANK:Q_RANK + KV_RANK], gkv_ref[...]).astype(BF16)
    cos = cos_ref[...]
    sin = sin_ref[...]
    o = Q_RANK + KV_RANK
    kr = t[:, o:o + LANES] * cos + t[:, o + LANES:o + 2 * LANES] * sin
    nt = (((1,), (1,)), ((), ()))
    qa = lax.dot_general(wqa_ref[...], cq, nt, preferred_element_type=F32)
    qb = lax.dot_general(wqb_ref[...], cq, nt, preferred_element_type=F32)
    kn = _dot(ckv, wk_ref[...])
    cost = cost_ref[...]
    sint = sint_ref[...]
    for hd in range(MLA_HEADS):
        sl = slice(hd * HEAD_PAD, (hd + 1) * HEAD_PAD)
        q_ref[0, sl, :] = ((qa[sl] * cost + qb[sl] * sint) * qscale).astype(BF16)
        k_ref[0, hd] = (kn[:, sl] + kr).astype(BF16)
    vt = lax.dot_general(wvt_ref[...], ckv, nt, preferred_element_type=F32)
    tm = vt.shape[1]
    vt = vt.reshape(MLA_HEADS, VT_ROWS, tm)
    ones = lax.broadcasted_iota(jnp.int32, vt.shape, 1) == ONES_ROW
    vt_ref[0, 0] = jnp.where(ones, 1.0, vt).astype(BF16)


def _qkv(x, sh, sc, g, w, cos_t, sin_t, tm):
    b, l, d = x.shape
    nt = l // tm
    hw = MLA_HEADS * HEAD_PAD
    qscale = LOG2E / math.sqrt(QK_NOPE + QK_ROPE)
    full = lambda a: pl.BlockSpec(a.shape, lambda bi, i: (0,) * a.ndim)
    return pl.pallas_call(
        functools.partial(_qkv_kernel, qscale=qscale),
        grid=(b, nt),
        in_specs=[
            pl.BlockSpec((1, tm, d), lambda bi, i: (bi, i, 0)),
            pl.BlockSpec((1, 1, d), lambda bi, i: (bi, 0, 0)),
            pl.BlockSpec((1, 1, d), lambda bi, i: (bi, 0, 0)),
            full(g), full(w["wd"]), full(w["gq"]), full(w["gkv"]), full(w["wqa"]), full(w["wqb"]),
            full(w["wk"]), full(w["wvt"]),
            pl.BlockSpec((tm, LANES), lambda bi, i: (i, 0)),
            pl.BlockSpec((tm, LANES), lambda bi, i: (i, 0)),
            pl.BlockSpec((LANES, tm), lambda bi, i: (0, i)),
            pl.BlockSpec((LANES, tm), lambda bi, i: (0, i)),
        ],
        out_specs=[
            pl.BlockSpec((1, hw, tm), lambda bi, i: (bi, 0, i)),
            pl.BlockSpec((1, MLA_HEADS, tm, HEAD_PAD), lambda bi, i: (bi, 0, i, 0)),
            pl.BlockSpec((1, 1, MLA_HEADS, VT_ROWS, tm), lambda bi, i: (bi, i, 0, 0, 0)),
        ],
        out_shape=[
            jax.ShapeDtypeStruct((b, hw, l), BF16),
            jax.ShapeDtypeStruct((b, MLA_HEADS, l, HEAD_PAD), BF16),
            jax.ShapeDtypeStruct((b, nt, MLA_HEADS, VT_ROWS, tm), BF16),
        ],
        compiler_params=_cparams("parallel", "parallel"),
        name="mla_qkv",
    )(x, sh, sc, g, w["wd"], w["gq"], w["gkv"], w["wqa"], w["wqb"], w["wk"], w["wvt"], cos_t, sin_t, cos_t.T, sin_t.T)


def _mla_weights(w_dq, g_q, w_uq, w_dkv, g_kv, w_ukv):
    d = w_dq.shape[0]
    hq = QK_NOPE + QK_ROPE
    half = QK_ROPE // 2
    w_rope = w_dkv[:, KV_RANK:]
    w_rope_sw = jnp.concatenate([w_rope[:, half:], w_rope[:, :half]], axis=1)
    zl = jnp.zeros((d, QK_NOPE), F32)
    zr = jnp.zeros((d, HEAD_PAD - hq), F32)
    wd = jnp.concatenate([w_dq, w_dkv[:, :KV_RANK], zl, w_rope, zr, zl, w_rope_sw, zr], axis=1)
    wq = w_uq.reshape(Q_RANK, MLA_HEADS, hq)
    zq = jnp.zeros((Q_RANK, MLA_HEADS, HEAD_PAD - hq), F32)
    wqa = jnp.concatenate([wq, zq], axis=2).reshape(Q_RANK, MLA_HEADS * HEAD_PAD)
    zn = jnp.zeros((Q_RANK, MLA_HEADS, QK_NOPE), F32)
    wqb = jnp.concatenate([zn, wq[:, :, QK_NOPE + half:], wq[:, :, QK_NOPE:QK_NOPE + half], zq],
                          axis=2).reshape(Q_RANK, MLA_HEADS * HEAD_PAD)
    wkv = w_ukv.reshape(KV_RANK, MLA_HEADS, QK_NOPE + V_DIM)
    zk = jnp.zeros((KV_RANK, MLA_HEADS, HEAD_PAD - QK_NOPE), F32)
    wk = jnp.concatenate([wkv[:, :, :QK_NOPE], zk], axis=2).reshape(KV_RANK, MLA_HEADS * HEAD_PAD)
    wv = jnp.transpose(wkv[:, :, QK_NOPE:], (1, 2, 0))
    wvt = jnp.concatenate([wv, jnp.zeros((MLA_HEADS, VT_ROWS - V_DIM, KV_RANK), F32)], axis=1)
    return {
        "wd": wd.astype(BF16), "gq": g_q.reshape(1, -1), "gkv": g_kv.reshape(1, -1),
        "wqa": wqa.T.astype(BF16), "wqb": wqb.T.astype(BF16), "wk": wk.astype(BF16),
        "wvt": wvt.reshape(MLA_HEADS * VT_ROWS, KV_RANK).astype(BF16),
    }


def _rope_tables(l):
    t = jnp.arange(l)
    row = (t // GRID_W).astype(F32)
    col = (t % GRID_W).astype(F32)
    n_freq = QK_ROPE // 4
    inv = ROPE_BASE ** (-jnp.arange(n_freq, dtype=F32) / n_freq)
    ang = jnp.concatenate([row[:, None] * inv, col[:, None] * inv], axis=-1)
    cos, sin = jnp.cos(ang), jnp.sin(ang)
    ones = jnp.ones((l, QK_NOPE), F32)
    zl = jnp.zeros((l, QK_NOPE), F32)
    zr = jnp.zeros((l, HEAD_PAD - QK_NOPE - QK_ROPE), F32)
    cos_t = jnp.concatenate([ones, cos, cos, zr], axis=1)
    sin_t = jnp.concatenate([zl, -sin, sin, zr], axis=1)
    return cos_t, sin_t


def _no_rope_tables(l):
    keep = jnp.concatenate([jnp.ones((l, QK_NOPE + QK_ROPE), F32),
                            jnp.zeros((l, HEAD_PAD - QK_NOPE - QK_ROPE), F32)], axis=1)
    return keep, jnp.zeros((l, HEAD_PAD), F32)


def _attn_ctx_kernel(q_ref, kc_ref, vtc_ref, o_ref):
    outs = []
    for hd in range(2):
        s = _dot(kc_ref[0, hd], q_ref[0, hd * HEAD_PAD:(hd + 1) * HEAD_PAD, :])
        p = jnp.exp2(s - jnp.max(s, axis=0, keepdims=True)).astype(BF16)
        acc = _dot(vtc_ref[0, 0, hd], p)
        outs.append(acc[:V_DIM] / acc[ONES_ROW:ONES_ROW + 1])
    o_ref[0] = jnp.concatenate(outs, axis=0).T.astype(BF16)


def _attn_kernel(q_ref, kc_ref, vtc_ref, k_ref, vt_ref, o_ref, *scr, n_lat_chunks, tk, unroll):
    heads = (0, 1)
    s_scr = [scr[0:2], scr[2:4]]
    p_scr = [scr[4:6], scr[6:8]]
    acc_scr = scr[8:10]
    tq = q_ref.shape[2]
    n_pos = n_lat_chunks + 1

    def stage_a(hd, slot, p):
        if isinstance(p, int) and p == 0:
            k = kc_ref[0, hd]
        else:
            start = (p - 1) * tk
            k = k_ref[0, hd, pl.ds(start if isinstance(p, int) else pl.multiple_of(start, tk), tk), :]
        s = _dot(k, q_ref[0, hd * HEAD_PAD:(hd + 1) * HEAD_PAD, :])
        s_scr[hd][slot][...] = s
        return jnp.max(s.reshape(tk // SUBLANES, SUBLANES, tq), axis=0)

    def stage_b(hd, slot, cm, m_old):
        m_new = jnp.maximum(m_old, jnp.max(cm, axis=0, keepdims=True))
        p_scr[hd][slot][...] = jnp.exp2(s_scr[hd][slot][...] - m_new).astype(BF16)
        return m_new, jnp.exp2(m_old - m_new)

    def stage_c(hd, slot, p, alpha):
        vt = vtc_ref[0, 0, hd] if isinstance(p, int) and p == 0 else vt_ref[0, p - 1, hd]
        acc_scr[hd][...] = acc_scr[hd][...] * alpha + _dot(vt, p_scr[hd][slot][...])

    carry = []
    for hd in heads:
        acc_scr[hd][...] = jnp.zeros(acc_scr[hd].shape, F32)
        cm0 = stage_a(hd, 0, 0)
        cm1 = stage_a(hd, 1, 1)
        m, al = stage_b(hd, 0, cm0, jnp.full((1, tq), -jnp.inf, F32))
        carry += [m, al, cm1]

    def steps(carry, p0, parity, count):
        carry = list(carry)
        for u in range(count):
            slot = (parity + u) % 2
            for hd in heads:
                m, al, cm = carry[3 * hd:3 * hd + 3]
                stage_c(hd, slot, p0 + u, al)
                m, al = stage_b(hd, 1 - slot, cm, m)
                cm = stage_a(hd, slot, p0 + u + 2)
                carry[3 * hd:3 * hd + 3] = [m, al, cm]
        return tuple(carry)

    n_steps = n_pos - 2
    peel = n_steps % unroll or min(unroll, n_steps)
    carry = steps(carry, 0, 0, peel)
    carry = lax.fori_loop(0, (n_steps - peel) // unroll,
                          lambda j, cr: steps(cr, peel + unroll * j, peel % 2, unroll), carry)
    outs = []
    last = (n_pos - 2) % 2
    for hd in heads:
        m, al, cm = carry[3 * hd:3 * hd + 3]
        stage_c(hd, last, n_pos - 2, al)
        m, al = stage_b(hd, 1 - last, cm, m)
        stage_c(hd, 1 - last, n_pos - 1, al)
        acc = acc_scr[hd][...]
        outs.append(acc[:V_DIM] / acc[ONES_ROW:ONES_ROW + 1])
    o_ref[0] = jnp.concatenate(outs, axis=0).T.astype(BF16)


def _attention(q, kc, vtc, k=None, vt=None, *, tq):
    b, _, lq = q.shape
    cl = kc.shape[2]
    nq = lq // tq
    hp = 2 * HEAD_PAD
    in_specs = [
        pl.BlockSpec((1, hp, tq), lambda bi, h, i: (bi, h, i)),
        pl.BlockSpec((1, 2, cl, HEAD_PAD), lambda bi, h, i: (bi, h, 0, 0)),
        pl.BlockSpec((1, 1, 2, VT_ROWS, cl), lambda bi, h, i: (bi, 0, h, 0, 0)),
    ]
    args = [q, kc, vtc]
    n_chunks = 0
    scratch = []
    body = _attn_ctx_kernel
    if k is not None:
        lk = k.shape[2]
        n_chunks, tk = vt.shape[1], vt.shape[4]
        assert n_chunks >= 2 and cl == tk
        unroll = max(2, min(ATTN_UNROLL, (n_chunks - 2) // 2 * 2))
        body = functools.partial(_attn_kernel, n_lat_chunks=n_chunks, tk=tk, unroll=unroll)
        in_specs += [
            pl.BlockSpec((1, 2, lk, HEAD_PAD), lambda bi, h, i: (bi, h, 0, 0)),
            pl.BlockSpec((1, n_chunks, 2, VT_ROWS, tk), lambda bi, h, i: (bi, 0, h, 0, 0)),
        ]
        args += [k, vt]
        scratch = ([pltpu.VMEM((tk, tq), F32)] * 4 + [pltpu.VMEM((tk, tq), BF16)] * 4
                   + [pltpu.VMEM((VT_ROWS, tq), F32)] * 2)
    return pl.pallas_call(
        body,
        grid=(b, MLA_HEADS // 2, nq),
        in_specs=in_specs,
        out_specs=pl.BlockSpec((1, tq, 2 * V_DIM), lambda bi, h, i: (bi, i, h)),
        out_shape=jax.ShapeDtypeStruct((b, lq, MLA_HEADS * V_DIM), BF16),
        scratch_shapes=scratch,
        compiler_params=_cparams("parallel", "parallel", "arbitrary"),
        name="attention" if n_chunks else "attention_ctx",
    )(*args)


def _mlp_kernel(a_ref, wo_ref, bo_ref, gt0_ref, g0_ref, x_ref, sh_ref, sc_ref, gt_ref, g1_ref, g2_ref,
                w1_ref, w2_ref, o_ref):
    y = _dot(a_ref[0], wo_ref[...]) + bo_ref[...]
    x = x_ref[0] + gt0_ref[0] * _rms(y, g0_ref[...])
    h = (_rms(x, g1_ref[...]) * (1.0 + sc_ref[0]) + sh_ref[0]).astype(BF16)
    dff = w1_ref.shape[1]
    m = None
    for c in range(dff // MLP_FF_CHUNK):
        cols = slice(c * MLP_FF_CHUNK, (c + 1) * MLP_FF_CHUNK)
        u = jnp.maximum(_dot(h, w1_ref[:, cols]), 0.0)
        part = _dot((u * u).astype(BF16), w2_ref[cols, :])
        m = part if m is None else m + part
    o_ref[0] = x + gt_ref[0] * _rms(m, g2_ref[...])


def _mixer_out_mlp(a, wo, bo, gt0, g0, x, sh, sc, gt, g1, g2, w1, w2, tm):
    b, l, d = x.shape
    din = a.shape[2]
    dff = w1.shape[1]
    vec = pl.BlockSpec((1, 1, d), lambda bi, i: (bi, 0, 0))
    row = pl.BlockSpec((1, d), lambda bi, i: (0, 0))
    const = lambda shape: pl.BlockSpec(shape, lambda bi, i: (0, 0), pipeline_mode=pl.Buffered(1))
    return pl.pallas_call(
        _mlp_kernel,
        grid=(b, l // tm),
        in_specs=[
            pl.BlockSpec((1, tm, din), lambda bi, i: (bi, i, 0)),
            const((din, d)), row, vec, row,
            pl.BlockSpec((1, tm, d), lambda bi, i: (bi, i, 0)),
            vec, vec, vec, row, row,
            const((d, dff)), const((dff, d)),
        ],
        out_specs=pl.BlockSpec((1, tm, d), lambda bi, i: (bi, i, 0)),
        out_shape=jax.ShapeDtypeStruct((b, l, d), F32),
        compiler_params=_cparams("parallel", "parallel"),
        name="mixer_out_mlp",
    )(a, wo, bo, gt0, g0, x, sh, sc, gt, g1, g2, w1, w2)


def _hy_in_kernel(x_ref, xp_ref, xn_ref, sh_ref, sc_ref, g_ref, w_ref, b_ref, ws_ref, bs_ref,
                  x1_ref, x2_ref, v_ref, u_scr):
    i = pl.program_id(1)
    n = pl.num_programs(1)
    tm = x_ref.shape[1]
    d = x_ref.shape[2]
    xa = jnp.concatenate([xp_ref[0], x_ref[0], xn_ref[0]], axis=0)
    h = (_rms(xa, g_ref[...]) * (1.0 + sc_ref[0]) + sh_ref[0]).astype(BF16)
    u_scr[...] = _dot(h, w_ref[...]) + b_ref[...]

    @pl.when(i == 0)
    def _():
        u_scr[0:HALO, :] = jnp.zeros((HALO, 3 * d), F32)

    @pl.when(i == n - 1)
    def _():
        u_scr[tm + HALO:tm + 2 * HALO, :] = jnp.zeros((HALO, 3 * d), F32)

    rows = tm + 2 * HALO
    for j, o_ref in enumerate((x1_ref, x2_ref, v_ref)):
        sl = slice(j * d, (j + 1) * d)
        u = u_scr[:, sl]
        prev = pltpu.roll(u, 1, axis=0)[HALO:HALO + tm]
        nxt = pltpu.roll(u, rows - 1, axis=0)[HALO:HALO + tm]
        y = (prev * ws_ref[0:1, sl] + u[HALO:HALO + tm] * ws_ref[1:2, sl] + nxt * ws_ref[2:3, sl]
             + bs_ref[:, sl])
        o_ref[0] = y.astype(BF16)


def _hy_in(x, sh, sc, g, w_in, b_in, w_short, b_short, tm):
    b, l, d = x.shape
    nt = l // tm
    tb = tm // HALO
    nb = l // HALO
    vec = pl.BlockSpec((1, 1, d), lambda bi, i: (bi, 0, 0))
    out = pl.BlockSpec((1, tm, d), lambda bi, i: (bi, i, 0))
    return pl.pallas_call(
        _hy_in_kernel,
        grid=(b, nt),
        in_specs=[
            pl.BlockSpec((1, tm, d), lambda bi, i: (bi, i, 0)),
            pl.BlockSpec((1, HALO, d), lambda bi, i: (bi, jnp.maximum(i * tb - 1, 0), 0)),
            pl.BlockSpec((1, HALO, d), lambda bi, i: (bi, jnp.minimum((i + 1) * tb, nb - 1), 0)),
            vec, vec,
            pl.BlockSpec((1, d), lambda bi, i: (0, 0)),
            pl.BlockSpec((d, 3 * d), lambda bi, i: (0, 0)),
            pl.BlockSpec((1, 3 * d), lambda bi, i: (0, 0)),
            pl.BlockSpec((3, 3 * d), lambda bi, i: (0, 0)),
            pl.BlockSpec((1, 3 * d), lambda bi, i: (0, 0)),
        ],
        out_specs=[out, out, out],
        out_shape=[jax.ShapeDtypeStruct((b, l, d), BF16)] * 3,
        scratch_shapes=[pltpu.VMEM((tm + 2 * HALO, 3 * d), F32)],
        compiler_params=_cparams("parallel", "parallel"),
        name="hyena_in",
    )(x, x, x, sh, sc, g, w_in, b_in, w_short, b_short)


def _filt_kernel(z_ref, w1_ref, b1_ref, w2_ref, b2_ref, w3_ref, fr_ref, dl_ref, o_ref, *, l):
    tl = z_ref.shape[0]
    d = dl_ref.shape[1]
    fr = fr_ref[...]
    hdn = jnp.sin(fr * (_dot3(z_ref[...], w1_ref[...]) + b1_ref[...]))
    hdn = jnp.sin(fr * (_dot3(hdn, w2_ref[...]) + b2_ref[...]))
    hf = _dot3(hdn, w3_ref[...])
    rows = pl.program_id(0) * tl + lax.broadcasted_iota(jnp.int32, (tl, d), 0)
    t = rows.astype(F32) * (1.0 / (l - 1))
    decay = jnp.exp(-t * jnp.abs(dl_ref[...]))
    for j in range(2 * HY_ORDER):
        f = hf[:, j * d:(j + 1) * d] * decay
        if j % 2 == 1:
            f = jnp.where(rows == 0, 0.0, f)
        o_ref[j] = f.astype(BF16)


def _filters(l, d, f_w1, f_b1, f_w2, f_b2, f_w3, f_freq):
    t = jnp.linspace(0.0, 1.0, l, dtype=F32)[:, None]
    w = (2.0 * math.pi) * jnp.arange(l, dtype=F32)[:, None] / l
    f = jnp.linspace(1e-4, HY_BANDS - 1, HY_BANDS, dtype=F32)[None, :]
    kz = HY_FILTER_HIDDEN
    z = jnp.concatenate([t, jnp.cos(f * w), -jnp.sin(f * w), jnp.zeros((l, kz - HY_EMB), F32)], axis=-1)
    w1 = jnp.concatenate([f_w1, jnp.zeros((kz - HY_EMB, HY_FILTER_HIDDEN), F32)], axis=0)
    min_decay = math.log(HY_TARGET) / HY_SLOW_PCT
    max_decay = math.log(HY_TARGET) / HY_FAST_PCT
    deltas = jnp.linspace(min_decay, max_decay, d, dtype=F32)[None, :]
    tl = FILTER_ROWS
    full = lambda a: pl.BlockSpec(a.shape, lambda i: (0,) * a.ndim)
    ops = [w1, f_b1.reshape(1, -1), f_w2, f_b2.reshape(1, -1), f_w3, f_freq.reshape(1, -1), deltas]
    return pl.pallas_call(
        functools.partial(_filt_kernel, l=l),
        grid=(l // tl,),
        in_specs=[pl.BlockSpec((tl, kz), lambda i: (i, 0))] + [full(a) for a in ops],
        out_specs=pl.BlockSpec((2 * HY_ORDER, tl, d), lambda i: (0, i, 0)),
        out_shape=jax.ShapeDtypeStruct((2 * HY_ORDER, l, d), BF16),
        compiler_params=_cparams("parallel"),
        name="hyena_filters",
    )(z, *ops)


def _dft_tables(l):
    n = 2 * l
    n2 = DFT_N2
    n1 = n // n2
    n1h = n1 // 2
    slots = -(-(n1h + 1) // SLOT_BLOCK) * SLOT_BLOCK
    k1 = jnp.arange(slots)
    valid = (k1 <= n1h)[:, None]
    a = jnp.arange(n1h)
    ang1 = (2.0 * math.pi / n1) * ((k1[:, None] * a[None, :]) % n1).astype(F32)
    f1 = jnp.concatenate([jnp.where(valid, jnp.cos(ang1), 0.0), jnp.where(valid, -jnp.sin(ang1), 0.0)], axis=0)
    kk = jnp.arange(n1h)
    wgt = jnp.where(kk == 0, 1.0, 2.0)[None, :] / n
    ang2 = (2.0 * math.pi / n1) * ((a[:, None] * kk[None, :]) % n1).astype(F32)
    f2 = jnp.concatenate([wgt * jnp.cos(ang2), -wgt * jnp.sin(ang2)], axis=1)
    k2 = jnp.arange(n2)
    j = jnp.arange(n2)
    prod = (j[None, None, :] * (k1[:, None, None] + n1 * k2[None, :, None])) % n
    th = (2.0 * math.pi / n) * prod.astype(F32)
    live = valid[:, :, None]
    c, s = jnp.where(live, jnp.cos(th), 0.0), jnp.where(live, jnp.sin(th), 0.0)
    g = jnp.concatenate([jnp.concatenate([c, s], axis=2), jnp.concatenate([-s, c], axis=2)], axis=1)
    ct, st = jnp.swapaxes(c, 1, 2), jnp.swapaxes(s, 1, 2)
    gi = jnp.concatenate([jnp.concatenate([ct, -st], axis=2), jnp.concatenate([st, ct], axis=2)], axis=1)
    return {"n": n, "n1h": n1h, "slots": slots,
            "f1": f1.astype(BF16), "f2": f2.astype(BF16), "g": g.astype(BF16), "gi": gi.astype(BF16)}


def _swap_rows(x):
    return pltpu.einshape("abc->bac", x)


def _stage1_kernel(f_ref, u_ref, a_ref, xs_scr):
    jb = pl.program_id(2)
    n2, n1h, td = xs_scr.shape

    @pl.when(jb == 0)
    def _():
        xs_scr[...] = _swap_rows(u_ref[0].reshape(n1h, n2, td))

    for jj in range(J_BLOCK):
        a_ref[0, jj] = _dot(f_ref[...], xs_scr[jb * J_BLOCK + jj]).astype(BF16)


def _stage1(u, tab):
    b, l, d = u.shape
    n1h, slots = tab["n1h"], tab["slots"]
    n2 = DFT_N2
    td = STAGE1_TD
    return pl.pallas_call(
        _stage1_kernel,
        grid=(b, d // td, n2 // J_BLOCK),
        in_specs=[
            pl.BlockSpec((2 * slots, n1h), lambda bi, ci, jb: (0, 0)),
            pl.BlockSpec((1, l, td), lambda bi, ci, jb: (bi, 0, ci)),
        ],
        out_specs=pl.BlockSpec((1, J_BLOCK, 2 * slots, td), lambda bi, ci, jb: (bi, jb, 0, ci)),
        out_shape=jax.ShapeDtypeStruct((b, n2, 2 * slots, d), BF16),
        scratch_shapes=[pltpu.VMEM((n2, n1h, td), BF16)],
        compiler_params=_cparams("parallel", "parallel", "arbitrary"),
        name="dft_stage1",
    )(tab["f1"], u)


def _slot_rows(ref):
    n2, _, sb, td = ref.shape
    return _swap_rows(ref[...].reshape(n2, 2 * sb, td))


def _filt_spec_kernel(g_ref, af_ref, ab_ref, bias_ref, k_ref, f_scr, b_scr):
    n2 = f_scr.shape[1]
    f_scr[...] = _slot_rows(af_ref.at[0])
    b_scr[...] = _slot_rows(ab_ref.at[0])
    bias = bias_ref[0]

    def body(s, carry):
        g = g_ref[s]
        xf = _dot(g, jnp.concatenate([f_scr[s], f_scr[SLOT_BLOCK + s]], axis=0))
        xb = _dot(g, jnp.concatenate([b_scr[s], b_scr[SLOT_BLOCK + s]], axis=0))
        k_ref[0, 0, s] = (xf[:n2] + xb[:n2] + bias).astype(BF16)
        k_ref[0, 1, s] = (xf[n2:] - xb[n2:]).astype(BF16)
        return carry

    lax.fori_loop(0, SLOT_BLOCK, body, 0, unroll=4)


def _filt_spectrum(a_filt, bias, tab):
    _, n2, _, d = a_filt.shape
    slots = tab["slots"]
    td = MID_TD
    a5 = a_filt.reshape(a_filt.shape[0], n2, 2, slots, d)
    blk = lambda seq: pl.BlockSpec((1, n2, 2, SLOT_BLOCK, td), lambda o, kb, ci: (2 * o + seq, 0, 0, kb, ci))
    return pl.pallas_call(
        _filt_spec_kernel,
        grid=(HY_ORDER, slots // SLOT_BLOCK, d // td),
        in_specs=[
            pl.BlockSpec((SLOT_BLOCK, 2 * n2, 2 * n2), lambda o, kb, ci: (kb, 0, 0)),
            blk(0), blk(1),
            pl.BlockSpec((1, 1, td), lambda o, kb, ci: (o, 0, ci)),
        ],
        out_specs=pl.BlockSpec((1, 2, SLOT_BLOCK, n2, td), lambda o, kb, ci: (o, 0, kb, 0, ci)),
        out_shape=jax.ShapeDtypeStruct((HY_ORDER, 2, slots, n2, d), BF16),
        scratch_shapes=[pltpu.VMEM((2 * SLOT_BLOCK, n2, td), BF16)] * 2,
        compiler_params=_cparams("parallel", "parallel", "parallel"),
        name="hyena_filter_spectrum",
    )(tab["g"], a5, a5, bias.reshape(HY_ORDER, 1, d))


def _mid_kernel(g_ref, gi_ref, a_ref, k_ref, z_ref, a_scr, z_scr):
    n2 = a_scr.shape[1]
    a_scr[...] = _slot_rows(a_ref.at[0])

    def body(s, carry):
        x = _dot(g_ref[s], jnp.concatenate([a_scr[s], a_scr[SLOT_BLOCK + s]], axis=0))
        xr, xi = x[:n2], x[n2:]
        kr = k_ref[0, 0, s].astype(F32)
        ki = k_ref[0, 1, s].astype(F32)
        y = jnp.concatenate([xr * kr - xi * ki, xr * ki + xi * kr], axis=0).astype(BF16)
        z = _dot(gi_ref[s], y)
        z_scr[s] = z[:n2].astype(BF16)
        z_scr[SLOT_BLOCK + s] = z[n2:].astype(BF16)
        return carry

    lax.fori_loop(0, SLOT_BLOCK, body, 0, unroll=4)
    td = z_scr.shape[2]
    z_ref[0] = _swap_rows(z_scr[...]).reshape(n2, 2, SLOT_BLOCK, td)


def _mid(a, kf, order, tab):
    b, n2, _, d = a.shape
    slots = tab["slots"]
    td = MID_TD
    a5 = a.reshape(b, n2, 2, slots, d)
    gspec = pl.BlockSpec((SLOT_BLOCK, 2 * n2, 2 * n2), lambda bi, kb, ci: (kb, 0, 0))
    aspec = pl.BlockSpec((1, n2, 2, SLOT_BLOCK, td), lambda bi, kb, ci: (bi, 0, 0, kb, ci))
    z = pl.pallas_call(
        _mid_kernel,
        grid=(b, slots // SLOT_BLOCK, d // td),
        in_specs=[
            gspec, gspec, aspec,
            pl.BlockSpec((1, 2, SLOT_BLOCK, n2, td), lambda bi, kb, ci: (order, 0, kb, 0, ci)),
        ],
        out_specs=aspec,
        out_shape=jax.ShapeDtypeStruct((b, n2, 2, slots, d), BF16),
        scratch_shapes=[pltpu.VMEM((2 * SLOT_BLOCK, n2, td), BF16)] * 2,
        compiler_params=_cparams("parallel", "parallel", "parallel"),
        name="dft_mid",
    )(tab["g"], tab["gi"], a5, kf)
    return z.reshape(b, n2, 2 * slots, d)


def _stage2_kernel(f_ref, z_ref, x_ref, o_ref, ys_scr, *, slots, inv_n):
    jb = pl.program_id(2)
    n2, n1h, td = ys_scr.shape
    rows = lax.broadcasted_iota(jnp.int32, (n1h, td), 0)
    for jj in range(J_BLOCK):
        z = z_ref[0, jj]
        y = _dot(f_ref[...], jnp.concatenate([z[0:n1h], z[slots:slots + n1h]], axis=0))
        nyq = z[n1h:n1h + 1].astype(F32) * inv_n
        ys_scr[jb * J_BLOCK + jj] = (y + jnp.where(rows % 2 == 0, nyq, -nyq)).astype(BF16)

    @pl.when(jb == pl.num_programs(2) - 1)
    def _():
        o_ref[0] = _swap_rows(ys_scr[...]).reshape(n1h * n2, td) * x_ref[0]


def _stage2(z, gate, tab):
    b, n2, _, d = z.shape
    n1h, slots = tab["n1h"], tab["slots"]
    l = n1h * n2
    td = STAGE2_TD
    return pl.pallas_call(
        functools.partial(_stage2_kernel, slots=slots, inv_n=1.0 / tab["n"]),
        grid=(b, d // td, n2 // J_BLOCK),
        in_specs=[
            pl.BlockSpec((n1h, 2 * n1h), lambda bi, ci, jb: (0, 0)),
            pl.BlockSpec((1, J_BLOCK, 2 * slots, td), lambda bi, ci, jb: (bi, jb, 0, ci)),
            pl.BlockSpec((1, l, td), lambda bi, ci, jb: (bi, 0, ci)),
        ],
        out_specs=pl.BlockSpec((1, l, td), lambda bi, ci, jb: (bi, 0, ci)),
        out_shape=jax.ShapeDtypeStruct((b, l, d), BF16),
        scratch_shapes=[pltpu.VMEM((n2, n1h, td), BF16)],
        compiler_params=_cparams("parallel", "parallel", "arbitrary"),
        name="dft_stage2",
    )(tab["f2"], z, gate)


def _small_dft_tables(l):
    n = 2 * l
    k = jnp.arange(l)
    t = jnp.arange(l)
    ang = (2.0 * math.pi / n) * ((k[:, None] * t[None, :]) % n).astype(F32)
    sign = jnp.where(t % 2 == 0, 1.0, -1.0)[None, :]
    pad = jnp.zeros((SUBLANES - 1, l), F32)
    fwd = jnp.concatenate([jnp.cos(ang), -jnp.sin(ang), sign, pad], axis=0)
    wgt = jnp.where(k == 0, 1.0, 2.0)[None, :] / n
    ang_t = ang.T
    inv = jnp.concatenate([wgt * jnp.cos(ang_t), -wgt * jnp.sin(ang_t)], axis=1)
    return {"n": n, "fwd": fwd.astype(BF16), "inv": inv.astype(BF16)}


def _small_spec_kernel(f_ref, hf_ref, hb_ref, bias_ref, k_ref):
    l = hf_ref.shape[1]
    xf = _dot(f_ref[...], hf_ref[0])
    xb = _dot(f_ref[...], hb_ref[0])
    bias = bias_ref[0]
    k_ref[0, 0:l, :] = (xf[:l] + xb[:l] + bias).astype(BF16)
    k_ref[0, l:2 * l, :] = (xf[l:2 * l] - xb[l:2 * l]).astype(BF16)
    k_ref[0, 2 * l:, :] = (xf[2 * l:] + xb[2 * l:] + bias).astype(BF16)


def _small_filt_spectrum(filt, bias, tab):
    _, l, d = filt.shape
    rows = 2 * l + SUBLANES
    tc = SMALL_TD
    return pl.pallas_call(
        _small_spec_kernel,
        grid=(HY_ORDER, d // tc),
        in_specs=[
            pl.BlockSpec((rows, l), lambda o, j: (0, 0)),
            pl.BlockSpec((1, l, tc), lambda o, j: (2 * o, 0, j)),
            pl.BlockSpec((1, l, tc), lambda o, j: (2 * o + 1, 0, j)),
            pl.BlockSpec((1, 1, tc), lambda o, j: (o, 0, j)),
        ],
        out_specs=pl.BlockSpec((1, rows, tc), lambda o, j: (o, 0, j)),
        out_shape=jax.ShapeDtypeStruct((HY_ORDER, rows, d), BF16),
        compiler_params=_cparams("parallel", "parallel"),
        name="hyena_filter_spectrum_ctx",
    )(tab["fwd"], filt, filt, bias.reshape(HY_ORDER, 1, d))


def _small_conv_kernel(f_ref, fi_ref, u_ref, k_ref, x_ref, o_ref, *, inv_n):
    l = u_ref.shape[1]
    x = _dot(f_ref[...], u_ref[0])
    xr, xi = x[:l], x[l:2 * l]
    kr = k_ref[0, 0:l, :].astype(F32)
    ki = k_ref[0, l:2 * l, :].astype(F32)
    y = jnp.concatenate([xr * kr - xi * ki, xr * ki + xi * kr], axis=0).astype(BF16)
    out = _dot(fi_ref[...], y)
    nyq = x[2 * l:2 * l + 1] * k_ref[0, 2 * l:2 * l + 1, :].astype(F32) * inv_n
    rows = lax.broadcasted_iota(jnp.int32, out.shape, 0)
    out = out + jnp.where(rows % 2 == 0, nyq, -nyq)
    o_ref[0] = (out * x_ref[0].astype(F32)).astype(BF16)


def _small_conv(u, kf, order, gate, tab):
    b, l, d = u.shape
    rows = 2 * l + SUBLANES
    tc = SMALL_TD
    return pl.pallas_call(
        functools.partial(_small_conv_kernel, inv_n=1.0 / tab["n"]),
        grid=(b, d // tc),
        in_specs=[
            pl.BlockSpec((rows, l), lambda bi, j: (0, 0)),
            pl.BlockSpec((l, 2 * l), lambda bi, j: (0, 0)),
            pl.BlockSpec((1, l, tc), lambda bi, j: (bi, 0, j)),
            pl.BlockSpec((1, rows, tc), lambda bi, j: (order, 0, j)),
            pl.BlockSpec((1, l, tc), lambda bi, j: (bi, 0, j)),
        ],
        out_specs=pl.BlockSpec((1, l, tc), lambda bi, j: (bi, 0, j)),
        out_shape=jax.ShapeDtypeStruct((b, l, d), BF16),
        compiler_params=_cparams("parallel", "parallel"),
        name="long_conv_ctx",
    )(tab["fwd"], tab["inv"], u, kf, gate)


def _tile(l, want):
    while l % want:
        want //= 2
    return want


def _hyena_mixer_lat(x, sh, sc, g, p, tabs):
    b, l, d = x.shape
    x1, x2, v = _hy_in(x, sh, sc, g, p["w_in"], p["b_in"], p["w_short"], p["b_short"], _tile(l, ROW_TILE))
    filt = _filters(l, d, *p["filter"])
    kf = _filt_spectrum(_stage1(filt, tabs), p["bias"], tabs)
    z = _stage2(_mid(_stage1(v, tabs), kf, 0, tabs), x1, tabs)
    z = _stage2(_mid(_stage1(z, tabs), kf, 1, tabs), x2, tabs)
    return z


def _hyena_mixer_ctx(x, sh, sc, g, p, tabs):
    b, l, d = x.shape
    x1, x2, v = _hy_in(x, sh, sc, g, p["w_in"], p["b_in"], p["w_short"], p["b_short"], l)
    filt = _filters(l, d, *p["filter"])
    kf = _small_filt_spectrum(filt, p["bias"], tabs)
    z = _small_conv(v, kf, 0, x1, tabs)
    z = _small_conv(z, kf, 1, x2, tabs)
    return z


def kernel(x, c, ctx, c_ctx, ada_w, ada_b, norm_g, mla_w_dq, mla_g_q, mla_w_uq, mla_w_dkv, mla_g_kv,
           mla_w_ukv, mla_w_o, hy_w_in, hy_b_in, hy_w_short, hy_b_short, hy_f_w1, hy_f_b1, hy_f_w2,
           hy_f_b2, hy_f_w3, hy_f_freq, hy_bias, hy_w_out, hy_b_out, mlp_w1, mlp_w2):
    b, l, d = x.shape
    cl = ctx.shape[1]
    depth = ada_w.shape[0]
    assert b + 1 <= SUBLANES and l % ATTN_TK == 0 and cl == ATTN_TK

    cvec = jnp.concatenate([c, c_ctx[None, :], jnp.zeros((SUBLANES - b - 1, d), F32)], axis=0)
    mods = _ada(cvec, ada_w, ada_b)

    rope_lat = _rope_tables(l)
    rope_ctx = _no_rope_tables(cl)
    tabs_lat = _dft_tables(l)
    tabs_ctx = _small_dft_tables(cl)
    zero_bias = jnp.zeros((1, d), F32)

    xc = ctx
    for i in range(depth):
        last = i == depth - 1
        j = i // 2
        g = norm_g[i].reshape(4, 1, d)
        m_lat = mods[i, :b].reshape(b, 6, 1, d)
        m_ctx = jnp.broadcast_to(mods[i, b].reshape(1, 6, 1, d), (b, 6, 1, d))
        lat = [m_lat[:, k] for k in range(6)]
        cx = [m_ctx[:, k] for k in range(6)]

        if i % 2 == 0:
            w = _mla_weights(mla_w_dq[j], mla_g_q[j], mla_w_uq[j], mla_w_dkv[j], mla_g_kv[j], mla_w_ukv[j])
            wo = mla_w_o[j].astype(BF16)
            bo = zero_bias
            qc, kc, vtc = _qkv(xc, cx[0], cx[1], g[0], w, *rope_ctx, cl)
            ql, kl, vtl = _qkv(x, lat[0], lat[1], g[0], w, *rope_lat, ATTN_TK)
            y_lat = _attention(ql, kc, vtc, kl, vtl, tq=_tile(l, ATTN_TQ))
            y_ctx = None if last else _attention(qc, kc, vtc, tq=cl)
        else:
            p = {
                "w_in": hy_w_in[j].astype(BF16), "b_in": hy_b_in[j].reshape(1, -1),
                "w_short": hy_w_short[j], "b_short": hy_b_short[j].reshape(1, -1),
                "filter": (hy_f_w1[j], hy_f_b1[j], hy_f_w2[j], hy_f_b2[j], hy_f_w3[j], hy_f_freq[j]),
                "bias": hy_bias[j],
            }
            wo = hy_w_out[j].astype(BF16)
            bo = hy_b_out[j].reshape(1, d)
            y_lat = _hyena_mixer_lat(x, lat[0], lat[1], g[0], p, tabs_lat)
            y_ctx = None if last else _hyena_mixer_ctx(xc, cx[0], cx[1], g[0], p, tabs_ctx)

        w1 = mlp_w1[i].astype(BF16)
        w2 = mlp_w2[i].astype(BF16)
        x = _mixer_out_mlp(y_lat, wo, bo, lat[2], g[1], x, lat[3], lat[4], lat[5], g[2], g[3], w1, w2,
                           _tile(l, MLP_ROWS))
        if not last:
            xc = _mixer_out_mlp(y_ctx, wo, bo, cx[2], g[1], xc, cx[3], cx[4], cx[5], g[2], g[3], w1, w2, cl)
    return x
```

```python
import functools
import math

import jax
import jax.numpy as jnp
from jax import lax
from jax.experimental import pallas as pl
from jax.experimental.pallas import tpu as pltpu

F32 = jnp.float32
BF16 = jnp.bfloat16

GRID_W = 64
MLA_HEADS = 16
QK_NOPE = 64
QK_ROPE = 32
V_DIM = 64
Q_RANK = 256
KV_RANK = 128
ROPE_BASE = 10000.0
HY_ORDER = 2
HY_EMB = 33
HY_BANDS = (HY_EMB - 1) // 2
HY_FILTER_HIDDEN = 64
HY_TARGET = 1e-2
HY_FAST_PCT = 0.3
HY_SLOW_PCT = 1.5
EPS = 1e-6

LANES = 128
SUBLANES = 8
HEAD_PAD = LANES
VT_ROWS = 80
ONES_ROW = V_DIM
VMEM_LIMIT = 56 * 1024 * 1024

ADA_TN = 1536
ATTN_TQ = 512
ATTN_TK = 256
ATTN_UNROLL = 10
MLP_ROWS = 1024
MLP_FF_CHUNK = 1024
ROW_TILE = 1024
HALO = SUBLANES
FILTER_ROWS = 256
DFT_N2 = 128
SLOT_BLOCK = 16
J_BLOCK = 32
STAGE1_TD = 256
STAGE2_TD = 256
MID_TD = 512
SMALL_TD = 512

LOG2E = 1.4426950408889634


def _cparams(*sem):
    return pltpu.CompilerParams(dimension_semantics=sem, vmem_limit_bytes=VMEM_LIMIT)


def _dot(a, b):
    return jnp.dot(a, b, preferred_element_type=F32)


def _dot3(a, b):
    ah = a.astype(BF16)
    al = (a - ah.astype(F32)).astype(BF16)
    bh = b.astype(BF16)
    bl = (b - bh.astype(F32)).astype(BF16)
    return _dot(ah, bh) + _dot(al, bh) + _dot(ah, bl)


def _rms(x, g):
    return x * lax.rsqrt(jnp.mean(x * x, axis=-1, keepdims=True) + EPS) * g


def _ada_kernel(c_ref, w_ref, b_ref, o_ref):
    c = c_ref[...]
    s = c / (1.0 + jnp.exp(-c))
    o_ref[0] = _dot3(s, w_ref[0]) + b_ref[0]


def _ada(cvec, ada_w, ada_b):
    depth, d, n6 = ada_w.shape
    rows = cvec.shape[0]
    return pl.pallas_call(
        _ada_kernel,
        grid=(depth, n6 // ADA_TN),
        in_specs=[
            pl.BlockSpec((rows, d), lambda i, j: (0, 0)),
            pl.BlockSpec((1, d, ADA_TN), lambda i, j: (i, 0, j)),
            pl.BlockSpec((1, 1, ADA_TN), lambda i, j: (i, 0, j)),
        ],
        out_specs=pl.BlockSpec((1, rows, ADA_TN), lambda i, j: (i, 0, j)),
        out_shape=jax.ShapeDtypeStruct((depth, rows, n6), F32),
        compiler_params=_cparams("parallel", "parallel"),
        name="ada",
    )(cvec, ada_w, ada_b.reshape(depth, 1, n6))


def _qkv_kernel(x_ref, sh_ref, sc_ref, g_ref, wd_ref, gq_ref, gkv_ref, wqa_ref, wqb_ref, wk_ref,
                wvt_ref, cos_ref, sin_ref, cost_ref, sint_ref, q_ref, k_ref, vt_ref, *, qscale):
    h = _rms(x_ref[0], g_ref[...]) * (1.0 + sc_ref[0]) + sh_ref[0]
    t = _dot(h.astype(BF16), wd_ref[...])
    cq = _rms(t[:, :Q_RANK], gq_ref[...]).astype(BF16)
    ckv = _rms(t[:, Q_RANK:Q_RANK + KV_RANK], gkv_ref[...]).astype(BF16)
    cos = cos_ref[...]
    sin = sin_ref[...]
    o = Q_RANK + KV_RANK
    kr = t[:, o:o + LANES] * cos + t[:, o + LANES:o + 2 * LANES] * sin
    nt = (((1,), (1,)), ((), ()))
    qa = lax.dot_general(wqa_ref[...], cq, nt, preferred_element_type=F32)
    qb = lax.dot_general(wqb_ref[...], cq, nt, preferred_element_type=F32)
    kn = _dot(ckv, wk_ref[...])
    cost = cost_ref[...]
    sint = sint_ref[...]
    for hd in range(MLA_HEADS):
        sl = slice(hd * HEAD_PAD, (hd + 1) * HEAD_PAD)
        q_ref[0, sl, :] = ((qa[sl] * cost + qb[sl] * sint) * qscale).astype(BF16)
        k_ref[0, hd] = (kn[:, sl] + kr).astype(BF16)
    vt = lax.dot_general(wvt_ref[...], ckv, nt, preferred_element_type=F32)
    tm = vt.shape[1]
    vt = vt.reshape(MLA_HEADS, VT_ROWS, tm)
    ones = lax.broadcasted_iota(jnp.int32, vt.shape, 1) == ONES_ROW
    vt_ref[0, 0] = jnp.where(ones, 1.0, vt).astype(BF16)


def _qkv(x, sh, sc, g, w, cos_t, sin_t, tm):
    b, l, d = x.shape
    nt = l // tm
    hw = MLA_HEADS * HEAD_PAD
    qscale = LOG2E / math.sqrt(QK_NOPE + QK_ROPE)
    full = lambda a: pl.BlockSpec(a.shape, lambda bi, i: (0,) * a.ndim)
    return pl.pallas_call(
        functools.partial(_qkv_kernel, qscale=qscale),
        grid=(b, nt),
        in_specs=[
            pl.BlockSpec((1, tm, d), lambda bi, i: (bi, i, 0)),
            pl.BlockSpec((1, 1, d), lambda bi, i: (bi, 0, 0)),
            pl.BlockSpec((1, 1, d), lambda bi, i: (bi, 0, 0)),
            full(g), full(w["wd"]), full(w["gq"]), full(w["gkv"]), full(w["wqa"]), full(w["wqb"]),
            full(w["wk"]), full(w["wvt"]),
            pl.BlockSpec((tm, LANES), lambda bi, i: (i, 0)),
            pl.BlockSpec((tm, LANES), lambda bi, i: (i, 0)),
            pl.BlockSpec((LANES, tm), lambda bi, i: (0, i)),
            pl.BlockSpec((LANES, tm), lambda bi, i: (0, i)),
        ],
        out_specs=[
            pl.BlockSpec((1, hw, tm), lambda bi, i: (bi, 0, i)),
            pl.BlockSpec((1, MLA_HEADS, tm, HEAD_PAD), lambda bi, i: (bi, 0, i, 0)),
            pl.BlockSpec((1, 1, MLA_HEADS, VT_ROWS, tm), lambda bi, i: (bi, i, 0, 0, 0)),
        ],
        out_shape=[
            jax.ShapeDtypeStruct((b, hw, l), BF16),
            jax.ShapeDtypeStruct((b, MLA_HEADS, l, HEAD_PAD), BF16),
            jax.ShapeDtypeStruct((b, nt, MLA_HEADS, VT_ROWS, tm), BF16),
        ],
        compiler_params=_cparams("parallel", "parallel"),
        name="mla_qkv",
    )(x, sh, sc, g, w["wd"], w["gq"], w["gkv"], w["wqa"], w["wqb"], w["wk"], w["wvt"], cos_t, sin_t, cos_t.T, sin_t.T)


def _mla_weights(w_dq, g_q, w_uq, w_dkv, g_kv, w_ukv):
    d = w_dq.shape[0]
    hq = QK_NOPE + QK_ROPE
    half = QK_ROPE // 2
    w_rope = w_dkv[:, KV_RANK:]
    w_rope_sw = jnp.concatenate([w_rope[:, half:], w_rope[:, :half]], axis=1)
    zl = jnp.zeros((d, QK_NOPE), F32)
    zr = jnp.zeros((d, HEAD_PAD - hq), F32)
    wd = jnp.concatenate([w_dq, w_dkv[:, :KV_RANK], zl, w_rope, zr, zl, w_rope_sw, zr], axis=1)
    wq = w_uq.reshape(Q_RANK, MLA_HEADS, hq)
    zq = jnp.zeros((Q_RANK, MLA_HEADS, HEAD_PAD - hq), F32)
    wqa = jnp.concatenate([wq, zq], axis=2).reshape(Q_RANK, MLA_HEADS * HEAD_PAD)
    zn = jnp.zeros((Q_RANK, MLA_HEADS, QK_NOPE), F32)
    wqb = jnp.concatenate([zn, wq[:, :, QK_NOPE + half:], wq[:, :, QK_NOPE:QK_NOPE + half], zq],
                          axis=2).reshape(Q_RANK, MLA_HEADS * HEAD_PAD)
    wkv = w_ukv.reshape(KV_RANK, MLA_HEADS, QK_NOPE + V_DIM)
    zk = jnp.zeros((KV_RANK, MLA_HEADS, HEAD_PAD - QK_NOPE), F32)
    wk = jnp.concatenate([wkv[:, :, :QK_NOPE], zk], axis=2).reshape(KV_RANK, MLA_HEADS * HEAD_PAD)
    wv = jnp.transpose(wkv[:, :, QK_NOPE:], (1, 2, 0))
    wvt = jnp.concatenate([wv, jnp.zeros((MLA_HEADS, VT_ROWS - V_DIM, KV_RANK), F32)], axis=1)
    return {
        "wd": wd.astype(BF16), "gq": g_q.reshape(1, -1), "gkv": g_kv.reshape(1, -1),
        "wqa": wqa.T.astype(BF16), "wqb": wqb.T.astype(BF16), "wk": wk.astype(BF16),
        "wvt": wvt.reshape(MLA_HEADS * VT_ROWS, KV_RANK).astype(BF16),
    }


def _rope_tables(l):
    t = jnp.arange(l)
    row = (t // GRID_W).astype(F32)
    col = (t % GRID_W).astype(F32)
    n_freq = QK_ROPE // 4
    inv = ROPE_BASE ** (-jnp.arange(n_freq, dtype=F32) / n_freq)
    ang = jnp.concatenate([row[:, None] * inv, col[:, None] * inv], axis=-1)
    cos, sin = jnp.cos(ang), jnp.sin(ang)
    ones = jnp.ones((l, QK_NOPE), F32)
    zl = jnp.zeros((l, QK_NOPE), F32)
    zr = jnp.zeros((l, HEAD_PAD - QK_NOPE - QK_ROPE), F32)
    cos_t = jnp.concatenate([ones, cos, cos, zr], axis=1)
    sin_t = jnp.concatenate([zl, -sin, sin, zr], axis=1)
    return cos_t, sin_t


def _no_rope_tables(l):
    keep = jnp.concatenate([jnp.ones((l, QK_NOPE + QK_ROPE), F32),
                            jnp.zeros((l, HEAD_PAD - QK_NOPE - QK_ROPE), F32)], axis=1)
    return keep, jnp.zeros((l, HEAD_PAD), F32)


def _attn_ctx_kernel(q_ref, kc_ref, vtc_ref, o_ref):
    outs = []
    for hd in range(2):
        s = _dot(kc_ref[0, hd], q_ref[0, hd * HEAD_PAD:(hd + 1) * HEAD_PAD, :])
        p = jnp.exp2(s - jnp.max(s, axis=0, keepdims=True)).astype(BF16)
        acc = _dot(vtc_ref[0, 0, hd], p)
        outs.append(acc[:V_DIM] / acc[ONES_ROW:ONES_ROW + 1])
    o_ref[0] = jnp.concatenate(outs, axis=0).T.astype(BF16)


def _attn_kernel(q_ref, kc_ref, vtc_ref, k_ref, vt_ref, o_ref, *scr, n_lat_chunks, tk, unroll):
    heads = (0, 1)
    s_scr = [scr[0:2], scr[2:4]]
    p_scr = [scr[4:6], scr[6:8]]
    acc_scr = scr[8:10]
    tq = q_ref.shape[2]
    n_pos = n_lat_chunks + 1

    def stage_a(hd, slot, p):
        if isinstance(p, int) and p == 0:
            k = kc_ref[0, hd]
        else:
            start = (p - 1) * tk
            k = k_ref[0, hd, pl.ds(start if isinstance(p, int) else pl.multiple_of(start, tk), tk), :]
        s = _dot(k, q_ref[0, hd * HEAD_PAD:(hd + 1) * HEAD_PAD, :])
        s_scr[hd][slot][...] = s
        return jnp.max(s.reshape(tk // SUBLANES, SUBLANES, tq), axis=0)

    def stage_b(hd, slot, cm, m_old):
        m_new = jnp.maximum(m_old, jnp.max(cm, axis=0, keepdims=True))
        p_scr[hd][slot][...] = jnp.exp2(s_scr[hd][slot][...] - m_new).astype(BF16)
        return m_new, jnp.exp2(m_old - m_new)

    def stage_c(hd, slot, p, alpha):
        vt = vtc_ref[0, 0, hd] if isinstance(p, int) and p == 0 else vt_ref[0, p - 1, hd]
        acc_scr[hd][...] = acc_scr[hd][...] * alpha + _dot(vt, p_scr[hd][slot][...])

    carry = []
    for hd in heads:
        acc_scr[hd][...] = jnp.zeros(acc_scr[hd].shape, F32)
        cm0 = stage_a(hd, 0, 0)
        cm1 = stage_a(hd, 1, 1)
        m, al = stage_b(hd, 0, cm0, jnp.full((1, tq), -jnp.inf, F32))
        carry += [m, al, cm1]

    def steps(carry, p0, parity, count):
        carry = list(carry)
        for u in range(count):
            slot = (parity + u) % 2
            for hd in heads:
                m, al, cm = carry[3 * hd:3 * hd + 3]
                stage_c(hd, slot, p0 + u, al)
                m, al = stage_b(hd, 1 - slot, cm, m)
                cm = stage_a(hd, slot, p0 + u + 2)
                carry[3 * hd:3 * hd + 3] = [m, al, cm]
        return tuple(carry)

    n_steps = n_pos - 2
    peel = n_steps % unroll or min(unroll, n_steps)
    carry = steps(carry, 0, 0, peel)
    carry = lax.fori_loop(0, (n_steps - peel) // unroll,
                          lambda j, cr: steps(cr, peel + unroll * j, peel % 2, unroll), carry)
    outs = []
    last = (n_pos - 2) % 2
    for hd in heads:
        m, al, cm = carry[3 * hd:3 * hd + 3]
        stage_c(hd, last, n_pos - 2, al)
        m, al = stage_b(hd, 1 - last, cm, m)
        stage_c(hd, 1 - last, n_pos - 1, al)
        acc = acc_scr[hd][...]
        outs.append(acc[:V_DIM] / acc[ONES_ROW:ONES_ROW + 1])
    o_ref[0] = jnp.concatenate(outs, axis=0).T.astype(BF16)


def _attention(q, kc, vtc, k=None, vt=None, *, tq):
    b, _, lq = q.shape
    cl = kc.shape[2]
    nq = lq // tq
    hp = 2 * HEAD_PAD
    in_specs = [
        pl.BlockSpec((1, hp, tq), lambda bi, h, i: (bi, h, i)),
        pl.BlockSpec((1, 2, cl, HEAD_PAD), lambda bi, h, i: (bi, h, 0, 0)),
        pl.BlockSpec((1, 1, 2, VT_ROWS, cl), lambda bi, h, i: (bi, 0, h, 0, 0)),
    ]
    args = [q, kc, vtc]
    n_chunks = 0
    scratch = []
    body = _attn_ctx_kernel
    if k is not None:
        lk = k.shape[2]
        n_chunks, tk = vt.shape[1], vt.shape[4]
        assert n_chunks >= 2 and cl == tk
        unroll = max(2, min(ATTN_UNROLL, (n_chunks - 2) // 2 * 2))
        body = functools.partial(_attn_kernel, n_lat_chunks=n_chunks, tk=tk, unroll=unroll)
        in_specs += [
            pl.BlockSpec((1, 2, lk, HEAD_PAD), lambda bi, h, i: (bi, h, 0, 0)),
            pl.BlockSpec((1, n_chunks, 2, VT_ROWS, tk), lambda bi, h, i: (bi, 0, h, 0, 0)),
        ]
        args += [k, vt]
        scratch = ([pltpu.VMEM((tk, tq), F32)] * 4 + [pltpu.VMEM((tk, tq), BF16)] * 4
                   + [pltpu.VMEM((VT_ROWS, tq), F32)] * 2)
    return pl.pallas_call(
        body,
        grid=(b, MLA_HEADS // 2, nq),
        in_specs=in_specs,
        out_specs=pl.BlockSpec((1, tq, 2 * V_DIM), lambda bi, h, i: (bi, i, h)),
        out_shape=jax.ShapeDtypeStruct((b, lq, MLA_HEADS * V_DIM), BF16),
        scratch_shapes=scratch,
        compiler_params=_cparams("parallel", "parallel", "arbitrary"),
        name="attention" if n_chunks else "attention_ctx",
    )(*args)


def _mlp_kernel(a_ref, wo_ref, bo_ref, gt0_ref, g0_ref, x_ref, sh_ref, sc_ref, gt_ref, g1_ref, g2_ref,
                w1_ref, w2_ref, o_ref):
    y = _dot(a_ref[0], wo_ref[...]) + bo_ref[...]
    x = x_ref[0] + gt0_ref[0] * _rms(y, g0_ref[...])
    h = (_rms(x, g1_ref[...]) * (1.0 + sc_ref[0]) + sh_ref[0]).astype(BF16)
    dff = w1_ref.shape[1]
    m = None
    for c in range(dff // MLP_FF_CHUNK):
        cols = slice(c * MLP_FF_CHUNK, (c + 1) * MLP_FF_CHUNK)
        u = jnp.maximum(_dot(h, w1_ref[:, cols]), 0.0)
        part = _dot((u * u).astype(BF16), w2_ref[cols, :])
        m = part if m is None else m + part
    o_ref[0] = x + gt_ref[0] * _rms(m, g2_ref[...])


def _mixer_out_mlp(a, wo, bo, gt0, g0, x, sh, sc, gt, g1, g2, w1, w2, tm):
    b, l, d = x.shape
    din = a.shape[2]
    dff = w1.shape[1]
    vec = pl.BlockSpec((1, 1, d), lambda bi, i: (bi, 0, 0))
    row = pl.BlockSpec((1, d), lambda bi, i: (0, 0))
    const = lambda shape: pl.BlockSpec(shape, lambda bi, i: (0, 0), pipeline_mode=pl.Buffered(1))
    return pl.pallas_call(
        _mlp_kernel,
        grid=(b, l // tm),
        in_specs=[
            pl.BlockSpec((1, tm, din), lambda bi, i: (bi, i, 0)),
            const((din, d)), row, vec, row,
            pl.BlockSpec((1, tm, d), lambda bi, i: (bi, i, 0)),
            vec, vec, vec, row, row,
            const((d, dff)), const((dff, d)),
        ],
        out_specs=pl.BlockSpec((1, tm, d), lambda bi, i: (bi, i, 0)),
        out_shape=jax.ShapeDtypeStruct((b, l, d), F32),
        compiler_params=_cparams("parallel", "parallel"),
        name="mixer_out_mlp",
    )(a, wo, bo, gt0, g0, x, sh, sc, gt, g1, g2, w1, w2)


def _hy_in_kernel(x_ref, xp_ref, xn_ref, sh_ref, sc_ref, g_ref, w_ref, b_ref, ws_ref, bs_ref,
                  x1_ref, x2_ref, v_ref, u_scr):
    i = pl.program_id(1)
    n = pl.num_programs(1)
    tm = x_ref.shape[1]
    d = x_ref.shape[2]
    xa = jnp.concatenate([xp_ref[0], x_ref[0], xn_ref[0]], axis=0)
    h = (_rms(xa, g_ref[...]) * (1.0 + sc_ref[0]) + sh_ref[0]).astype(BF16)
    u_scr[...] = _dot(h, w_ref[...]) + b_ref[...]

    @pl.when(i == 0)
    def _():
        u_scr[0:HALO, :] = jnp.zeros((HALO, 3 * d), F32)

    @pl.when(i == n - 1)
    def _():
        u_scr[tm + HALO:tm + 2 * HALO, :] = jnp.zeros((HALO, 3 * d), F32)

    rows = tm + 2 * HALO
    for j, o_ref in enumerate((x1_ref, x2_ref, v_ref)):
        sl = slice(j * d, (j + 1) * d)
        u = u_scr[:, sl]
        prev = pltpu.roll(u, 1, axis=0)[HALO:HALO + tm]
        nxt = pltpu.roll(u, rows - 1, axis=0)[HALO:HALO + tm]
        y = (prev * ws_ref[0:1, sl] + u[HALO:HALO + tm] * ws_ref[1:2, sl] + nxt * ws_ref[2:3, sl]
             + bs_ref[:, sl])
        o_ref[0] = y.astype(BF16)


def _hy_in(x, sh, sc, g, w_in, b_in, w_short, b_short, tm):
    b, l, d = x.shape
    nt = l // tm
    tb = tm // HALO
    nb = l // HALO
    vec = pl.BlockSpec((1, 1, d), lambda bi, i: (bi, 0, 0))
    out = pl.BlockSpec((1, tm, d), lambda bi, i: (bi, i, 0))
    return pl.pallas_call(
        _hy_in_kernel,
        grid=(b, nt),
        in_specs=[
            pl.BlockSpec((1, tm, d), lambda bi, i: (bi, i, 0)),
            pl.BlockSpec((1, HALO, d), lambda bi, i: (bi, jnp.maximum(i * tb - 1, 0), 0)),
            pl.BlockSpec((1, HALO, d), lambda bi, i: (bi, jnp.minimum((i + 1) * tb, nb - 1), 0)),
            vec, vec,
            pl.BlockSpec((1, d), lambda bi, i: (0, 0)),
            pl.BlockSpec((d, 3 * d), lambda bi, i: (0, 0)),
            pl.BlockSpec((1, 3 * d), lambda bi, i: (0, 0)),
            pl.BlockSpec((3, 3 * d), lambda bi, i: (0, 0)),
            pl.BlockSpec((1, 3 * d), lambda bi, i: (0, 0)),
        ],
        out_specs=[out, out, out],
        out_shape=[jax.ShapeDtypeStruct((b, l, d), BF16)] * 3,
        scratch_shapes=[pltpu.VMEM((tm + 2 * HALO, 3 * d), F32)],
        compiler_params=_cparams("parallel", "parallel"),
        name="hyena_in",
    )(x, x, x, sh, sc, g, w_in, b_in, w_short, b_short)


def _filt_kernel(z_ref, w1_ref, b1_ref, w2_ref, b2_ref, w3_ref, fr_ref, dl_ref, o_ref, *, l):
    tl = z_ref.shape[0]
    d = dl_ref.shape[1]
    fr = fr_ref[...]
    hdn = jnp.sin(fr * (_dot3(z_ref[...], w1_ref[...]) + b1_ref[...]))
    hdn = jnp.sin(fr * (_dot3(hdn, w2_ref[...]) + b2_ref[...]))
    hf = _dot3(hdn, w3_ref[...])
    rows = pl.program_id(0) * tl + lax.broadcasted_iota(jnp.int32, (tl, d), 0)
    t = rows.astype(F32) * (1.0 / (l - 1))
    decay = jnp.exp(-t * jnp.abs(dl_ref[...]))
    for j in range(2 * HY_ORDER):
        f = hf[:, j * d:(j + 1) * d] * decay
        if j % 2 == 1:
            f = jnp.where(rows == 0, 0.0, f)
        o_ref[j] = f.astype(BF16)


def _filters(l, d, f_w1, f_b1, f_w2, f_b2, f_w3, f_freq):
    t = jnp.linspace(0.0, 1.0, l, dtype=F32)[:, None]
    w = (2.0 * math.pi) * jnp.arange(l, dtype=F32)[:, None] / l
    f = jnp.linspace(1e-4, HY_BANDS - 1, HY_BANDS, dtype=F32)[None, :]
    kz = HY_FILTER_HIDDEN
    z = jnp.concatenate([t, jnp.cos(f * w), -jnp.sin(f * w), jnp.zeros((l, kz - HY_EMB), F32)], axis=-1)
    w1 = jnp.concatenate([f_w1, jnp.zeros((kz - HY_EMB, HY_FILTER_HIDDEN), F32)], axis=0)
    min_decay = math.log(HY_TARGET) / HY_SLOW_PCT
    max_decay = math.log(HY_TARGET) / HY_FAST_PCT
    deltas = jnp.linspace(min_decay, max_decay, d, dtype=F32)[None, :]
    tl = FILTER_ROWS
    full = lambda a: pl.BlockSpec(a.shape, lambda i: (0,) * a.ndim)
    ops = [w1, f_b1.reshape(1, -1), f_w2, f_b2.reshape(1, -1), f_w3, f_freq.reshape(1, -1), deltas]
    return pl.pallas_call(
        functools.partial(_filt_kernel, l=l),
        grid=(l // tl,),
        in_specs=[pl.BlockSpec((tl, kz), lambda i: (i, 0))] + [full(a) for a in ops],
        out_specs=pl.BlockSpec((2 * HY_ORDER, tl, d), lambda i: (0, i, 0)),
        out_shape=jax.ShapeDtypeStruct((2 * HY_ORDER, l, d), BF16),
        compiler_params=_cparams("parallel"),
        name="hyena_filters",
    )(z, *ops)


def _dft_tables(l):
    n = 2 * l
    n2 = DFT_N2
    n1 = n // n2
    n1h = n1 // 2
    slots = -(-(n1h + 1) // SLOT_BLOCK) * SLOT_BLOCK
    k1 = jnp.arange(slots)
    valid = (k1 <= n1h)[:, None]
    a = jnp.arange(n1h)
    ang1 = (2.0 * math.pi / n1) * ((k1[:, None] * a[None, :]) % n1).astype(F32)
    f1 = jnp.concatenate([jnp.where(valid, jnp.cos(ang1), 0.0), jnp.where(valid, -jnp.sin(ang1), 0.0)], axis=0)
    kk = jnp.arange(n1h)
    wgt = jnp.where(kk == 0, 1.0, 2.0)[None, :] / n
    ang2 = (2.0 * math.pi / n1) * ((a[:, None] * kk[None, :]) % n1).astype(F32)
    f2 = jnp.concatenate([wgt * jnp.cos(ang2), -wgt * jnp.sin(ang2)], axis=1)
    k2 = jnp.arange(n2)
    j = jnp.arange(n2)
    prod = (j[None, None, :] * (k1[:, None, None] + n1 * k2[None, :, None])) % n
    th = (2.0 * math.pi / n) * prod.astype(F32)
    live = valid[:, :, None]
    c, s = jnp.where(live, jnp.cos(th), 0.0), jnp.where(live, jnp.sin(th), 0.0)
    g = jnp.concatenate([jnp.concatenate([c, s], axis=2), jnp.concatenate([-s, c], axis=2)], axis=1)
    ct, st = jnp.swapaxes(c, 1, 2), jnp.swapaxes(s, 1, 2)
    gi = jnp.concatenate([jnp.concatenate([ct, -st], axis=2), jnp.concatenate([st, ct], axis=2)], axis=1)
    return {"n": n, "n1h": n1h, "slots": slots,
            "f1": f1.astype(BF16), "f2": f2.astype(BF16), "g": g.astype(BF16), "gi": gi.astype(BF16)}


def _swap_rows(x):
    return pltpu.einshape("abc->bac", x)


def _stage1_kernel(f_ref, u_ref, a_ref, xs_scr):
    jb = pl.program_id(2)
    n2, n1h, td = xs_scr.shape

    @pl.when(jb == 0)
    def _():
        xs_scr[...] = _swap_rows(u_ref[0].reshape(n1h, n2, td))

    for jj in range(J_BLOCK):
        a_ref[0, jj] = _dot(f_ref[...], xs_scr[jb * J_BLOCK + jj]).astype(BF16)


def _stage1(u, tab):
    b, l, d = u.shape
    n1h, slots = tab["n1h"], tab["slots"]
    n2 = DFT_N2
    td = STAGE1_TD
    return pl.pallas_call(
        _stage1_kernel,
        grid=(b, d // td, n2 // J_BLOCK),
        in_specs=[
            pl.BlockSpec((2 * slots, n1h), lambda bi, ci, jb: (0, 0)),
            pl.BlockSpec((1, l, td), lambda bi, ci, jb: (bi, 0, ci)),
        ],
        out_specs=pl.BlockSpec((1, J_BLOCK, 2 * slots, td), lambda bi, ci, jb: (bi, jb, 0, ci)),
        out_shape=jax.ShapeDtypeStruct((b, n2, 2 * slots, d), BF16),
        scratch_shapes=[pltpu.VMEM((n2, n1h, td), BF16)],
        compiler_params=_cparams("parallel", "parallel", "arbitrary"),
        name="dft_stage1",
    )(tab["f1"], u)


def _slot_rows(ref):
    n2, _, sb, td = ref.shape
    return _swap_rows(ref[...].reshape(n2, 2 * sb, td))


def _filt_spec_kernel(g_ref, af_ref, ab_ref, bias_ref, k_ref, f_scr, b_scr):
    n2 = f_scr.shape[1]
    f_scr[...] = _slot_rows(af_ref.at[0])
    b_scr[...] = _slot_rows(ab_ref.at[0])
    bias = bias_ref[0]

    def body(s, carry):
        g = g_ref[s]
        xf = _dot(g, jnp.concatenate([f_scr[s], f_scr[SLOT_BLOCK + s]], axis=0))
        xb = _dot(g, jnp.concatenate([b_scr[s], b_scr[SLOT_BLOCK + s]], axis=0))
        k_ref[0, 0, s] = (xf[:n2] + xb[:n2] + bias).astype(BF16)
        k_ref[0, 1, s] = (xf[n2:] - xb[n2:]).astype(BF16)
        return carry

    lax.fori_loop(0, SLOT_BLOCK, body, 0, unroll=16)


def _filt_spectrum(a_filt, bias, tab):
    _, n2, _, d = a_filt.shape
    slots = tab["slots"]
    td = MID_TD
    a5 = a_filt.reshape(a_filt.shape[0], n2, 2, slots, d)
    blk = lambda seq: pl.BlockSpec((1, n2, 2, SLOT_BLOCK, td), lambda o, kb, ci: (2 * o + seq, 0, 0, kb, ci))
    return pl.pallas_call(
        _filt_spec_kernel,
        grid=(HY_ORDER, slots // SLOT_BLOCK, d // td),
        in_specs=[
            pl.BlockSpec((SLOT_BLOCK, 2 * n2, 2 * n2), lambda o, kb, ci: (kb, 0, 0)),
            blk(0), blk(1),
            pl.BlockSpec((1, 1, td), lambda o, kb, ci: (o, 0, ci)),
        ],
        out_specs=pl.BlockSpec((1, 2, SLOT_BLOCK, n2, td), lambda o, kb, ci: (o, 0, kb, 0, ci)),
        out_shape=jax.ShapeDtypeStruct((HY_ORDER, 2, slots, n2, d), BF16),
        scratch_shapes=[pltpu.VMEM((2 * SLOT_BLOCK, n2, td), BF16)] * 2,
        compiler_params=_cparams("parallel", "parallel", "parallel"),
        name="hyena_filter_spectrum",
    )(tab["g"], a5, a5, bias.reshape(HY_ORDER, 1, d))


def _mid_kernel(g_ref, gi_ref, a_ref, k_ref, z_ref, a_scr, z_scr):
    n2 = a_scr.shape[1]
    a_scr[...] = _slot_rows(a_ref.at[0])

    def body(s, carry):
        x = _dot(g_ref[s], jnp.concatenate([a_scr[s], a_scr[SLOT_BLOCK + s]], axis=0))
        xr, xi = x[:n2], x[n2:]
        kr = k_ref[0, 0, s].astype(F32)
        ki = k_ref[0, 1, s].astype(F32)
        y = jnp.concatenate([xr * kr - xi * ki, xr * ki + xi * kr], axis=0).astype(BF16)
        z = _dot(gi_ref[s], y)
        z_scr[s] = z[:n2].astype(BF16)
        z_scr[SLOT_BLOCK + s] = z[n2:].astype(BF16)
        return carry

    lax.fori_loop(0, SLOT_BLOCK, body, 0, unroll=16)
    td = z_scr.shape[2]
    z_ref[0] = _swap_rows(z_scr[...]).reshape(n2, 2, SLOT_BLOCK, td)


def _mid(a, kf, order, tab):
    b, n2, _, d = a.shape
    slots = tab["slots"]
    td = MID_TD
    a5 = a.reshape(b, n2, 2, slots, d)
    gspec = pl.BlockSpec((SLOT_BLOCK, 2 * n2, 2 * n2), lambda bi, kb, ci: (kb, 0, 0))
    aspec = pl.BlockSpec((1, n2, 2, SLOT_BLOCK, td), lambda bi, kb, ci: (bi, 0, 0, kb, ci))
    z = pl.pallas_call(
        _mid_kernel,
        grid=(b, slots // SLOT_BLOCK, d // td),
        in_specs=[
            gspec, gspec, aspec,
            pl.BlockSpec((1, 2, SLOT_BLOCK, n2, td), lambda bi, kb, ci: (order, 0, kb, 0, ci)),
        ],
        out_specs=aspec,
        out_shape=jax.ShapeDtypeStruct((b, n2, 2, slots, d), BF16),
        scratch_shapes=[pltpu.VMEM((2 * SLOT_BLOCK, n2, td), BF16)] * 2,
        compiler_params=_cparams("parallel", "parallel", "parallel"),
        name="dft_mid",
    )(tab["g"], tab["gi"], a5, kf)
    return z.reshape(b, n2, 2 * slots, d)


def _stage2_kernel(f_ref, z_ref, x_ref, o_ref, ys_scr, *, slots, inv_n):
    jb = pl.program_id(2)
    n2, n1h, td = ys_scr.shape
    rows = lax.broadcasted_iota(jnp.int32, (n1h, td), 0)
    for jj in range(J_BLOCK):
        z = z_ref[0, jj]
        y = _dot(f_ref[...], jnp.concatenate([z[0:n1h], z[slots:slots + n1h]], axis=0))
        nyq = z[n1h:n1h + 1].astype(F32) * inv_n
        ys_scr[jb * J_BLOCK + jj] = (y + jnp.where(rows % 2 == 0, nyq, -nyq)).astype(BF16)

    @pl.when(jb == pl.num_programs(2) - 1)
    def _():
        o_ref[0] = _swap_rows(ys_scr[...]).reshape(n1h * n2, td) * x_ref[0]


def _stage2(z, gate, tab):
    b, n2, _, d = z.shape
    n1h, slots = tab["n1h"], tab["slots"]
    l = n1h * n2
    td = STAGE2_TD
    return pl.pallas_call(
        functools.partial(_stage2_kernel, slots=slots, inv_n=1.0 / tab["n"]),
        grid=(b, d // td, n2 // J_BLOCK),
        in_specs=[
            pl.BlockSpec((n1h, 2 * n1h), lambda bi, ci, jb: (0, 0)),
            pl.BlockSpec((1, J_BLOCK, 2 * slots, td), lambda bi, ci, jb: (bi, jb, 0, ci)),
            pl.BlockSpec((1, l, td), lambda bi, ci, jb: (bi, 0, ci)),
        ],
        out_specs=pl.BlockSpec((1, l, td), lambda bi, ci, jb: (bi, 0, ci)),
        out_shape=jax.ShapeDtypeStruct((b, l, d), BF16),
        scratch_shapes=[pltpu.VMEM((n2, n1h, td), BF16)],
        compiler_params=_cparams("parallel", "parallel", "arbitrary"),
        name="dft_stage2",
    )(tab["f2"], z, gate)


def _small_dft_tables(l):
    n = 2 * l
    k = jnp.arange(l)
    t = jnp.arange(l)
    ang = (2.0 * math.pi / n) * ((k[:, None] * t[None, :]) % n).astype(F32)
    sign = jnp.where(t % 2 == 0, 1.0, -1.0)[None, :]
    pad = jnp.zeros((SUBLANES - 1, l), F32)
    fwd = jnp.concatenate([jnp.cos(ang), -jnp.sin(ang), sign, pad], axis=0)
    wgt = jnp.where(k == 0, 1.0, 2.0)[None, :] / n
    ang_t = ang.T
    inv = jnp.concatenate([wgt * jnp.cos(ang_t), -wgt * jnp.sin(ang_t)], axis=1)
    return {"n": n, "fwd": fwd.astype(BF16), "inv": inv.astype(BF16)}


def _small_spec_kernel(f_ref, hf_ref, hb_ref, bias_ref, k_ref):
    l = hf_ref.shape[1]
    xf = _dot(f_ref[...], hf_ref[0])
    xb = _dot(f_ref[...], hb_ref[0])
    bias = bias_ref[0]
    k_ref[0, 0:l, :] = (xf[:l] + xb[:l] + bias).astype(BF16)
    k_ref[0, l:2 * l, :] = (xf[l:2 * l] - xb[l:2 * l]).astype(BF16)
    k_ref[0, 2 * l:, :] = (xf[2 * l:] + xb[2 * l:] + bias).astype(BF16)


def _small_filt_spectrum(filt, bias, tab):
    _, l, d = filt.shape
    rows = 2 * l + SUBLANES
    tc = SMALL_TD
    return pl.pallas_call(
        _small_spec_kernel,
        grid=(HY_ORDER, d // tc),
        in_specs=[
            pl.BlockSpec((rows, l), lambda o, j: (0, 0)),
            pl.BlockSpec((1, l, tc), lambda o, j: (2 * o, 0, j)),
            pl.BlockSpec((1, l, tc), lambda o, j: (2 * o + 1, 0, j)),
            pl.BlockSpec((1, 1, tc), lambda o, j: (o, 0, j)),
        ],
        out_specs=pl.BlockSpec((1, rows, tc), lambda o, j: (o, 0, j)),
        out_shape=jax.ShapeDtypeStruct((HY_ORDER, rows, d), BF16),
        compiler_params=_cparams("parallel", "parallel"),
        name="hyena_filter_spectrum_ctx",
    )(tab["fwd"], filt, filt, bias.reshape(HY_ORDER, 1, d))


def _small_conv_kernel(f_ref, fi_ref, u_ref, k_ref, x_ref, o_ref, *, inv_n):
    l = u_ref.shape[1]
    x = _dot(f_ref[...], u_ref[0])
    xr, xi = x[:l], x[l:2 * l]
    kr = k_ref[0, 0:l, :].astype(F32)
    ki = k_ref[0, l:2 * l, :].astype(F32)
    y = jnp.concatenate([xr * kr - xi * ki, xr * ki + xi * kr], axis=0).astype(BF16)
    out = _dot(fi_ref[...], y)
    nyq = x[2 * l:2 * l + 1] * k_ref[0, 2 * l:2 * l + 1, :].astype(F32) * inv_n
    rows = lax.broadcasted_iota(jnp.int32, out.shape, 0)
    out = out + jnp.where(rows % 2 == 0, nyq, -nyq)
    o_ref[0] = (out * x_ref[0].astype(F32)).astype(BF16)


def _small_conv(u, kf, order, gate, tab):
    b, l, d = u.shape
    rows = 2 * l + SUBLANES
    tc = SMALL_TD
    return pl.pallas_call(
        functools.partial(_small_conv_kernel, inv_n=1.0 / tab["n"]),
        grid=(b, d // tc),
        in_specs=[
            pl.BlockSpec((rows, l), lambda bi, j: (0, 0)),
            pl.BlockSpec((l, 2 * l), lambda bi, j: (0, 0)),
            pl.BlockSpec((1, l, tc), lambda bi, j: (bi, 0, j)),
            pl.BlockSpec((1, rows, tc), lambda bi, j: (order, 0, j)),
            pl.BlockSpec((1, l, tc), lambda bi, j: (bi, 0, j)),
        ],
        out_specs=pl.BlockSpec((1, l, tc), lambda bi, j: (bi, 0, j)),
        out_shape=jax.ShapeDtypeStruct((b, l, d), BF16),
        compiler_params=_cparams("parallel", "parallel"),
        name="long_conv_ctx",
    )(tab["fwd"], tab["inv"], u, kf, gate)


def _tile(l, want):
    while l % want:
        want //= 2
    return want


def _hyena_mixer_lat(x, sh, sc, g, p, tabs):
    b, l, d = x.shape
    x1, x2, v = _hy_in(x, sh, sc, g, p["w_in"], p["b_in"], p["w_short"], p["b_short"], _tile(l, ROW_TILE))
    filt = _filters(l, d, *p["filter"])
    kf = _filt_spectrum(_stage1(filt, tabs), p["bias"], tabs)
    z = _stage2(_mid(_stage1(v, tabs), kf, 0, tabs), x1, tabs)
    z = _stage2(_mid(_stage1(z, tabs), kf, 1, tabs), x2, tabs)
    return z


def _hyena_mixer_ctx(x, sh, sc, g, p, tabs):
    b, l, d = x.shape
    x1, x2, v = _hy_in(x, sh, sc, g, p["w_in"], p["b_in"], p["w_short"], p["b_short"], l)
    filt = _filters(l, d, *p["filter"])
    kf = _small_filt_spectrum(filt, p["bias"], tabs)
    z = _small_conv(v, kf, 0, x1, tabs)
    z = _small_conv(z, kf, 1, x2, tabs)
    return z


def kernel(x, c, ctx, c_ctx, ada_w, ada_b, norm_g, mla_w_dq, mla_g_q, mla_w_uq, mla_w_dkv, mla_g_kv,
           mla_w_ukv, mla_w_o, hy_w_in, hy_b_in, hy_w_short, hy_b_short, hy_f_w1, hy_f_b1, hy_f_w2,
           hy_f_b2, hy_f_w3, hy_f_freq, hy_bias, hy_w_out, hy_b_out, mlp_w1, mlp_w2):
    b, l, d = x.shape
    cl = ctx.shape[1]
    depth = ada_w.shape[0]
    assert b + 1 <= SUBLANES and l % ATTN_TK == 0 and cl == ATTN_TK

    cvec = jnp.concatenate([c, c_ctx[None, :], jnp.zeros((SUBLANES - b - 1, d), F32)], axis=0)
    mods = _ada(cvec, ada_w, ada_b)

    rope_lat = _rope_tables(l)
    rope_ctx = _no_rope_tables(cl)
    tabs_lat = _dft_tables(l)
    tabs_ctx = _small_dft_tables(cl)
    zero_bias = jnp.zeros((1, d), F32)

    xc = ctx
    for i in range(depth):
        last = i == depth - 1
        j = i // 2
        g = norm_g[i].reshape(4, 1, d)
        m_lat = mods[i, :b].reshape(b, 6, 1, d)
        m_ctx = jnp.broadcast_to(mods[i, b].reshape(1, 6, 1, d), (b, 6, 1, d))
        lat = [m_lat[:, k] for k in range(6)]
        cx = [m_ctx[:, k] for k in range(6)]

        if i % 2 == 0:
            w = _mla_weights(mla_w_dq[j], mla_g_q[j], mla_w_uq[j], mla_w_dkv[j], mla_g_kv[j], mla_w_ukv[j])
            wo = mla_w_o[j].astype(BF16)
            bo = zero_bias
            qc, kc, vtc = _qkv(xc, cx[0], cx[1], g[0], w, *rope_ctx, cl)
            ql, kl, vtl = _qkv(x, lat[0], lat[1], g[0], w, *rope_lat, ATTN_TK)
            y_lat = _attention(ql, kc, vtc, kl, vtl, tq=_tile(l, ATTN_TQ))
            y_ctx = None if last else _attention(qc, kc, vtc, tq=cl)
        else:
            p = {
                "w_in": hy_w_in[j].astype(BF16), "b_in": hy_b_in[j].reshape(1, -1),
                "w_short": hy_w_short[j], "b_short": hy_b_short[j].reshape(1, -1),
                "filter": (hy_f_w1[j], hy_f_b1[j], hy_f_w2[j], hy_f_b2[j], hy_f_w3[j], hy_f_freq[j]),
                "bias": hy_bias[j],
            }
            wo = hy_w_out[j].astype(BF16)
            bo = hy_b_out[j].reshape(1, d)
            y_lat = _hyena_mixer_lat(x, lat[0], lat[1], g[0], p, tabs_lat)
            y_ctx = None if last else _hyena_mixer_ctx(xc, cx[0], cx[1], g[0], p, tabs_ctx)

        w1 = mlp_w1[i].astype(BF16)
        w2 = mlp_w2[i].astype(BF16)
        x = _mixer_out_mlp(y_lat, wo, bo, lat[2], g[1], x, lat[3], lat[4], lat[5], g[2], g[3], w1, w2,
                           _tile(l, MLP_ROWS))
        if not last:
            xc = _mixer_out_mlp(y_ctx, wo, bo, cx[2], g[1], xc, cx[3], cx[4], cx[5], g[2], g[3], w1, w2, cl)
    return x
```

```python
import functools
import math

import jax
import jax.numpy as jnp
from jax import lax
from jax.experimental import pallas as pl
from jax.experimental.pallas import tpu as pltpu

F32 = jnp.float32
BF16 = jnp.bfloat16

GRID_W = 64
MLA_HEADS = 16
QK_NOPE = 64
QK_ROPE = 32
V_DIM = 64
Q_RANK = 256
KV_RANK = 128
ROPE_BASE = 10000.0
HY_ORDER = 2
HY_EMB = 33
HY_BANDS = (HY_EMB - 1) // 2
HY_FILTER_HIDDEN = 64
HY_TARGET = 1e-2
HY_FAST_PCT = 0.3
HY_SLOW_PCT = 1.5
EPS = 1e-6

LANES = 128
SUBLANES = 8
HEAD_PAD = LANES
VT_ROWS = 80
ONES_ROW = V_DIM
VMEM_LIMIT = 56 * 1024 * 1024

ADA_TN = 1536
ATTN_TQ = 512
ATTN_TK = 256
ATTN_UNROLL = 10
MLP_ROWS = 1024
MLP_FF_CHUNK = 1024
ROW_TILE = 1024
HALO = SUBLANES
FILTER_ROWS = 256
DFT_N2 = 128
SLOT_BLOCK = 16
J_BLOCK = 32
STAGE1_TD = 256
STAGE2_TD = 256
MID_TD = 512
SMALL_TD = 512

LOG2E = 1.4426950408889634


def _cparams(*sem):
    return pltpu.CompilerParams(dimension_semantics=sem, vmem_limit_bytes=VMEM_LIMIT)


def _dot(a, b):
    return jnp.dot(a, b, preferred_element_type=F32)


def _dot3(a, b):
    ah = a.astype(BF16)
    al = (a - ah.astype(F32)).astype(BF16)
    bh = b.astype(BF16)
    bl = (b - bh.astype(F32)).astype(BF16)
    return _dot(ah, bh) + _dot(al, bh) + _dot(ah, bl)


def _rms(x, g):
    return x * lax.rsqrt(jnp.mean(x * x, axis=-1, keepdims=True) + EPS) * g


def _ada_kernel(c_ref, w_ref, b_ref, o_ref):
    c = c_ref[...]
    s = c / (1.0 + jnp.exp(-c))
    o_ref[0] = _dot3(s, w_ref[0]) + b_ref[0]


def _ada(cvec, ada_w, ada_b):
    depth, d, n6 = ada_w.shape
    rows = cvec.shape[0]
    return pl.pallas_call(
        _ada_kernel,
        grid=(depth, n6 // ADA_TN),
        in_specs=[
            pl.BlockSpec((rows, d), lambda i, j: (0, 0)),
            pl.BlockSpec((1, d, ADA_TN), lambda i, j: (i, 0, j)),
            pl.BlockSpec((1, 1, ADA_TN), lambda i, j: (i, 0, j)),
        ],
        out_specs=pl.BlockSpec((1, rows, ADA_TN), lambda i, j: (i, 0, j)),
        out_shape=jax.ShapeDtypeStruct((depth, rows, n6), F32),
        compiler_params=_cparams("parallel", "parallel"),
        name="ada",
    )(cvec, ada_w, ada_b.reshape(depth, 1, n6))


def _qkv_kernel(x_ref, sh_ref, sc_ref, g_ref, wd_ref, gq_ref, gkv_ref, wqa_ref, wqb_ref, wk_ref,
                wvt_ref, cos_ref, sin_ref, cost_ref, sint_ref, q_ref, k_ref, vt_ref, *, qscale):
    h = _rms(x_ref[0], g_ref[...]) * (1.0 + sc_ref[0]) + sh_ref[0]
    t = _dot(h.astype(BF16), wd_ref[...])
    cq = _rms(t[:, :Q_RANK], gq_ref[...]).astype(BF16)
    ckv = _rms(t[:, Q_RANK:Q_RANK + KV_RANK], gkv_ref[...]).astype(BF16)
    cos = cos_ref[...]
    sin = sin_ref[...]
    o = Q_RANK + KV_RANK
    kr = t[:, o:o + LANES] * cos + t[:, o + LANES:o + 2 * LANES] * sin
    nt = (((1,), (1,)), ((), ()))
    qa = lax.dot_general(wqa_ref[...], cq, nt, preferred_element_type=F32)
    qb = lax.dot_general(wqb_ref[...], cq, nt, preferred_element_type=F32)
    kn = _dot(ckv, wk_ref[...])
    cost = cost_ref[...]
    sint = sint_ref[...]
    for hd in range(MLA_HEADS):
        sl = slice(hd * HEAD_PAD, (hd + 1) * HEAD_PAD)
        q_ref[0, sl, :] = ((qa[sl] * cost + qb[sl] * sint) * qscale).astype(BF16)
        k_ref[0, hd] = (kn[:, sl] + kr).astype(BF16)
    vt = lax.dot_general(wvt_ref[...], ckv, nt, preferred_element_type=F32)
    tm = vt.shape[1]
    vt = vt.reshape(MLA_HEADS, VT_ROWS, tm)
    ones = lax.broadcasted_iota(jnp.int32, vt.shape, 1) == ONES_ROW
    vt_ref[0, 0] = jnp.where(ones, 1.0, vt).astype(BF16)


def _qkv(x, sh, sc, g, w, cos_t, sin_t, tm):
    b, l, d = x.shape
    nt = l // tm
    hw = MLA_HEADS * HEAD_PAD
    qscale = LOG2E / math.sqrt(QK_NOPE + QK_ROPE)
    full = lambda a: pl.BlockSpec(a.shape, lambda bi, i: (0,) * a.ndim)
    return pl.pallas_call(
        functools.partial(_qkv_kernel, qscale=qscale),
        grid=(b, nt),
        in_specs=[
            pl.BlockSpec((1, tm, d), lambda bi, i: (bi, i, 0)),
            pl.BlockSpec((1, 1, d), lambda bi, i: (bi, 0, 0)),
            pl.BlockSpec((1, 1, d), lambda bi, i: (bi, 0, 0)),
            full(g), full(w["wd"]), full(w["gq"]), full(w["gkv"]), full(w["wqa"]), full(w["wqb"]),
            full(w["wk"]), full(w["wvt"]),
            pl.BlockSpec((tm, LANES), lambda bi, i: (i, 0)),
            pl.BlockSpec((tm, LANES), lambda bi, i: (i, 0)),
            pl.BlockSpec((LANES, tm), lambda bi, i: (0, i)),
            pl.BlockSpec((LANES, tm), lambda bi, i: (0, i)),
        ],
        out_specs=[
            pl.BlockSpec((1, hw, tm), lambda bi, i: (bi, 0, i)),
            pl.BlockSpec((1, MLA_HEADS, tm, HEAD_PAD), lambda bi, i: (bi, 0, i, 0)),
            pl.BlockSpec((1, 1, MLA_HEADS, VT_ROWS, tm), lambda bi, i: (bi, i, 0, 0, 0)),
        ],
        out_shape=[
            jax.ShapeDtypeStruct((b, hw, l), BF16),
            jax.ShapeDtypeStruct((b, MLA_HEADS, l, HEAD_PAD), BF16),
            jax.ShapeDtypeStruct((b, nt, MLA_HEADS, VT_ROWS, tm), BF16),
        ],
        compiler_params=_cparams("parallel", "parallel"),
        name="mla_qkv",
    )(x, sh, sc, g, w["wd"], w["gq"], w["gkv"], w["wqa"], w["wqb"], w["wk"], w["wvt"], cos_t, sin_t, cos_t.T, sin_t.T)


def _mla_weights(w_dq, g_q, w_uq, w_dkv, g_kv, w_ukv):
    d = w_dq.shape[0]
    hq = QK_NOPE + QK_ROPE
    half = QK_ROPE // 2
    w_rope = w_dkv[:, KV_RANK:]
    w_rope_sw = jnp.concatenate([w_rope[:, half:], w_rope[:, :half]], axis=1)
    zl = jnp.zeros((d, QK_NOPE), F32)
    zr = jnp.zeros((d, HEAD_PAD - hq), F32)
    wd = jnp.concatenate([w_dq, w_dkv[:, :KV_RANK], zl, w_rope, zr, zl, w_rope_sw, zr], axis=1)
    wq = w_uq.reshape(Q_RANK, MLA_HEADS, hq)
    zq = jnp.zeros((Q_RANK, MLA_HEADS, HEAD_PAD - hq), F32)
    wqa = jnp.concatenate([wq, zq], axis=2).reshape(Q_RANK, MLA_HEADS * HEAD_PAD)
    zn = jnp.zeros((Q_RANK, MLA_HEADS, QK_NOPE), F32)
    wqb = jnp.concatenate([zn, wq[:, :, QK_NOPE + half:], wq[:, :, QK_NOPE:QK_NOPE + half], zq],
                          axis=2).reshape(Q_RANK, MLA_HEADS * HEAD_PAD)
    wkv = w_ukv.reshape(KV_RANK, MLA_HEADS, QK_NOPE + V_DIM)
    zk = jnp.zeros((KV_RANK, MLA_HEADS, HEAD_PAD - QK_NOPE), F32)
    wk = jnp.concatenate([wkv[:, :, :QK_NOPE], zk], axis=2).reshape(KV_RANK, MLA_HEADS * HEAD_PAD)
    wv = jnp.transpose(wkv[:, :, QK_NOPE:], (1, 2, 0))
    wvt = jnp.concatenate([wv, jnp.zeros((MLA_HEADS, VT_ROWS - V_DIM, KV_RANK), F32)], axis=1)
    return {
        "wd": wd.astype(BF16), "gq": g_q.reshape(1, -1), "gkv": g_kv.reshape(1, -1),
        "wqa": wqa.T.astype(BF16), "wqb": wqb.T.astype(BF16), "wk": wk.astype(BF16),
        "wvt": wvt.reshape(MLA_HEADS * VT_ROWS, KV_RANK).astype(BF16),
    }


def _rope_tables(l):
    t = jnp.arange(l)
    row = (t // GRID_W).astype(F32)
    col = (t % GRID_W).astype(F32)
    n_freq = QK_ROPE // 4
    inv = ROPE_BASE ** (-jnp.arange(n_freq, dtype=F32) / n_freq)
    ang = jnp.concatenate([row[:, None] * inv, col[:, None] * inv], axis=-1)
    cos, sin = jnp.cos(ang), jnp.sin(ang)
    ones = jnp.ones((l, QK_NOPE), F32)
    zl = jnp.zeros((l, QK_NOPE), F32)
    zr = jnp.zeros((l, HEAD_PAD - QK_NOPE - QK_ROPE), F32)
    cos_t = jnp.concatenate([ones, cos, cos, zr], axis=1)
    sin_t = jnp.concatenate([zl, -sin, sin, zr], axis=1)
    return cos_t, sin_t


def _no_rope_tables(l):
    keep = jnp.concatenate([jnp.ones((l, QK_NOPE + QK_ROPE), F32),
                            jnp.zeros((l, HEAD_PAD - QK_NOPE - QK_ROPE), F32)], axis=1)
    return keep, jnp.zeros((l, HEAD_PAD), F32)


def _attn_ctx_kernel(q_ref, kc_ref, vtc_ref, o_ref):
    outs = []
    for hd in range(2):
        s = _dot(kc_ref[0, hd], q_ref[0, hd * HEAD_PAD:(hd + 1) * HEAD_PAD, :])
        p = jnp.exp2(s - jnp.max(s, axis=0, keepdims=True)).astype(BF16)
        acc = _dot(vtc_ref[0, 0, hd], p)
        outs.append(acc[:V_DIM] / acc[ONES_ROW:ONES_ROW + 1])
    o_ref[0] = jnp.concatenate(outs, axis=0).T.astype(BF16)


def _attn_kernel(q_ref, kc_ref, vtc_ref, k_ref, vt_ref, o_ref, *scr, n_lat_chunks, tk, unroll):
    heads = (0, 1)
    s_scr = [scr[0:2], scr[2:4]]
    p_scr = [scr[4:6], scr[6:8]]
    acc_scr = scr[8:10]
    tq = q_ref.shape[2]
    halves = [slice(0, tq // 2), slice(tq // 2, tq)]
    n_pos = n_lat_chunks + 1

    def stage_a(hd, slot, p):
        if isinstance(p, int) and p == 0:
            k = kc_ref[0, hd]
        else:
            start = (p - 1) * tk
            k = k_ref[0, hd, pl.ds(start if isinstance(p, int) else pl.multiple_of(start, tk), tk), :]
        cms = []
        for hs in halves:
            s = _dot(k, q_ref[0, hd * HEAD_PAD:(hd + 1) * HEAD_PAD, hs])
            s_scr[hd][slot][:, hs] = s
            cms.append(jnp.max(s.reshape(tk // SUBLANES, SUBLANES, s.shape[1]), axis=0))
        return jnp.concatenate(cms, axis=1)

    def stage_b(hd, slot, cm, m_old):
        m_new = jnp.maximum(m_old, jnp.max(cm, axis=0, keepdims=True))
        for hs in halves:
            p_scr[hd][slot][:, hs] = jnp.exp2(s_scr[hd][slot][:, hs] - m_new[:, hs]).astype(BF16)
        return m_new, jnp.exp2(m_old - m_new)

    def stage_c(hd, slot, p, alpha):
        vt = vtc_ref[0, 0, hd] if isinstance(p, int) and p == 0 else vt_ref[0, p - 1, hd]
        for hs in halves:
            acc_scr[hd][:, hs] = acc_scr[hd][:, hs] * alpha[:, hs] + _dot(vt, p_scr[hd][slot][:, hs])

    carry = []
    for hd in heads:
        acc_scr[hd][...] = jnp.zeros(acc_scr[hd].shape, F32)
        cm0 = stage_a(hd, 0, 0)
        cm1 = stage_a(hd, 1, 1)
        m, al = stage_b(hd, 0, cm0, jnp.full((1, tq), -jnp.inf, F32))
        carry += [m, al, cm1]

    def steps(carry, p0, parity, count):
        carry = list(carry)
        for u in range(count):
            slot = (parity + u) % 2
            for hd in heads:
                m, al, cm = carry[3 * hd:3 * hd + 3]
                stage_c(hd, slot, p0 + u, al)
                m, al = stage_b(hd, 1 - slot, cm, m)
                cm = stage_a(hd, slot, p0 + u + 2)
                carry[3 * hd:3 * hd + 3] = [m, al, cm]
        return tuple(carry)

    n_steps = n_pos - 2
    peel = n_steps % unroll or min(unroll, n_steps)
    carry = steps(carry, 0, 0, peel)
    carry = lax.fori_loop(0, (n_steps - peel) // unroll,
                          lambda j, cr: steps(cr, peel + unroll * j, peel % 2, unroll), carry)
    outs = []
    last = (n_pos - 2) % 2
    for hd in heads:
        m, al, cm = carry[3 * hd:3 * hd + 3]
        stage_c(hd, last, n_pos - 2, al)
        m, al = stage_b(hd, 1 - last, cm, m)
        stage_c(hd, 1 - last, n_pos - 1, al)
        acc = acc_scr[hd][...]
        outs.append(acc[:V_DIM] / acc[ONES_ROW:ONES_ROW + 1])
    o_ref[0] = jnp.concatenate(outs, axis=0).T.astype(BF16)


def _attention(q, kc, vtc, k=None, vt=None, *, tq):
    b, _, lq = q.shape
    cl = kc.shape[2]
    nq = lq // tq
    hp = 2 * HEAD_PAD
    in_specs = [
        pl.BlockSpec((1, hp, tq), lambda bi, h, i: (bi, h, i)),
        pl.BlockSpec((1, 2, cl, HEAD_PAD), lambda bi, h, i: (bi, h, 0, 0)),
        pl.BlockSpec((1, 1, 2, VT_ROWS, cl), lambda bi, h, i: (bi, 0, h, 0, 0)),
    ]
    args = [q, kc, vtc]
    n_chunks = 0
    scratch = []
    body = _attn_ctx_kernel
    if k is not None:
        lk = k.shape[2]
        n_chunks, tk = vt.shape[1], vt.shape[4]
        assert n_chunks >= 2 and cl == tk
        unroll = max(2, min(ATTN_UNROLL, (n_chunks - 2) // 2 * 2))
        body = functools.partial(_attn_kernel, n_lat_chunks=n_chunks, tk=tk, unroll=unroll)
        in_specs += [
            pl.BlockSpec((1, 2, lk, HEAD_PAD), lambda bi, h, i: (bi, h, 0, 0)),
            pl.BlockSpec((1, n_chunks, 2, VT_ROWS, tk), lambda bi, h, i: (bi, 0, h, 0, 0)),
        ]
        args += [k, vt]
        scratch = ([pltpu.VMEM((tk, tq), F32)] * 4 + [pltpu.VMEM((tk, tq), BF16)] * 4
                   + [pltpu.VMEM((VT_ROWS, tq), F32)] * 2)
    return pl.pallas_call(
        body,
        grid=(b, MLA_HEADS // 2, nq),
        in_specs=in_specs,
        out_specs=pl.BlockSpec((1, tq, 2 * V_DIM), lambda bi, h, i: (bi, i, h)),
        out_shape=jax.ShapeDtypeStruct((b, lq, MLA_HEADS * V_DIM), BF16),
        scratch_shapes=scratch,
        compiler_params=_cparams("parallel", "parallel", "arbitrary"),
        name="attention" if n_chunks else "attention_ctx",
    )(*args)


def _mlp_kernel(a_ref, wo_ref, bo_ref, gt0_ref, g0_ref, x_ref, sh_ref, sc_ref, gt_ref, g1_ref, g2_ref,
                w1_ref, w2_ref, o_ref):
    y = _dot(a_ref[0], wo_ref[...]) + bo_ref[...]
    x = x_ref[0] + gt0_ref[0] * _rms(y, g0_ref[...])
    h = (_rms(x, g1_ref[...]) * (1.0 + sc_ref[0]) + sh_ref[0]).astype(BF16)
    dff = w1_ref.shape[1]
    m = None
    for c in range(dff // MLP_FF_CHUNK):
        cols = slice(c * MLP_FF_CHUNK, (c + 1) * MLP_FF_CHUNK)
        u = jnp.maximum(_dot(h, w1_ref[:, cols]), 0.0)
        part = _dot((u * u).astype(BF16), w2_ref[cols, :])
        m = part if m is None else m + part
    o_ref[0] = x + gt_ref[0] * _rms(m, g2_ref[...])


def _mixer_out_mlp(a, wo, bo, gt0, g0, x, sh, sc, gt, g1, g2, w1, w2, tm):
    b, l, d = x.shape
    din = a.shape[2]
    dff = w1.shape[1]
    vec = pl.BlockSpec((1, 1, d), lambda bi, i: (bi, 0, 0))
    row = pl.BlockSpec((1, d), lambda bi, i: (0, 0))
    const = lambda shape: pl.BlockSpec(shape, lambda bi, i: (0, 0), pipeline_mode=pl.Buffered(1))
    return pl.pallas_call(
        _mlp_kernel,
        grid=(b, l // tm),
        in_specs=[
            pl.BlockSpec((1, tm, din), lambda bi, i: (bi, i, 0)),
            const((din, d)), row, vec, row,
            pl.BlockSpec((1, tm, d), lambda bi, i: (bi, i, 0)),
            vec, vec, vec, row, row,
            const((d, dff)), const((dff, d)),
        ],
        out_specs=pl.BlockSpec((1, tm, d), lambda bi, i: (bi, i, 0)),
        out_shape=jax.ShapeDtypeStruct((b, l, d), F32),
        compiler_params=_cparams("parallel", "parallel"),
        name="mixer_out_mlp",
    )(a, wo, bo, gt0, g0, x, sh, sc, gt, g1, g2, w1, w2)


def _hy_in_kernel(x_ref, xp_ref, xn_ref, sh_ref, sc_ref, g_ref, w_ref, b_ref, ws_ref, bs_ref,
                  x1_ref, x2_ref, v_ref, u_scr):
    i = pl.program_id(1)
    n = pl.num_programs(1)
    tm = x_ref.shape[1]
    d = x_ref.shape[2]
    xa = jnp.concatenate([xp_ref[0], x_ref[0], xn_ref[0]], axis=0)
    h = (_rms(xa, g_ref[...]) * (1.0 + sc_ref[0]) + sh_ref[0]).astype(BF16)
    u_scr[...] = _dot(h, w_ref[...]) + b_ref[...]

    @pl.when(i == 0)
    def _():
        u_scr[0:HALO, :] = jnp.zeros((HALO, 3 * d), F32)

    @pl.when(i == n - 1)
    def _():
        u_scr[tm + HALO:tm + 2 * HALO, :] = jnp.zeros((HALO, 3 * d), F32)

    rows = tm + 2 * HALO
    for j, o_ref in enumerate((x1_ref, x2_ref, v_ref)):
        sl = slice(j * d, (j + 1) * d)
        u = u_scr[:, sl]
        prev = pltpu.roll(u, 1, axis=0)[HALO:HALO + tm]
        nxt = pltpu.roll(u, rows - 1, axis=0)[HALO:HALO + tm]
        y = (prev * ws_ref[0:1, sl] + u[HALO:HALO + tm] * ws_ref[1:2, sl] + nxt * ws_ref[2:3, sl]
             + bs_ref[:, sl])
        o_ref[0] = y.astype(BF16)


def _hy_in(x, sh, sc, g, w_in, b_in, w_short, b_short, tm):
    b, l, d = x.shape
    nt = l // tm
    tb = tm // HALO
    nb = l // HALO
    vec = pl.BlockSpec((1, 1, d), lambda bi, i: (bi, 0, 0))
    out = pl.BlockSpec((1, tm, d), lambda bi, i: (bi, i, 0))
    return pl.pallas_call(
        _hy_in_kernel,
        grid=(b, nt),
        in_specs=[
            pl.BlockSpec((1, tm, d), lambda bi, i: (bi, i, 0)),
            pl.BlockSpec((1, HALO, d), lambda bi, i: (bi, jnp.maximum(i * tb - 1, 0), 0)),
            pl.BlockSpec((1, HALO, d), lambda bi, i: (bi, jnp.minimum((i + 1) * tb, nb - 1), 0)),
            vec, vec,
            pl.BlockSpec((1, d), lambda bi, i: (0, 0)),
            pl.BlockSpec((d, 3 * d), lambda bi, i: (0, 0)),
            pl.BlockSpec((1, 3 * d), lambda bi, i: (0, 0)),
            pl.BlockSpec((3, 3 * d), lambda bi, i: (0, 0)),
            pl.BlockSpec((1, 3 * d), lambda bi, i: (0, 0)),
        ],
        out_specs=[out, out, out],
        out_shape=[jax.ShapeDtypeStruct((b, l, d), BF16)] * 3,
        scratch_shapes=[pltpu.VMEM((tm + 2 * HALO, 3 * d), F32)],
        compiler_params=_cparams("parallel", "parallel"),
        name="hyena_in",
    )(x, x, x, sh, sc, g, w_in, b_in, w_short, b_short)


def _filt_kernel(z_ref, w1_ref, b1_ref, w2_ref, b2_ref, w3_ref, fr_ref, dl_ref, o_ref, *, l):
    tl = z_ref.shape[0]
    d = dl_ref.shape[1]
    fr = fr_ref[...]
    hdn = jnp.sin(fr * (_dot3(z_ref[...], w1_ref[...]) + b1_ref[...]))
    hdn = jnp.sin(fr * (_dot3(hdn, w2_ref[...]) + b2_ref[...]))
    hf = _dot3(hdn, w3_ref[...])
    rows = pl.program_id(0) * tl + lax.broadcasted_iota(jnp.int32, (tl, d), 0)
    t = rows.astype(F32) * (1.0 / (l - 1))
    decay = jnp.exp(-t * jnp.abs(dl_ref[...]))
    for j in range(2 * HY_ORDER):
        f = hf[:, j * d:(j + 1) * d] * decay
        if j % 2 == 1:
            f = jnp.where(rows == 0, 0.0, f)
        o_ref[j] = f.astype(BF16)


def _filters(l, d, f_w1, f_b1, f_w2, f_b2, f_w3, f_freq):
    t = jnp.linspace(0.0, 1.0, l, dtype=F32)[:, None]
    w = (2.0 * math.pi) * jnp.arange(l, dtype=F32)[:, None] / l
    f = jnp.linspace(1e-4, HY_BANDS - 1, HY_BANDS, dtype=F32)[None, :]
    kz = HY_FILTER_HIDDEN
    z = jnp.concatenate([t, jnp.cos(f * w), -jnp.sin(f * w), jnp.zeros((l, kz - HY_EMB), F32)], axis=-1)
    w1 = jnp.concatenate([f_w1, jnp.zeros((kz - HY_EMB, HY_FILTER_HIDDEN), F32)], axis=0)
    min_decay = math.log(HY_TARGET) / HY_SLOW_PCT
    max_decay = math.log(HY_TARGET) / HY_FAST_PCT
    deltas = jnp.linspace(min_decay, max_decay, d, dtype=F32)[None, :]
    tl = FILTER_ROWS
    full = lambda a: pl.BlockSpec(a.shape, lambda i: (0,) * a.ndim)
    ops = [w1, f_b1.reshape(1, -1), f_w2, f_b2.reshape(1, -1), f_w3, f_freq.reshape(1, -1), deltas]
    return pl.pallas_call(
        functools.partial(_filt_kernel, l=l),
        grid=(l // tl,),
        in_specs=[pl.BlockSpec((tl, kz), lambda i: (i, 0))] + [full(a) for a in ops],
        out_specs=pl.BlockSpec((2 * HY_ORDER, tl, d), lambda i: (0, i, 0)),
        out_shape=jax.ShapeDtypeStruct((2 * HY_ORDER, l, d), BF16),
        compiler_params=_cparams("parallel"),
        name="hyena_filters",
    )(z, *ops)


def _dft_tables(l):
    n = 2 * l
    n2 = DFT_N2
    n1 = n // n2
    n1h = n1 // 2
    slots = -(-(n1h + 1) // SLOT_BLOCK) * SLOT_BLOCK
    k1 = jnp.arange(slots)
    valid = (k1 <= n1h)[:, None]
    a = jnp.arange(n1h)
    ang1 = (2.0 * math.pi / n1) * ((k1[:, None] * a[None, :]) % n1).astype(F32)
    f1 = jnp.concatenate([jnp.where(valid, jnp.cos(ang1), 0.0), jnp.where(valid, -jnp.sin(ang1), 0.0)], axis=0)
    kk = jnp.arange(n1h)
    wgt = jnp.where(kk == 0, 1.0, 2.0)[None, :] / n
    ang2 = (2.0 * math.pi / n1) * ((a[:, None] * kk[None, :]) % n1).astype(F32)
    f2 = jnp.concatenate([wgt * jnp.cos(ang2), -wgt * jnp.sin(ang2)], axis=1)
    k2 = jnp.arange(n2)
    j = jnp.arange(n2)
    prod = (j[None, None, :] * (k1[:, None, None] + n1 * k2[None, :, None])) % n
    th = (2.0 * math.pi / n) * prod.astype(F32)
    live = valid[:, :, None]
    c, s = jnp.where(live, jnp.cos(th), 0.0), jnp.where(live, jnp.sin(th), 0.0)
    g = jnp.concatenate([jnp.concatenate([c, s], axis=2), jnp.concatenate([-s, c], axis=2)], axis=1)
    ct, st = jnp.swapaxes(c, 1, 2), jnp.swapaxes(s, 1, 2)
    gi = jnp.concatenate([jnp.concatenate([ct, -st], axis=2), jnp.concatenate([st, ct], axis=2)], axis=1)
    return {"n": n, "n1h": n1h, "slots": slots,
            "f1": f1.astype(BF16), "f2": f2.astype(BF16), "g": g.astype(BF16), "gi": gi.astype(BF16)}


def _swap_rows(x):
    return pltpu.einshape("abc->bac", x)


def _stage1_kernel(f_ref, u_ref, a_ref, xs_scr):
    jb = pl.program_id(2)
    n2, n1h, td = xs_scr.shape

    @pl.when(jb == 0)
    def _():
        xs_scr[...] = _swap_rows(u_ref[0].reshape(n1h, n2, td))

    for jj in range(J_BLOCK):
        a_ref[0, jj] = _dot(f_ref[...], xs_scr[jb * J_BLOCK + jj]).astype(BF16)


def _stage1(u, tab):
    b, l, d = u.shape
    n1h, slots = tab["n1h"], tab["slots"]
    n2 = DFT_N2
    td = STAGE1_TD
    return pl.pallas_call(
        _stage1_kernel,
        grid=(b, d // td, n2 // J_BLOCK),
        in_specs=[
            pl.BlockSpec((2 * slots, n1h), lambda bi, ci, jb: (0, 0)),
            pl.BlockSpec((1, l, td), lambda bi, ci, jb: (bi, 0, ci)),
        ],
        out_specs=pl.BlockSpec((1, J_BLOCK, 2 * slots, td), lambda bi, ci, jb: (bi, jb, 0, ci)),
        out_shape=jax.ShapeDtypeStruct((b, n2, 2 * slots, d), BF16),
        scratch_shapes=[pltpu.VMEM((n2, n1h, td), BF16)],
        compiler_params=_cparams("parallel", "parallel", "arbitrary"),
        name="dft_stage1",
    )(tab["f1"], u)


def _slot_rows(ref):
    n2, _, sb, td = ref.shape
    return _swap_rows(ref[...].reshape(n2, 2 * sb, td))


def _filt_spec_kernel(g_ref, af_ref, ab_ref, bias_ref, k_ref, f_scr, b_scr):
    n2 = f_scr.shape[1]
    f_scr[...] = _slot_rows(af_ref.at[0])
    b_scr[...] = _slot_rows(ab_ref.at[0])
    bias = bias_ref[0]

    def body(s, carry):
        g = g_ref[s]
        xf = _dot(g, jnp.concatenate([f_scr[s], f_scr[SLOT_BLOCK + s]], axis=0))
        xb = _dot(g, jnp.concatenate([b_scr[s], b_scr[SLOT_BLOCK + s]], axis=0))
        k_ref[0, 0, s] = (xf[:n2] + xb[:n2] + bias).astype(BF16)
        k_ref[0, 1, s] = (xf[n2:] - xb[n2:]).astype(BF16)
        return carry

    lax.fori_loop(0, SLOT_BLOCK, body, 0, unroll=16)


def _filt_spectrum(a_filt, bias, tab):
    _, n2, _, d = a_filt.shape
    slots = tab["slots"]
    td = MID_TD
    a5 = a_filt.reshape(a_filt.shape[0], n2, 2, slots, d)
    blk = lambda seq: pl.BlockSpec((1, n2, 2, SLOT_BLOCK, td), lambda o, kb, ci: (2 * o + seq, 0, 0, kb, ci))
    return pl.pallas_call(
        _filt_spec_kernel,
        grid=(HY_ORDER, slots // SLOT_BLOCK, d // td),
        in_specs=[
            pl.BlockSpec((SLOT_BLOCK, 2 * n2, 2 * n2), lambda o, kb, ci: (kb, 0, 0)),
            blk(0), blk(1),
            pl.BlockSpec((1, 1, td), lambda o, kb, ci: (o, 0, ci)),
        ],
        out_specs=pl.BlockSpec((1, 2, SLOT_BLOCK, n2, td), lambda o, kb, ci: (o, 0, kb, 0, ci)),
        out_shape=jax.ShapeDtypeStruct((HY_ORDER, 2, slots, n2, d), BF16),
        scratch_shapes=[pltpu.VMEM((2 * SLOT_BLOCK, n2, td), BF16)] * 2,
        compiler_params=_cparams("parallel", "parallel", "parallel"),
        name="hyena_filter_spectrum",
    )(tab["g"], a5, a5, bias.reshape(HY_ORDER, 1, d))


def _mid_kernel(g_ref, gi_ref, a_ref, k_ref, z_ref, a_scr, z_scr):
    n2 = a_scr.shape[1]
    a_scr[...] = _slot_rows(a_ref.at[0])

    def body(s, carry):
        x = _dot(g_ref[s], jnp.concatenate([a_scr[s], a_scr[SLOT_BLOCK + s]], axis=0))
        xr, xi = x[:n2], x[n2:]
        kr = k_ref[0, 0, s].astype(F32)
        ki = k_ref[0, 1, s].astype(F32)
        y = jnp.concatenate([xr * kr - xi * ki, xr * ki + xi * kr], axis=0).astype(BF16)
        z = _dot(gi_ref[s], y)
        z_scr[s] = z[:n2].astype(BF16)
        z_scr[SLOT_BLOCK + s] = z[n2:].astype(BF16)
        return carry

    lax.fori_loop(0, SLOT_BLOCK, body, 0, unroll=16)
    td = z_scr.shape[2]
    z_ref[0] = _swap_rows(z_scr[...]).reshape(n2, 2, SLOT_BLOCK, td)


def _mid(a, kf, order, tab):
    b, n2, _, d = a.shape
    slots = tab["slots"]
    td = MID_TD
    a5 = a.reshape(b, n2, 2, slots, d)
    gspec = pl.BlockSpec((SLOT_BLOCK, 2 * n2, 2 * n2), lambda bi, kb, ci: (kb, 0, 0))
    aspec = pl.BlockSpec((1, n2, 2, SLOT_BLOCK, td), lambda bi, kb, ci: (bi, 0, 0, kb, ci))
    z = pl.pallas_call(
        _mid_kernel,
        grid=(b, slots // SLOT_BLOCK, d // td),
        in_specs=[
            gspec, gspec, aspec,
            pl.BlockSpec((1, 2, SLOT_BLOCK, n2, td), lambda bi, kb, ci: (order, 0, kb, 0, ci)),
        ],
        out_specs=aspec,
        out_shape=jax.ShapeDtypeStruct((b, n2, 2, slots, d), BF16),
        scratch_shapes=[pltpu.VMEM((2 * SLOT_BLOCK, n2, td), BF16)] * 2,
        compiler_params=_cparams("parallel", "parallel", "parallel"),
        name="dft_mid",
    )(tab["g"], tab["gi"], a5, kf)
    return z.reshape(b, n2, 2 * slots, d)


def _stage2_kernel(f_ref, z_ref, x_ref, o_ref, ys_scr, *, slots, inv_n):
    jb = pl.program_id(2)
    n2, n1h, td = ys_scr.shape
    rows = lax.broadcasted_iota(jnp.int32, (n1h, td), 0)
    for jj in range(J_BLOCK):
        z = z_ref[0, jj]
        y = _dot(f_ref[...], jnp.concatenate([z[0:n1h], z[slots:slots + n1h]], axis=0))
        nyq = z[n1h:n1h + 1].astype(F32) * inv_n
        ys_scr[jb * J_BLOCK + jj] = (y + jnp.where(rows % 2 == 0, nyq, -nyq)).astype(BF16)

    @pl.when(jb == pl.num_programs(2) - 1)
    def _():
        o_ref[0] = _swap_rows(ys_scr[...]).reshape(n1h * n2, td) * x_ref[0]


def _stage2(z, gate, tab):
    b, n2, _, d = z.shape
    n1h, slots = tab["n1h"], tab["slots"]
    l = n1h * n2
    td = STAGE2_TD
    return pl.pallas_call(
        functools.partial(_stage2_kernel, slots=slots, inv_n=1.0 / tab["n"]),
        grid=(b, d // td, n2 // J_BLOCK),
        in_specs=[
            pl.BlockSpec((n1h, 2 * n1h), lambda bi, ci, jb: (0, 0)),
            pl.BlockSpec((1, J_BLOCK, 2 * slots, td), lambda bi, ci, jb: (bi, jb, 0, ci)),
            pl.BlockSpec((1, l, td), lambda bi, ci, jb: (bi, 0, ci)),
        ],
        out_specs=pl.BlockSpec((1, l, td), lambda bi, ci, jb: (bi, 0, ci)),
        out_shape=jax.ShapeDtypeStruct((b, l, d), BF16),
        scratch_shapes=[pltpu.VMEM((n2, n1h, td), BF16)],
        compiler_params=_cparams("parallel", "parallel", "arbitrary"),
        name="dft_stage2",
    )(tab["f2"], z, gate)


def _small_dft_tables(l):
    n = 2 * l
    k = jnp.arange(l)
    t = jnp.arange(l)
    ang = (2.0 * math.pi / n) * ((k[:, None] * t[None, :]) % n).astype(F32)
    sign = jnp.where(t % 2 == 0, 1.0, -1.0)[None, :]
    pad = jnp.zeros((SUBLANES - 1, l), F32)
    fwd = jnp.concatenate([jnp.cos(ang), -jnp.sin(ang), sign, pad], axis=0)
    wgt = jnp.where(k == 0, 1.0, 2.0)[None, :] / n
    ang_t = ang.T
    inv = jnp.concatenate([wgt * jnp.cos(ang_t), -wgt * jnp.sin(ang_t)], axis=1)
    return {"n": n, "fwd": fwd.astype(BF16), "inv": inv.astype(BF16)}


def _small_spec_kernel(f_ref, hf_ref, hb_ref, bias_ref, k_ref):
    l = hf_ref.shape[1]
    xf = _dot(f_ref[...], hf_ref[0])
    xb = _dot(f_ref[...], hb_ref[0])
    bias = bias_ref[0]
    k_ref[0, 0:l, :] = (xf[:l] + xb[:l] + bias).astype(BF16)
    k_ref[0, l:2 * l, :] = (xf[l:2 * l] - xb[l:2 * l]).astype(BF16)
    k_ref[0, 2 * l:, :] = (xf[2 * l:] + xb[2 * l:] + bias).astype(BF16)


def _small_filt_spectrum(filt, bias, tab):
    _, l, d = filt.shape
    rows = 2 * l + SUBLANES
    tc = SMALL_TD
    return pl.pallas_call(
        _small_spec_kernel,
        grid=(HY_ORDER, d // tc),
        in_specs=[
            pl.BlockSpec((rows, l), lambda o, j: (0, 0)),
            pl.BlockSpec((1, l, tc), lambda o, j: (2 * o, 0, j)),
            pl.BlockSpec((1, l, tc), lambda o, j: (2 * o + 1, 0, j)),
            pl.BlockSpec((1, 1, tc), lambda o, j: (o, 0, j)),
        ],
        out_specs=pl.BlockSpec((1, rows, tc), lambda o, j: (o, 0, j)),
        out_shape=jax.ShapeDtypeStruct((HY_ORDER, rows, d), BF16),
        compiler_params=_cparams("parallel", "parallel"),
        name="hyena_filter_spectrum_ctx",
    )(tab["fwd"], filt, filt, bias.reshape(HY_ORDER, 1, d))


def _small_conv_kernel(f_ref, fi_ref, u_ref, k_ref, x_ref, o_ref, *, inv_n):
    l = u_ref.shape[1]
    x = _dot(f_ref[...], u_ref[0])
    xr, xi = x[:l], x[l:2 * l]
    kr = k_ref[0, 0:l, :].astype(F32)
    ki = k_ref[0, l:2 * l, :].astype(F32)
    y = jnp.concatenate([xr * kr - xi * ki, xr * ki + xi * kr], axis=0).astype(BF16)
    out = _dot(fi_ref[...], y)
    nyq = x[2 * l:2 * l + 1] * k_ref[0, 2 * l:2 * l + 1, :].astype(F32) * inv_n
    rows = lax.broadcasted_iota(jnp.int32, out.shape, 0)
    out = out + jnp.where(rows % 2 == 0, nyq, -nyq)
    o_ref[0] = (out * x_ref[0].astype(F32)).astype(BF16)


def _small_conv(u, kf, order, gate, tab):
    b, l, d = u.shape
    rows = 2 * l + SUBLANES
    tc = SMALL_TD
    return pl.pallas_call(
        functools.partial(_small_conv_kernel, inv_n=1.0 / tab["n"]),
        grid=(b, d // tc),
        in_specs=[
            pl.BlockSpec((rows, l), lambda bi, j: (0, 0)),
            pl.BlockSpec((l, 2 * l), lambda bi, j: (0, 0)),
            pl.BlockSpec((1, l, tc), lambda bi, j: (bi, 0, j)),
            pl.BlockSpec((1, rows, tc), lambda bi, j: (order, 0, j)),
            pl.BlockSpec((1, l, tc), lambda bi, j: (bi, 0, j)),
        ],
        out_specs=pl.BlockSpec((1, l, tc), lambda bi, j: (bi, 0, j)),
        out_shape=jax.ShapeDtypeStruct((b, l, d), BF16),
        compiler_params=_cparams("parallel", "parallel"),
        name="long_conv_ctx",
    )(tab["fwd"], tab["inv"], u, kf, gate)


def _tile(l, want):
    while l % want:
        want //= 2
    return want


def _hyena_mixer_lat(x, sh, sc, g, p, tabs):
    b, l, d = x.shape
    x1, x2, v = _hy_in(x, sh, sc, g, p["w_in"], p["b_in"], p["w_short"], p["b_short"], _tile(l, ROW_TILE))
    filt = _filters(l, d, *p["filter"])
    kf = _filt_spectrum(_stage1(filt, tabs), p["bias"], tabs)
    z = _stage2(_mid(_stage1(v, tabs), kf, 0, tabs), x1, tabs)
    z = _stage2(_mid(_stage1(z, tabs), kf, 1, tabs), x2, tabs)
    return z


def _hyena_mixer_ctx(x, sh, sc, g, p, tabs):
    b, l, d = x.shape
    x1, x2, v = _hy_in(x, sh, sc, g, p["w_in"], p["b_in"], p["w_short"], p["b_short"], l)
    filt = _filters(l, d, *p["filter"])
    kf = _small_filt_spectrum(filt, p["bias"], tabs)
    z = _small_conv(v, kf, 0, x1, tabs)
    z = _small_conv(z, kf, 1, x2, tabs)
    return z


def kernel(x, c, ctx, c_ctx, ada_w, ada_b, norm_g, mla_w_dq, mla_g_q, mla_w_uq, mla_w_dkv, mla_g_kv,
           mla_w_ukv, mla_w_o, hy_w_in, hy_b_in, hy_w_short, hy_b_short, hy_f_w1, hy_f_b1, hy_f_w2,
           hy_f_b2, hy_f_w3, hy_f_freq, hy_bias, hy_w_out, hy_b_out, mlp_w1, mlp_w2):
    b, l, d = x.shape
    cl = ctx.shape[1]
    depth = ada_w.shape[0]
    assert b + 1 <= SUBLANES and l % ATTN_TK == 0 and cl == ATTN_TK

    cvec = jnp.concatenate([c, c_ctx[None, :], jnp.zeros((SUBLANES - b - 1, d), F32)], axis=0)
    mods = _ada(cvec, ada_w, ada_b)

    rope_lat = _rope_tables(l)
    rope_ctx = _no_rope_tables(cl)
    tabs_lat = _dft_tables(l)
    tabs_ctx = _small_dft_tables(cl)
    zero_bias = jnp.zeros((1, d), F32)

    xc = ctx
    for i in range(depth):
        last = i == depth - 1
        j = i // 2
        g = norm_g[i].reshape(4, 1, d)
        m_lat = mods[i, :b].reshape(b, 6, 1, d)
        m_ctx = jnp.broadcast_to(mods[i, b].reshape(1, 6, 1, d), (b, 6, 1, d))
        lat = [m_lat[:, k] for k in range(6)]
        cx = [m_ctx[:, k] for k in range(6)]

        if i % 2 == 0:
            w = _mla_weights(mla_w_dq[j], mla_g_q[j], mla_w_uq[j], mla_w_dkv[j], mla_g_kv[j], mla_w_ukv[j])
            wo = mla_w_o[j].astype(BF16)
            bo = zero_bias
            qc, kc, vtc = _qkv(xc, cx[0], cx[1], g[0], w, *rope_ctx, cl)
            ql, kl, vtl = _qkv(x, lat[0], lat[1], g[0], w, *rope_lat, ATTN_TK)
            y_lat = _attention(ql, kc, vtc, kl, vtl, tq=_tile(l, ATTN_TQ))
            y_ctx = None if last else _attention(qc, kc, vtc, tq=cl)
        else:
            p = {
                "w_in": hy_w_in[j].astype(BF16), "b_in": hy_b_in[j].reshape(1, -1),
                "w_short": hy_w_short[j], "b_short": hy_b_short[j].reshape(1, -1),
                "filter": (hy_f_w1[j], hy_f_b1[j], hy_f_w2[j], hy_f_b2[j], hy_f_w3[j], hy_f_freq[j]),
                "bias": hy_bias[j],
            }
            wo = hy_w_out[j].astype(BF16)
            bo = hy_b_out[j].reshape(1, d)
            y_lat = _hyena_mixer_lat(x, lat[0], lat[1], g[0], p, tabs_lat)
            y_ctx = None if last else _hyena_mixer_ctx(xc, cx[0], cx[1], g[0], p, tabs_ctx)

        w1 = mlp_w1[i].astype(BF16)
        w2 = mlp_w2[i].astype(BF16)
        x = _mixer_out_mlp(y_lat, wo, bo, lat[2], g[1], x, lat[3], lat[4], lat[5], g[2], g[3], w1, w2,
                           _tile(l, MLP_ROWS))
        if not last:
            xc = _mixer_out_mlp(y_ctx, wo, bo, cx[2], g[1], xc, cx[3], cx[4], cx[5], g[2], g[3], w1, w2, cl)
    return x
```
